```python
import math
import jax, jax.numpy as jnp
from jax import lax
import numpy as np

D_MODEL = 2048
BATCH = 4
SEQ = 4096
DEPTH = 4

N_EVEN = (DEPTH + 1) // 2
N_ODD = DEPTH // 2
RWKV_WIDTH = D_MODEL // 2
RWKV_HEAD = 64
RWKV_HEADS = RWKV_WIDTH // RWKV_HEAD
RWKV_LORA = 64
S5_WIDTH = D_MODEL // 2
S5_GROUP = 16
S5_GROUPS = S5_WIDTH // S5_GROUP
S5_STATE = 64
A_IN = 4 * RWKV_WIDTH + 2 * RWKV_LORA
B_IN = 2 * S5_WIDTH
EVEN_IN = A_IN + B_IN
EVEN_MIX = RWKV_WIDTH + S5_WIDTH
MLA_HEADS = 16
QK_NOPE = 128
QK_ROPE = 64
V_HEAD = 128
Q_LORA = 512
KV_LORA = 512
MLA_WIDTH = MLA_HEADS * V_HEAD
ODD_IN = Q_LORA + KV_LORA + QK_ROPE + MLA_WIDTH
ROPE_BASE = 10000.0
Q_BLOCK = 128
NORM_EPS = 1e-6
LNX_EPS = 64e-5

kernel_name = 'hybrid_rwkv7_s5_mla_trunk'


def rms_norm(x, g):
    xf = x.astype(jnp.float32)
    xf = xf * lax.rsqrt(jnp.mean(xf * xf, axis=-1, keepdims=True) + NORM_EPS)
    return xf * g.astype(jnp.float32)


def ada_modulation(c, w, b):
    m = (jax.nn.silu(c.astype(jnp.float32)) @ w + b)[:, None, :]
    return m[..., :D_MODEL], m[..., D_MODEL:2 * D_MODEL], m[..., 2 * D_MODEL:]


def token_shift(p, mu):
    prev = jnp.pad(p, ((0, 0), (1, 0), (0, 0)))[:, :-1]
    return p + (prev - p) * mu


def rwkv7_time_mix(p, mu, w0, w2, a0, a2, k_k, k_a, r_k, lnx_g, lnx_b):
    bsz, seq, _ = p.shape
    W, H, N, R = RWKV_WIDTH, RWKV_HEADS, RWKV_HEAD, RWKV_LORA
    p = token_shift(p.astype(jnp.float32), mu)
    r, k, v, g = p[..., :W], p[..., W:2 * W], p[..., 2 * W:3 * W], p[..., 3 * W:4 * W]
    w_lo, a_lo = p[..., 4 * W:4 * W + R], p[..., 4 * W + R:]
    w = -jax.nn.softplus(-(w0 + jnp.tanh(w_lo) @ w2)) - 0.5
    decay = jnp.exp(-jnp.exp(w))
    a = jax.nn.sigmoid(a0 + a_lo @ a2)
    heads = lambda t: t.reshape(bsz, seq, H, N)
    kk = heads(k * k_k)
    kk = kk * lax.rsqrt(jnp.maximum(jnp.sum(kk * kk, -1, keepdims=True), 1e-24))
    k = k * (1.0 + (a - 1.0) * k_a)
    r, k, v, decay, a = heads(r), heads(k), heads(v), heads(decay), heads(a)

    def step(state, inp):
        r_t, dec_t, k_t, v_t, rem_t, rep_t = inp
        sa = jnp.einsum('bhvk,bhk->bhv', state, rem_t)
        state = (state * dec_t[:, :, None, :] + sa[..., None] * rep_t[:, :, None, :]
                 + v_t[..., None] * k_t[:, :, None, :])
        return state, jnp.einsum('bhvk,bhk->bhv', state, r_t)

    tm = lambda t: jnp.swapaxes(t, 0, 1)
    xs = (tm(r), tm(decay), tm(k), tm(v), tm(-kk), tm(kk * a))
    s0 = jnp.zeros((bsz, H, N, N), jnp.float32)
    _, y = lax.scan(step, s0, xs)
    y = tm(y)
    mean = jnp.mean(y, -1, keepdims=True)
    var = jnp.mean(jnp.square(y - mean), -1, keepdims=True)
    y = ((y - mean) * lax.rsqrt(var + LNX_EPS)).reshape(bsz, seq, W) * lnx_g + lnx_b
    bonus = jnp.sum(r * k * r_k.reshape(H, N), -1, keepdims=True) * v
    y = y + bonus.reshape(bsz, seq, W)
    return y * jax.nn.silu(g)


def complex_linear_combine(left, right):
    a1r, a1i, b1r, b1i = left
    a2r, a2i, b2r, b2i = right
    return (a2r * a1r - a2i * a1i, a2r * a1i + a2i * a1r,
            a2r * b1r - a2i * b1i + b2r, a2r * b1i + a2i * b1r + b2i)


def s5_ssm(u, lam_re, lam_im, log_dt, b_re, b_im, c_re, c_im, d, glu_w, glu_b):
    bsz, seq, _ = u.shape
    G, P, N = S5_GROUPS, S5_GROUP, S5_STATE
    u = u.astype(jnp.float32)
    ug = u.reshape(bsz, seq, G, P)
    lam_re = lam_re.astype(jnp.float32)
    lam_im = lam_im.astype(jnp.float32)
    dt = jnp.exp(log_dt.astype(jnp.float32))[:, None]
    mag = jnp.exp(lam_re * dt)
    e_re, e_im = mag * jnp.cos(lam_im * dt), mag * jnp.sin(lam_im * dt)
    den = lam_re * lam_re + lam_im * lam_im
    coef_re = ((e_re - 1.0) * lam_re + e_im * lam_im) / den
    coef_im = (e_im * lam_re - (e_re - 1.0) * lam_im) / den
    bb_re = coef_re[..., None] * b_re - coef_im[..., None] * b_im
    bb_im = coef_re[..., None] * b_im + coef_im[..., None] * b_re
    bu_re = jnp.einsum('blgp,gnp->blgn', ug, bb_re)
    bu_im = jnp.einsum('blgp,gnp->blgn', ug, bb_im)
    a_re = jnp.broadcast_to(e_re, (1, seq, G, N))
    a_im = jnp.broadcast_to(e_im, (1, seq, G, N))
    _, _, x_re, x_im = lax.associative_scan(complex_linear_combine, (a_re, a_im, bu_re, bu_im), axis=1)
    y = jnp.einsum('gpn,blgn->blgp', c_re, x_re) - jnp.einsum('gpn,blgn->blgp', c_im, x_im)
    y = y.reshape(bsz, seq, S5_WIDTH) + d * u
    y = jax.nn.gelu(y)
    return y * jax.nn.sigmoid(y @ glu_w + glu_b)


def rope_tables(positions):
    inv_freq = 1.0 / (ROPE_BASE ** (jnp.arange(0, QK_ROPE, 2, dtype=jnp.float32) / QK_ROPE))
    ang = positions.astype(jnp.float32)[..., None] * inv_freq
    return jnp.cos(ang), jnp.sin(ang)


def apply_rope(t, cos, sin):
    half = t.shape[-1] // 2
    t1, t2 = t[..., :half], t[..., half:]
    return jnp.concatenate([t1 * cos - t2 * sin, t2 * cos + t1 * sin], axis=-1)


def mla_attention(p, cos, sin, q_norm, w_q_up, kv_norm, w_kv_up):
    bsz, seq, _ = p.shape
    H = MLA_HEADS
    p = p.astype(jnp.float32)
    o1, o2, o3 = Q_LORA, Q_LORA + KV_LORA, Q_LORA + KV_LORA + QK_ROPE
    c_q, c_kv, k_pe, g = p[..., :o1], p[..., o1:o2], p[..., o2:o3], p[..., o3:]
    q = (rms_norm(c_q, q_norm) @ w_q_up).reshape(bsz, seq, H, QK_NOPE + QK_ROPE)
    kv = (rms_norm(c_kv, kv_norm) @ w_kv_up).reshape(bsz, seq, H, QK_NOPE + V_HEAD)
    q_nope = q[..., :QK_NOPE]
    q_pe = apply_rope(q[..., QK_NOPE:], cos[:, :, None], sin[:, :, None])
    k_nope, v = kv[..., :QK_NOPE], kv[..., QK_NOPE:]
    k_pe = apply_rope(k_pe, cos, sin)
    n_blk = seq // Q_BLOCK
    blocks = lambda t: jnp.moveaxis(t.reshape(bsz, n_blk, Q_BLOCK, H, t.shape[-1]), 1, 0)
    k_idx = jnp.arange(seq)
    scale = 1.0 / math.sqrt(QK_NOPE + QK_ROPE)

    def attend(inp):
        qn, qp, blk = inp
        s = (jnp.einsum('bqhd,bkhd->bhqk', qn, k_nope)
             + jnp.einsum('bqhd,bkd->bhqk', qp, k_pe)).astype(jnp.float32)
        q_idx = blk * Q_BLOCK + jnp.arange(Q_BLOCK)
        causal = q_idx[:, None] >= k_idx[None, :]
        s = jnp.where(causal, s * scale, -1e30)
        probs = jax.nn.softmax(s, axis=-1)
        return jnp.einsum('bhqk,bkhd->bqhd', probs, v)

    o = lax.map(attend, (blocks(q_nope), blocks(q_pe), jnp.arange(n_blk)))
    o = jnp.moveaxis(o, 0, 1).reshape(bsz, seq, H * V_HEAD)
    return o * jax.nn.silu(g)


def setup_inputs(seed: int = 0) -> dict:
    key = jax.random.key(seed)
    keys = jax.random.split(key, 64)
    counter = [0]

    def nk():
        counter[0] += 1
        return keys[counter[0] - 1]

    def nrm(shape, scale):
        return scale * jax.random.normal(nk(), shape, jnp.float32)

    def gain(shape):
        return 1.0 + nrm(shape, 0.05)

    def unif(shape, lo, hi):
        return jax.random.uniform(nk(), shape, jnp.float32, lo, hi)

    D, NE, NO = D_MODEL, N_EVEN, N_ODD
    x = nrm((BATCH, SEQ, D), 1.0)
    c = nrm((BATCH, D), 1.0)
    positions = (jax.random.randint(nk(), (BATCH, 1), 0, 1024, jnp.int32)
                 + jnp.arange(SEQ, dtype=jnp.int32)[None, :])
    s5_n = jnp.pi * jnp.arange(S5_STATE, dtype=jnp.float32)
    return {
        'x': x, 'c': c, 'positions': positions,
        'ev_ada_w': nrm((NE, D, 3 * D), 0.5 * D ** -0.5),
        'ev_ada_b': nrm((NE, 3 * D), 0.01),
        'ev_norm_pre': gain((NE, D)),
        'ev_norm_post': gain((NE, D)),
        'ev_w_in': nrm((NE, D, EVEN_IN), D ** -0.5),
        'ev_mu': unif((NE, A_IN), 0.0, 1.0),
        'ev_w0': unif((NE, RWKV_WIDTH), -6.0, -1.0),
        'ev_w2': nrm((NE, RWKV_LORA, RWKV_WIDTH), 0.5 * RWKV_LORA ** -0.5),
        'ev_a0': nrm((NE, RWKV_WIDTH), 0.1),
        'ev_a2': nrm((NE, RWKV_LORA, RWKV_WIDTH), 0.5 * RWKV_LORA ** -0.5),
        'ev_k_k': 0.85 + nrm((NE, RWKV_WIDTH), 0.05),
        'ev_k_a': gain((NE, RWKV_WIDTH)),
        'ev_r_k': nrm((NE, RWKV_WIDTH), 0.1),
        'ev_lnx_g': gain((NE, RWKV_WIDTH)),
        'ev_lnx_b': nrm((NE, RWKV_WIDTH), 0.01),
        'ev_lam_re': -0.5 + nrm((NE, S5_GROUPS, S5_STATE), 0.01),
        'ev_lam_im': s5_n + nrm((NE, S5_GROUPS, S5_STATE), 0.01),
        'ev_log_dt': unif((NE, S5_GROUPS), math.log(1e-3), math.log(1e-1)),
        'ev_b_re': nrm((NE, S5_GROUPS, S5_STATE, S5_GROUP), (2.0 * S5_GROUP) ** -0.5),
        'ev_b_im': nrm((NE, S5_GROUPS, S5_STATE, S5_GROUP), (2.0 * S5_GROUP) ** -0.5),
        'ev_c_re': nrm((NE, S5_GROUPS, S5_GROUP, S5_STATE), (2.0 * S5_STATE) ** -0.5),
        'ev_c_im': nrm((NE, S5_GROUPS, S5_GROUP, S5_STATE), (2.0 * S5_STATE) ** -0.5),
        'ev_d': nrm((NE, S5_WIDTH), 1.0),
        'ev_glu_w': nrm((NE, S5_WIDTH, S5_WIDTH), S5_WIDTH ** -0.5),
        'ev_glu_b': nrm((NE, S5_WIDTH), 0.01),
        'ev_w_out': nrm((NE, EVEN_MIX, D), EVEN_MIX ** -0.5),
        'od_ada_w': nrm((NO, D, 3 * D), 0.5 * D ** -0.5),
        'od_ada_b': nrm((NO, 3 * D), 0.01),
        'od_norm_pre': gain((NO, D)),
        'od_norm_post': gain((NO, D)),
        'od_w_in': nrm((NO, D, ODD_IN), D ** -0.5),
        'od_q_norm': gain((NO, Q_LORA)),
        'od_w_q_up': nrm((NO, Q_LORA, MLA_HEADS * (QK_NOPE + QK_ROPE)), Q_LORA ** -0.5),
        'od_kv_norm': gain((NO, KV_LORA)),
        'od_w_kv_up': nrm((NO, KV_LORA, MLA_HEADS * (QK_NOPE + V_HEAD)), KV_LORA ** -0.5),
        'od_w_out': nrm((NO, MLA_WIDTH, D), MLA_WIDTH ** -0.5),
    }


def reference(x, c, positions, ev_ada_w, ev_ada_b, ev_norm_pre, ev_norm_post, ev_w_in, ev_mu,
              ev_w0, ev_w2, ev_a0, ev_a2, ev_k_k, ev_k_a, ev_r_k, ev_lnx_g, ev_lnx_b,
              ev_lam_re, ev_lam_im, ev_log_dt, ev_b_re, ev_b_im, ev_c_re, ev_c_im, ev_d,
              ev_glu_w, ev_glu_b, ev_w_out, od_ada_w, od_ada_b, od_norm_pre, od_norm_post,
              od_w_in, od_q_norm, od_w_q_up, od_kv_norm, od_w_kv_up, od_w_out):
    h = x.astype(jnp.float32)
    cos, sin = rope_tables(positions)
    for i in range(DEPTH):
        j = i // 2
        if i % 2 == 0:
            shift, scale, gate = ada_modulation(c, ev_ada_w[j], ev_ada_b[j])
            z = rms_norm(h, ev_norm_pre[j]) * (1.0 + scale) + shift
            p = z @ ev_w_in[j]
            y_a = rwkv7_time_mix(p[..., :A_IN], ev_mu[j], ev_w0[j], ev_w2[j], ev_a0[j], ev_a2[j],
                                 ev_k_k[j], ev_k_a[j], ev_r_k[j], ev_lnx_g[j], ev_lnx_b[j])
            y_b = s5_ssm(p[..., A_IN:A_IN + S5_WIDTH], ev_lam_re[j], ev_lam_im[j], ev_log_dt[j],
                         ev_b_re[j], ev_b_im[j], ev_c_re[j], ev_c_im[j], ev_d[j],
                         ev_glu_w[j], ev_glu_b[j])
            y_b = y_b * jax.nn.silu(p[..., A_IN + S5_WIDTH:].astype(jnp.float32))
            y = jnp.concatenate([y_a, y_b], axis=-1) @ ev_w_out[j]
            post = ev_norm_post[j]
        else:
            shift, scale, gate = ada_modulation(c, od_ada_w[j], od_ada_b[j])
            z = rms_norm(h, od_norm_pre[j]) * (1.0 + scale) + shift
            p = z @ od_w_in[j]
            y = mla_attention(p, cos, sin, od_q_norm[j], od_w_q_up[j], od_kv_norm[j],
                              od_w_kv_up[j]) @ od_w_out[j]
            post = od_norm_post[j]
        h = h + gate * rms_norm(y, post)
    return h.astype(x.dtype)
```

```python
import functools
import math

import jax
import jax.numpy as jnp
from jax import lax
from jax.experimental import pallas as pl
from jax.experimental.pallas import tpu as pltpu

F32 = jnp.float32
BF16 = jnp.bfloat16
HIGHEST = lax.Precision.HIGHEST

NORM_EPS = 1e-6
LNX_EPS = 64e-5
ROPE_BASE = 10000.0

RWKV_HEAD = 64
RWKV_LORA = 64
S5_GROUP = 16
S5_STATE = 64
MLA_HEADS = 16
QK_NOPE = 128
QK_ROPE = 64
V_HEAD = 128
Q_LORA = 512
KV_LORA = 512

LANES = 128
V7X_VMEM_LIMIT = 56 * 1024 * 1024

RWKV_CHUNK = 64
RWKV_TB = 256
S5_CHUNK = 64
S5_TS = 256
S5_GT = 16


def _params(sem, vmem=V7X_VMEM_LIMIT):
    return pltpu.CompilerParams(dimension_semantics=sem, vmem_limit_bytes=vmem)


def _sigmoid(x):
    return 1.0 / (1.0 + jnp.exp(-x))


def _silu(x):
    return x * _sigmoid(x)


def _dot(a, b):
    return jnp.dot(a, b, preferred_element_type=F32)


def _dot_nt(a, b):
    return lax.dot_general(a, b, (((1,), (1,)), ((), ())), preferred_element_type=F32)


def _dot_tn(a, b):
    return lax.dot_general(a, b, (((0,), (0,)), ((), ())), preferred_element_type=F32)


def _resident(shape):
    nd = len(shape)
    return pl.BlockSpec(shape, lambda *_: (0,) * nd, pipeline_mode=pl.Buffered(1))


def _ada_kernel(c_ref, w_ref, b_ref, o_ref):
    s = _silu(c_ref[...]).astype(BF16)
    o_ref[0] = _dot(s, w_ref[0].astype(BF16)) + b_ref[0]


def _ada_call(c8, w, b):
    nl, d, n3 = w.shape
    tn = 768
    return pl.pallas_call(
        _ada_kernel,
        grid=(nl, n3 // tn),
        in_specs=[pl.BlockSpec((8, d), lambda l, n: (0, 0)),
                  pl.BlockSpec((1, d, tn), lambda l, n: (l, 0, n)),
                  pl.BlockSpec((1, 1, tn), lambda l, n: (l, 0, n))],
        out_specs=pl.BlockSpec((1, 8, tn), lambda l, n: (l, 0, n)),
        out_shape=jax.ShapeDtypeStruct((nl, 8, n3), F32),
        compiler_params=_params(("parallel", "parallel")),
        name="ada",
    )(c8, w, b.reshape(nl, 1, n3))


def _norm_proj_kernel(h_ref, g_ref, sc_ref, sh_ref, w_ref, o_ref, *, tn):
    x = h_ref[0]
    ms = jnp.mean(x * x, axis=-1, keepdims=True)
    z = (x * lax.rsqrt(ms + NORM_EPS) * g_ref[...]) * (1.0 + sc_ref[0]) + sh_ref[0]
    z = z.astype(BF16)
    n = w_ref.shape[1]
    for n0 in range(0, n, tn):
        o_ref[0, :, n0:n0 + tn] = _dot(z, w_ref[:, n0:n0 + tn]).astype(o_ref.dtype)


def _norm_proj_call(h, g, scale, shift, w, tm, tn):
    bsz, seq, d = h.shape
    n = w.shape[1]
    return pl.pallas_call(
        functools.partial(_norm_proj_kernel, tn=tn),
        grid=(bsz, seq // tm),
        in_specs=[pl.BlockSpec((1, tm, d), lambda b, t: (b, t, 0)),
                  _resident((1, d)),
                  pl.BlockSpec((1, 1, d), lambda b, t: (b, 0, 0)),
                  pl.BlockSpec((1, 1, d), lambda b, t: (b, 0, 0)),
                  _resident((d, n))],
        out_specs=pl.BlockSpec((1, tm, n), lambda b, t: (b, t, 0)),
        out_shape=jax.ShapeDtypeStruct((bsz, seq, n), BF16),
        compiler_params=_params(("parallel", "parallel")),
        name="norm_proj",
    )(h, g.reshape(1, d), scale, shift, w)


def _finish(y, post_ref, gate_ref, h_ref, o_ref):
    ms = jnp.mean(y * y, axis=-1, keepdims=True)
    yn = y * lax.rsqrt(ms + NORM_EPS) * post_ref[...]
    o_ref[0] = h_ref[0] + gate_ref[0] * yn


def _out_even_kernel(ya_ref, yb_ref, w_ref, post_ref, gate_ref, h_ref, o_ref):
    half = ya_ref.shape[2]
    y = _dot(ya_ref[0], w_ref[:half, :]) + _dot(yb_ref[0], w_ref[half:, :])
    _finish(y, post_ref, gate_ref, h_ref, o_ref)


def _out_odd_kernel(o_in_ref, g_ref, w_ref, post_ref, gate_ref, h_ref, o_ref):
    yin = o_in_ref[0].astype(F32) * _silu(g_ref[0].astype(F32))
    y = _dot(yin.astype(BF16), w_ref[...])
    _finish(y, post_ref, gate_ref, h_ref, o_ref)


def _out_call(kernel, acts, act_specs, w, post, gate, h, tm):
    bsz, seq, d = h.shape
    return pl.pallas_call(
        kernel,
        grid=(bsz, seq // tm),
        in_specs=act_specs + [
            _resident(w.shape),
            _resident((1, d)),
            pl.BlockSpec((1, 1, d), lambda b, t: (b, 0, 0)),
            pl.BlockSpec((1, tm, d), lambda b, t: (b, t, 0))],
        out_specs=pl.BlockSpec((1, tm, d), lambda b, t: (b, t, 0)),
        out_shape=jax.ShapeDtypeStruct((bsz, seq, d), F32),
        input_output_aliases={len(acts) + 4 - 1: 0},
        compiler_params=_params(("parallel", "parallel")),
        name=kernel.__name__.strip("_"),
    )(*acts, w, post.reshape(1, d), gate, h)


def _rwkv_kernel(r_ref, k_ref, v_ref, g_ref, lo_ref,
                 mu_ref, mulo_ref, w0_ref, w2_ref, a0_ref, a2_ref, kk_ref, ka_ref, rk_ref,
                 lng_ref, lnb_ref, eblk_ref,
                 o_ref,
                 carry_ref, carrylo_ref, state_ref,
                 rx_ref, remx_ref, repd_ref, kd_ref, repe_ref, ke_ref, vv_ref,
                 pc_ref, y_ref, bonus_ref, sg_ref, *, tb, ch):
    width = r_ref.shape[2]
    npair = width // LANES
    nch = tb // ch

    @pl.when(pl.program_id(1) == 0)
    def _():
        carry_ref[...] = jnp.zeros_like(carry_ref)
        carrylo_ref[...] = jnp.zeros_like(carrylo_ref)
        state_ref[...] = jnp.zeros_like(state_ref)

    row0 = lax.broadcasted_iota(jnp.int32, (tb, 1), 0) == 0

    def shift(x, cref, slot, mu):
        prev = jnp.where(row0, cref[slot:slot + 1, :], pltpu.roll(x, 1, axis=0))
        cref[slot:slot + 1, :] = x[tb - 1:tb, :]
        return x + (prev - x) * mu

    r = shift(r_ref[0].astype(F32), carry_ref, 0, mu_ref[0:1, :])
    k = shift(k_ref[0].astype(F32), carry_ref, 1, mu_ref[1:2, :])
    v = shift(v_ref[0].astype(F32), carry_ref, 2, mu_ref[2:3, :])
    g = shift(g_ref[0].astype(F32), carry_ref, 3, mu_ref[3:4, :])
    lo = shift(lo_ref[0].astype(F32), carrylo_ref, 0, mulo_ref[...])
    w_lo, a_lo = lo[:, :RWKV_LORA], lo[:, RWKV_LORA:]

    x = w0_ref[...] + jnp.dot(jnp.tanh(w_lo), w2_ref[...], precision=HIGHEST,
                              preferred_element_type=F32)
    y = -x
    softplus = jnp.maximum(y, 0.0) + jnp.log(1.0 + jnp.exp(-jnp.abs(y)))
    ld = -jnp.exp(-softplus - 0.5)
    a = _sigmoid(a0_ref[...] + jnp.dot(a_lo, a2_ref[...], precision=HIGHEST,
                                        preferred_element_type=F32))

    eblk = eblk_ref[...]

    def headsum(t):
        return _dot(t.astype(BF16), eblk)

    kk = k * kk_ref[...]
    kk = kk * lax.rsqrt(jnp.maximum(headsum(kk * kk), 1e-24))
    k2 = k * (1.0 + (a - 1.0) * ka_ref[...])
    rep = kk * a
    bonus_ref[...] = headsum(r * k2 * rk_ref[...]) * v
    sg_ref[...] = _silu(g)
    vv_ref[...] = v.astype(BF16)

    ri = lax.broadcasted_iota(jnp.int32, (tb, tb), 0)
    ci = lax.broadcasted_iota(jnp.int32, (tb, tb), 1)
    same = (ri // ch) == (ci // ch)
    lincl = jnp.where(same & (ci <= ri), 1.0, 0.0).astype(F32)
    lones = jnp.where(same, 1.0, 0.0).astype(F32)
    cum = jnp.dot(lincl, ld, precision=HIGHEST, preferred_element_type=F32)
    tot = jnp.dot(lones, ld, precision=HIGHEST, preferred_element_type=F32)
    pinv = jnp.exp(-cum)
    pend = jnp.exp(tot - cum)
    rx_ref[...] = (r * jnp.exp(cum)).astype(BF16)
    remx_ref[...] = (-kk * jnp.exp(cum - ld)).astype(BF16)
    repd_ref[...] = (rep * pinv).astype(BF16)
    kd_ref[...] = (k2 * pinv).astype(BF16)
    repe_ref[...] = (rep * pend).astype(BF16)
    ke_ref[...] = (k2 * pend).astype(BF16)
    pc_ref[...] = jnp.exp(tot)

    c2 = 2 * ch
    pi = lax.broadcasted_iota(jnp.int32, (c2, c2), 0)
    pj = lax.broadcasted_iota(jnp.int32, (c2, c2), 1)
    blk = (pi // ch) == (pj // ch)
    strict = blk & (pj < pi)
    incl = blk & (pj <= pi)
    lane = lax.broadcasted_iota(jnp.int32, (ch, LANES), 1)
    m0 = lane < RWKV_HEAD
    nsq = int(math.log2(ch)) - 1

    def by_head(t):
        z = jnp.zeros_like(t)
        return jnp.concatenate([jnp.where(m0, t, z), jnp.where(m0, z, t)], axis=0)

    def chunk(c, _):
        rows = pl.ds(pl.multiple_of(c * ch, ch), ch)
        for p in range(npair):
            cols = slice(p * LANES, (p + 1) * LANES)
            rx, remx = rx_ref[rows, cols], remx_ref[rows, cols]
            repd, kd = repd_ref[rows, cols], kd_ref[rows, cols]
            repe, ke = repe_ref[rows, cols], ke_ref[rows, cols]
            vv = vv_ref[rows, cols]
            s_old = state_ref[p]
            lhs = jnp.concatenate([by_head(remx), by_head(rx)], axis=0)
            rhs = jnp.concatenate([repd, repd, kd, kd], axis=0)
            m1 = _dot_nt(lhs, rhs)
            a_bd = jnp.where(strict, m1[:c2, :c2], 0.0)
            b_bd = jnp.where(strict, m1[:c2, c2:], 0.0)
            ar_bd = jnp.where(incl, m1[c2:, :c2], 0.0)
            br_bd = jnp.where(incl, m1[c2:, c2:], 0.0)
            gs = _dot_nt(lhs, s_old.astype(BF16))
            v_bd = by_head(vv)
            sa = gs[:c2] + _dot(b_bd.astype(BF16), v_bd)
            pw = a_bd
            for i in range(nsq):
                both = _dot(pw.astype(BF16), jnp.concatenate([pw, sa], axis=1).astype(BF16))
                pw = both[:, :c2]
                sa = sa + both[:, c2:]
            sa_bd = (sa + _dot(pw.astype(BF16), sa.astype(BF16))).astype(BF16)
            vals = jnp.concatenate([sa_bd, v_bd], axis=0)
            y_bd = gs[c2:] + _dot(jnp.concatenate([ar_bd, br_bd], axis=1).astype(BF16), vals)
            y_ref[rows, cols] = y_bd[:ch] + y_bd[ch:]
            upd = _dot_tn(vals, jnp.concatenate([by_head(repe), by_head(ke)], axis=0))
            pc = pc_ref[pl.ds(pl.multiple_of(c * ch, ch), 1), cols]
            state_ref[p] = s_old * pc + upd
        return 0

    lax.fori_loop(0, nch, chunk, 0)

    yv = y_ref[...]
    inv_n = 1.0 / RWKV_HEAD
    mean = headsum(yv) * inv_n
    dlt = yv - mean
    var = headsum(dlt * dlt) * inv_n
    yn = dlt * lax.rsqrt(var + LNX_EPS) * lng_ref[...] + lnb_ref[...]
    o_ref[0] = ((yn + bonus_ref[...]) * sg_ref[...]).astype(o_ref.dtype)


def _rwkv_call(p, mu, mulo, w0, w2, a0, a2, k_k, k_a, r_k, lnx_g, lnx_b, eblk, tb, ch):
    bsz, seq, _ = p.shape
    width = w0.shape[-1]
    npair = width // LANES
    lo_blk = p.shape[-1] // LANES - 1
    act = lambda j: pl.BlockSpec((1, tb, width), lambda b, t, j=j: (b, t, j))
    row = lambda a: a.reshape(1, width)
    bf = lambda: pltpu.VMEM((tb, width), BF16)
    f32 = lambda: pltpu.VMEM((tb, width), F32)
    return pl.pallas_call(
        functools.partial(_rwkv_kernel, tb=tb, ch=ch),
        grid=(bsz, seq // tb),
        in_specs=[act(0), act(1), act(2), act(3),
                  pl.BlockSpec((1, tb, LANES), lambda b, t: (b, t, lo_blk)),
                  _resident((4, width)), _resident((1, LANES)),
                  _resident((1, width)), _resident((RWKV_LORA, width)),
                  _resident((1, width)), _resident((RWKV_LORA, width)),
                  _resident((1, width)), _resident((1, width)), _resident((1, width)),
                  _resident((1, width)), _resident((1, width)),
                  _resident((width, width))],
        out_specs=pl.BlockSpec((1, tb, width), lambda b, t: (b, t, 0)),
        out_shape=jax.ShapeDtypeStruct((bsz, seq, width), BF16),
        scratch_shapes=[pltpu.VMEM((8, width), F32), pltpu.VMEM((8, LANES), F32),
                        pltpu.VMEM((npair, LANES, LANES), F32),
                        bf(), bf(), bf(), bf(), bf(), bf(), bf(),
                        f32(), f32(), f32(), f32()],
        compiler_params=_params(("parallel", "arbitrary")),
        name="rwkv",
    )(p, p, p, p, p, mu, mulo, row(w0), w2, row(a0), a2, row(k_k), row(k_a), row(r_k),
      row(lnx_g), row(lnx_b), eblk)


def _s5_kernel(u_ref, gate_ref, bblk_ref, cblk_ref, dnr_ref, dni_ref, dpr_ref, dpi_ref,
               lre_ref, lim_ref, d_ref, gw_ref, gb_ref, o_ref, xr_ref, xi_ref, y_ref, *, ts, cs):
    @pl.when(pl.program_id(1) == 0)
    def _():
        xr_ref[...] = jnp.zeros_like(xr_ref)
        xi_ref[...] = jnp.zeros_like(xi_ref)

    ntile, ucols, scols2 = bblk_ref.shape
    scols = scols2 // 2
    nsub = ts // cs
    u_bf = u_ref[0]
    ri = lax.broadcasted_iota(jnp.int32, (ts, ts), 0)
    ci = lax.broadcasted_iota(jnp.int32, (ts, ts), 1)
    ltri = jnp.where(((ri // cs) == (ci // cs)) & (ci <= ri), 1.0, 0.0).astype(BF16)

    for t in range(ntile):
        sc = slice(t * scols, (t + 1) * scols)
        bu = _dot(u_bf[:, t * ucols:(t + 1) * ucols], bblk_ref[t])
        dnr, dni = dnr_ref[:, sc], dni_ref[:, sc]
        dpr, dpi = dpr_ref[:, sc], dpi_ref[:, sc]
        zr, zi = [], []
        for s in range(nsub):
            br = bu[s * cs:(s + 1) * cs, :scols]
            bi = bu[s * cs:(s + 1) * cs, scols:]
            zr.append(br * dnr - bi * dni)
            zi.append(br * dni + bi * dnr)
        z = jnp.concatenate([jnp.concatenate(zr, axis=0), jnp.concatenate(zi, axis=0)], axis=1)
        csum = _dot(ltri, z.astype(BF16))
        lre, lim = lre_ref[:, sc], lim_ref[:, sc]
        pr, pi_ = xr_ref[0:1, sc], xi_ref[0:1, sc]
        xr, xi = [], []
        for s in range(nsub):
            ar = lre * pr - lim * pi_
            ai = lre * pi_ + lim * pr
            cr = csum[s * cs:(s + 1) * cs, :scols] + ar
            cim = csum[s * cs:(s + 1) * cs, scols:] + ai
            xs_r = cr * dpr - cim * dpi
            xs_i = cr * dpi + cim * dpr
            pr, pi_ = xs_r[cs - 1:cs, :], xs_i[cs - 1:cs, :]
            xr.append(xs_r)
            xi.append(xs_i)
        xr_ref[0:1, sc] = pr
        xi_ref[0:1, sc] = pi_
        xfull = jnp.concatenate([jnp.concatenate(xr, axis=0), jnp.concatenate(xi, axis=0)], axis=1)
        y_ref[:, t * ucols:(t + 1) * ucols] = _dot(xfull.astype(BF16), cblk_ref[t])

    y = y_ref[...] + d_ref[...] * u_bf.astype(F32)
    cdf = 0.5 * (1.0 + jnp.tanh(math.sqrt(2.0 / math.pi) * (y + 0.044715 * (y * y * y))))
    y = y * cdf
    glu = _dot(y.astype(BF16), gw_ref[...]) + gb_ref[...]
    o_ref[0] = (y * _sigmoid(glu) * _silu(gate_ref[0].astype(F32))).astype(o_ref.dtype)


def _s5_tables(lam_re, lam_im, log_dt, b_re, b_im, c_re, c_im, cs):
    g, n = lam_re.shape
    pch = b_re.shape[-1]
    nt = g // S5_GT
    dt = jnp.exp(log_dt)[:, None]
    mag = jnp.exp(lam_re * dt)
    e_re, e_im = mag * jnp.cos(lam_im * dt), mag * jnp.sin(lam_im * dt)
    den = lam_re * lam_re + lam_im * lam_im
    coef_re = ((e_re - 1.0) * lam_re + e_im * lam_im) / den
    coef_im = (e_im * lam_re - (e_re - 1.0) * lam_im) / den
    bb_re = coef_re[..., None] * b_re - coef_im[..., None] * b_im
    bb_im = coef_re[..., None] * b_im + coef_im[..., None] * b_re
    eye = jnp.eye(S5_GT, dtype=F32)

    def btile(bb):
        return jnp.einsum('tgnq,gh->tgqhn', bb.reshape(nt, S5_GT, n, pch), eye).reshape(
            nt, S5_GT * pch, S5_GT * n)

    def ctile(cc):
        return jnp.einsum('tgpn,gh->tgnhp', cc.reshape(nt, S5_GT, pch, n), eye).reshape(
            nt, S5_GT * n, S5_GT * pch)

    bblk = jnp.concatenate([btile(bb_re), btile(bb_im)], axis=-1).astype(BF16)
    cblk = jnp.concatenate([ctile(c_re), -ctile(c_im)], axis=1).astype(BF16)
    j = jnp.arange(cs, dtype=F32)[:, None, None]
    lr, li = (lam_re * dt)[None], (lam_im * dt)[None]
    flat = lambda t: t.reshape(t.shape[0], g * n)
    dpr, dpi = flat(jnp.exp(j * lr) * jnp.cos(j * li)), flat(jnp.exp(j * lr) * jnp.sin(j * li))
    dnr, dni = flat(jnp.exp(-j * lr) * jnp.cos(j * li)), flat(-jnp.exp(-j * lr) * jnp.sin(j * li))
    return bblk, cblk, dnr, dni, dpr, dpi, e_re.reshape(1, g * n), e_im.reshape(1, g * n)


def _s5_call(p, tables, d, glu_w, glu_b, u_blk, gate_blk, ts, cs):
    bsz, seq, _ = p.shape
    width = d.shape[-1]
    bblk, cblk, dnr, dni, dpr, dpi, lre, lim = tables
    nstate = lre.shape[-1]
    res = lambda a: _resident(a.shape)
    return pl.pallas_call(
        functools.partial(_s5_kernel, ts=ts, cs=cs),
        grid=(bsz, seq // ts),
        in_specs=[pl.BlockSpec((1, ts, width), lambda b, t: (b, t, u_blk)),
                  pl.BlockSpec((1, ts, width), lambda b, t: (b, t, gate_blk)),
                  res(bblk), res(cblk), res(dnr), res(dni), res(dpr), res(dpi), res(lre), res(lim),
                  _resident((1, width)), res(glu_w), _resident((1, width))],
        out_specs=pl.BlockSpec((1, ts, width), lambda b, t: (b, t, 0)),
        out_shape=jax.ShapeDtypeStruct((bsz, seq, width), BF16),
        scratch_shapes=[pltpu.VMEM((8, nstate), F32), pltpu.VMEM((8, nstate), F32),
                        pltpu.VMEM((ts, width), F32)],
        compiler_params=_params(("parallel", "arbitrary")),
        name="s5",
    )(p, p, bblk, cblk, dnr, dni, dpr, dpi, lre, lim, d.reshape(1, width), glu_w,
      glu_b.reshape(1, width))


def _rope128(x, cos, sin):
    lane = lax.broadcasted_iota(jnp.int32, x.shape, 1)
    half = QK_ROPE // 2
    partner = jnp.where(lane < half, pltpu.roll(x, LANES - half, axis=1), pltpu.roll(x, half, axis=1))
    return x * cos + partner * sin


def _qkv_kernel(cq_ref, ckv_ref, kpe_ref, cos_ref, sin_ref, qn_ref, kvn_ref, wq_ref, wkv_ref,
                q_ref, k_ref, v_ref, *, scale):
    def latent(ref, gain_ref):
        x = ref[0].astype(F32)
        ms = jnp.mean(x * x, axis=-1, keepdims=True)
        return (x * lax.rsqrt(ms + NORM_EPS) * gain_ref[...]).astype(BF16)

    cq = latent(cq_ref, qn_ref)
    ckv = latent(ckv_ref, kvn_ref)
    cos, sin = cos_ref[0], sin_ref[0]
    kpe = _rope128(kpe_ref[0].astype(F32), cos, sin).astype(BF16)
    hw = 2 * LANES
    for h in range(q_ref.shape[1]):
        qh = _dot(cq, wq_ref[:, h * hw:(h + 1) * hw]) * scale
        q_ref[0, h] = jnp.concatenate(
            [qh[:, :LANES], _rope128(qh[:, LANES:], cos, sin)], axis=1).astype(BF16)
        kvh = _dot(ckv, wkv_ref[:, h * hw:(h + 1) * hw])
        k_ref[0, h] = jnp.concatenate([kvh[:, :LANES].astype(BF16), kpe], axis=1)
        v_ref[0, h] = kvh[:, LANES:].astype(BF16)


def _qkv_call(p, cos, sin, q_norm, kv_norm, wq, wkv, cq_blk, ckv_blk, kpe_blk, tm):
    bsz, seq, _ = p.shape
    nh = MLA_HEADS
    hw = 2 * LANES
    scale = 1.0 / math.sqrt(QK_NOPE + QK_ROPE)
    head_out = lambda w: pl.BlockSpec((1, nh, tm, w), lambda b, t: (b, 0, t, 0))
    return pl.pallas_call(
        functools.partial(_qkv_kernel, scale=scale),
        grid=(bsz, seq // tm),
        in_specs=[pl.BlockSpec((1, tm, Q_LORA), lambda b, t: (b, t, cq_blk)),
                  pl.BlockSpec((1, tm, KV_LORA), lambda b, t: (b, t, ckv_blk)),
                  pl.BlockSpec((1, tm, LANES), lambda b, t: (b, t, kpe_blk)),
                  pl.BlockSpec((1, tm, LANES), lambda b, t: (b, t, 0)),
                  pl.BlockSpec((1, tm, LANES), lambda b, t: (b, t, 0)),
                  _resident((1, Q_LORA)), _resident((1, KV_LORA)),
                  _resident(wq.shape), _resident(wkv.shape)],
        out_specs=[head_out(hw), head_out(hw), head_out(V_HEAD)],
        out_shape=[jax.ShapeDtypeStruct((bsz, nh, seq, hw), BF16),
                   jax.ShapeDtypeStruct((bsz, nh, seq, hw), BF16),
                   jax.ShapeDtypeStruct((bsz, nh, seq, V_HEAD), BF16)],
        compiler_params=_params(("parallel", "parallel")),
        name="qkv",
    )(p, p, p, cos, sin, q_norm.reshape(1, Q_LORA), kv_norm.reshape(1, KV_LORA), wq, wkv)


def _attn_kernel(q_ref, k_ref, v_ref, o_ref, *, tq, tk):
    qi = pl.program_id(2)
    q = q_ref[0, 0]

    def step(j, carry, masked):
        m, l, acc = carry
        rows = pl.ds(pl.multiple_of(j * tk, tk), tk)
        s = _dot_nt(q, k_ref[0, 0, rows, :])
        if masked:
            qpos = qi * tq + lax.broadcasted_iota(jnp.int32, (tq, tk), 0)
            kpos = j * tk + lax.broadcasted_iota(jnp.int32, (tq, tk), 1)
            s = jnp.where(qpos >= kpos, s, -1e30)
        m_new = jnp.maximum(m, jnp.max(s, axis=-1, keepdims=True))
        pexp = jnp.exp(s - m_new)
        alpha = jnp.exp(m - m_new)
        l = alpha * l + jnp.sum(pexp, axis=-1, keepdims=True)
        acc = alpha * acc + _dot(pexp.astype(BF16), v_ref[0, 0, rows, :])
        return m_new, l, acc

    init = (jnp.full((tq, 1), -1e30, F32), jnp.zeros((tq, 1), F32),
            jnp.zeros((tq, v_ref.shape[3]), F32))
    nfull = qi * (tq // tk)
    carry = lax.fori_loop(0, nfull, lambda j, c: step(j, c, False), init)
    for jj in range(tq // tk):
        carry = step(nfull + jj, carry, True)
    _, l, acc = carry
    o_ref[0] = (acc / l).astype(o_ref.dtype)


def _attn_call(q, k, v, tq, tk):
    bsz, nh, seq, hw = q.shape
    vd = v.shape[3]
    return pl.pallas_call(
        functools.partial(_attn_kernel, tq=tq, tk=tk),
        grid=(bsz, nh, seq // tq),
        in_specs=[pl.BlockSpec((1, 1, tq, hw), lambda b, h, i: (b, h, i, 0)),
                  pl.BlockSpec((1, 1, seq, hw), lambda b, h, i: (b, h, 0, 0)),
                  pl.BlockSpec((1, 1, seq, vd), lambda b, h, i: (b, h, 0, 0))],
        out_specs=pl.BlockSpec((1, tq, vd), lambda b, h, i: (b, i, h)),
        out_shape=jax.ShapeDtypeStruct((bsz, seq, nh * vd), BF16),
        compiler_params=_params(("parallel", "parallel", "arbitrary")),
        name="attn",
    )(q, k, v)


def _rope_tables(positions):
    inv_freq = 1.0 / (ROPE_BASE ** (jnp.arange(0, QK_ROPE, 2, dtype=F32) / QK_ROPE))
    ang = positions.astype(F32)[..., None] * inv_freq
    cos, sin = jnp.cos(ang), jnp.sin(ang)
    zero = jnp.zeros_like(cos)
    return (jnp.concatenate([cos, cos, zero, zero], axis=-1),
            jnp.concatenate([-sin, sin, zero, zero], axis=-1))


def _modulation(ada, j, bsz, d):
    m = ada[j, :bsz]
    part = lambda i: m[:, i * d:(i + 1) * d].reshape(bsz, 1, d)
    return part(0), part(1), part(2)


def kernel(x, c, positions, ev_ada_w, ev_ada_b, ev_norm_pre, ev_norm_post, ev_w_in, ev_mu, ev_w0, ev_w2, ev_a0, ev_a2, ev_k_k, ev_k_a, ev_r_k, ev_lnx_g, ev_lnx_b, ev_lam_re, ev_lam_im, ev_log_dt, ev_b_re, ev_b_im, ev_c_re, ev_c_im, ev_d, ev_glu_w, ev_glu_b, ev_w_out, od_ada_w, od_ada_b, od_norm_pre, od_norm_post, od_w_in, od_q_norm, od_w_q_up, od_kv_norm, od_w_kv_up, od_w_out):
    bsz, seq, d = x.shape
    n_even, n_odd = ev_w_in.shape[0], od_w_in.shape[0]
    depth = n_even + n_odd
    rw = ev_w0.shape[-1]
    sw = ev_d.shape[-1]
    a_in = 4 * rw + 2 * RWKV_LORA
    h = x.astype(F32)

    c8 = jnp.zeros((8, d), F32).at[:bsz].set(c.astype(F32))
    ev_ada = _ada_call(c8, ev_ada_w, ev_ada_b)
    od_ada = _ada_call(c8, od_ada_w, od_ada_b)
    cos, sin = _rope_tables(positions)
    eblk = jnp.kron(jnp.eye(rw // RWKV_HEAD, dtype=F32),
                    jnp.ones((RWKV_HEAD, RWKV_HEAD), F32)).astype(BF16)

    tm_out = min(512, seq)
    for i in range(depth):
        j = i // 2
        if i % 2 == 0:
            shift, scale, gate = _modulation(ev_ada, j, bsz, d)
            w = ev_w_in[j]
            w_in = jnp.concatenate([w[:, :4 * rw], w[:, a_in:], w[:, 4 * rw:a_in]], axis=1).astype(BF16)
            p = _norm_proj_call(h, ev_norm_pre[j], scale, shift, w_in, tm=min(256, seq), tn=896)
            mu = ev_mu[j]
            y_a = _rwkv_call(p, mu[:4 * rw].reshape(4, rw), mu[4 * rw:].reshape(1, 2 * RWKV_LORA),
                             ev_w0[j], ev_w2[j], ev_a0[j], ev_a2[j], ev_k_k[j], ev_k_a[j],
                             ev_r_k[j], ev_lnx_g[j], ev_lnx_b[j], eblk,
                             tb=min(RWKV_TB, seq), ch=RWKV_CHUNK)
            tables = _s5_tables(ev_lam_re[j], ev_lam_im[j], ev_log_dt[j], ev_b_re[j], ev_b_im[j],
                                ev_c_re[j], ev_c_im[j], S5_CHUNK)
            y_b = _s5_call(p, tables, ev_d[j], ev_glu_w[j].astype(BF16), ev_glu_b[j],
                           u_blk=(4 * rw) // sw, gate_blk=(4 * rw) // sw + 1,
                           ts=min(S5_TS, seq), cs=S5_CHUNK)
            half = lambda t: pl.BlockSpec((1, tm_out, rw), lambda b, t_: (b, t_, 0))
            h = _out_call(_out_even_kernel, [y_a, y_b], [half(0), half(1)],
                          ev_w_out[j].astype(BF16), ev_norm_post[j], gate, h, tm_out)
        else:
            shift, scale, gate = _modulation(od_ada, j, bsz, d)
            w = od_w_in[j]
            o1, o2, o3 = Q_LORA, Q_LORA + KV_LORA, Q_LORA + KV_LORA + QK_ROPE
            mw = w.shape[1] - o3
            w_in = jnp.concatenate([w[:, o3:], w[:, :o3],
                                    jnp.zeros((d, LANES - QK_ROPE), w.dtype)], axis=1).astype(BF16)
            p = _norm_proj_call(h, od_norm_pre[j], scale, shift, w_in, tm=min(512, seq), tn=640)
            nh = MLA_HEADS
            wq = od_w_q_up[j].reshape(Q_LORA, nh, QK_NOPE + QK_ROPE)
            wq = jnp.pad(wq, ((0, 0), (0, 0), (0, 2 * LANES - QK_NOPE - QK_ROPE)))
            wq = wq.reshape(Q_LORA, nh * 2 * LANES).astype(BF16)
            wkv = od_w_kv_up[j].astype(BF16)
            q, k, v = _qkv_call(p, cos, sin, od_q_norm[j], od_kv_norm[j], wq, wkv,
                                cq_blk=mw // Q_LORA, ckv_blk=mw // KV_LORA + 1,
                                kpe_blk=(mw + o2) // LANES, tm=min(512, seq))
            o = _attn_call(q, k, v, tq=min(512, seq), tk=min(512, seq))
            full = lambda jblk: pl.BlockSpec((1, tm_out, mw), lambda b, t_, jblk=jblk: (b, t_, jblk))
            h = _out_call(_out_odd_kernel, [o, p], [full(0), full(0)],
                          od_w_out[j].astype(BF16), od_norm_post[j], gate, h, tm_out)
    return h.astype(x.dtype)
```

```python
import functools
import math

import jax
import jax.numpy as jnp
from jax import lax
from jax.experimental import pallas as pl
from jax.experimental.pallas import tpu as pltpu

F32 = jnp.float32
BF16 = jnp.bfloat16
HIGHEST = lax.Precision.HIGHEST

NORM_EPS = 1e-6
LNX_EPS = 64e-5
ROPE_BASE = 10000.0

RWKV_HEAD = 64
RWKV_LORA = 64
S5_GROUP = 16
S5_STATE = 64
MLA_HEADS = 16
QK_NOPE = 128
QK_ROPE = 64
V_HEAD = 128
Q_LORA = 512
KV_LORA = 512

LANES = 128
V7X_VMEM_LIMIT = 56 * 1024 * 1024

RWKV_CHUNK = 64
RWKV_TB = 256
S5_CHUNK = 64
S5_TS = 256
S5_GT = 16


def _params(sem, vmem=V7X_VMEM_LIMIT):
    return pltpu.CompilerParams(dimension_semantics=sem, vmem_limit_bytes=vmem)


def _sigmoid(x):
    return 1.0 / (1.0 + jnp.exp(-x))


def _silu(x):
    return x * _sigmoid(x)


def _dot(a, b):
    return jnp.dot(a, b, preferred_element_type=F32)


def _dot_nt(a, b):
    return lax.dot_general(a, b, (((1,), (1,)), ((), ())), preferred_element_type=F32)


def _dot_tn(a, b):
    return lax.dot_general(a, b, (((0,), (0,)), ((), ())), preferred_element_type=F32)


def _resident(shape):
    nd = len(shape)
    return pl.BlockSpec(shape, lambda *_: (0,) * nd, pipeline_mode=pl.Buffered(1))


def _ada_kernel(c_ref, w_ref, b_ref, o_ref):
    s = _silu(c_ref[...]).astype(BF16)
    o_ref[0] = _dot(s, w_ref[0].astype(BF16)) + b_ref[0]


def _ada_call(c8, w, b):
    nl, d, n3 = w.shape
    tn = 768
    return pl.pallas_call(
        _ada_kernel,
        grid=(nl, n3 // tn),
        in_specs=[pl.BlockSpec((8, d), lambda l, n: (0, 0)),
                  pl.BlockSpec((1, d, tn), lambda l, n: (l, 0, n)),
                  pl.BlockSpec((1, 1, tn), lambda l, n: (l, 0, n))],
        out_specs=pl.BlockSpec((1, 8, tn), lambda l, n: (l, 0, n)),
        out_shape=jax.ShapeDtypeStruct((nl, 8, n3), F32),
        compiler_params=_params(("parallel", "parallel")),
        name="ada",
    )(c8, w, b.reshape(nl, 1, n3))


def _norm_proj_kernel(h_ref, g_ref, sc_ref, sh_ref, w_ref, o_ref, *, tn):
    x = h_ref[0]
    ms = jnp.mean(x * x, axis=-1, keepdims=True)
    z = (x * lax.rsqrt(ms + NORM_EPS) * g_ref[...]) * (1.0 + sc_ref[0]) + sh_ref[0]
    z = z.astype(BF16)
    n = w_ref.shape[1]
    for n0 in range(0, n, tn):
        o_ref[0, :, n0:n0 + tn] = _dot(z, w_ref[:, n0:n0 + tn]).astype(o_ref.dtype)


def _norm_proj_call(h, g, scale, shift, w, tm, tn):
    bsz, seq, d = h.shape
    n = w.shape[1]
    return pl.pallas_call(
        functools.partial(_norm_proj_kernel, tn=tn),
        grid=(bsz, seq // tm),
        in_specs=[pl.BlockSpec((1, tm, d), lambda b, t: (b, t, 0)),
                  _resident((1, d)),
                  pl.BlockSpec((1, 1, d), lambda b, t: (b, 0, 0)),
                  pl.BlockSpec((1, 1, d), lambda b, t: (b, 0, 0)),
                  _resident((d, n))],
        out_specs=pl.BlockSpec((1, tm, n), lambda b, t: (b, t, 0)),
        out_shape=jax.ShapeDtypeStruct((bsz, seq, n), BF16),
        compiler_params=_params(("parallel", "parallel")),
        name="norm_proj",
    )(h, g.reshape(1, d), scale, shift, w)


def _finish(y, post_ref, gate_ref, h_ref, o_ref):
    ms = jnp.mean(y * y, axis=-1, keepdims=True)
    yn = y * lax.rsqrt(ms + NORM_EPS) * post_ref[...]
    o_ref[0] = h_ref[0] + gate_ref[0] * yn


def _out_even_kernel(ya_ref, yb_ref, w_ref, post_ref, gate_ref, h_ref, o_ref):
    half = ya_ref.shape[2]
    y = _dot(ya_ref[0], w_ref[:half, :]) + _dot(yb_ref[0], w_ref[half:, :])
    _finish(y, post_ref, gate_ref, h_ref, o_ref)


def _out_odd_kernel(o_in_ref, g_ref, w_ref, post_ref, gate_ref, h_ref, o_ref):
    yin = o_in_ref[0].astype(F32) * _silu(g_ref[0].astype(F32))
    y = _dot(yin.astype(BF16), w_ref[...])
    _finish(y, post_ref, gate_ref, h_ref, o_ref)


def _out_call(kernel, acts, act_specs, w, post, gate, h, tm):
    bsz, seq, d = h.shape
    return pl.pallas_call(
        kernel,
        grid=(bsz, seq // tm),
        in_specs=act_specs + [
            _resident(w.shape),
            _resident((1, d)),
            pl.BlockSpec((1, 1, d), lambda b, t: (b, 0, 0)),
            pl.BlockSpec((1, tm, d), lambda b, t: (b, t, 0))],
        out_specs=pl.BlockSpec((1, tm, d), lambda b, t: (b, t, 0)),
        out_shape=jax.ShapeDtypeStruct((bsz, seq, d), F32),
        input_output_aliases={len(acts) + 4 - 1: 0},
        compiler_params=_params(("parallel", "parallel")),
        name=kernel.__name__.strip("_"),
    )(*acts, w, post.reshape(1, d), gate, h)


def _rwkv_kernel(r_ref, k_ref, v_ref, g_ref, lo_ref,
                 mu_ref, mulo_ref, w0_ref, w2_ref, a0_ref, a2_ref, kk_ref, ka_ref, rk_ref,
                 lng_ref, lnb_ref, eblk_ref,
                 o_ref,
                 carry_ref, carrylo_ref, state_ref,
                 rx_ref, remx_ref, repd_ref, kd_ref, repe_ref, ke_ref, vv_ref,
                 pc_ref, y_ref, bonus_ref, sg_ref, *, tb, ch):
    width = r_ref.shape[2]
    npair = width // LANES
    nch = tb // ch

    @pl.when(pl.program_id(1) == 0)
    def _():
        carry_ref[...] = jnp.zeros_like(carry_ref)
        carrylo_ref[...] = jnp.zeros_like(carrylo_ref)
        state_ref[...] = jnp.zeros_like(state_ref)

    row0 = lax.broadcasted_iota(jnp.int32, (tb, 1), 0) == 0

    def shift(x, cref, slot, mu):
        prev = jnp.where(row0, cref[slot:slot + 1, :], pltpu.roll(x, 1, axis=0))
        cref[slot:slot + 1, :] = x[tb - 1:tb, :]
        return x + (prev - x) * mu

    r = shift(r_ref[0].astype(F32), carry_ref, 0, mu_ref[0:1, :])
    k = shift(k_ref[0].astype(F32), carry_ref, 1, mu_ref[1:2, :])
    v = shift(v_ref[0].astype(F32), carry_ref, 2, mu_ref[2:3, :])
    g = shift(g_ref[0].astype(F32), carry_ref, 3, mu_ref[3:4, :])
    lo = shift(lo_ref[0].astype(F32), carrylo_ref, 0, mulo_ref[...])
    w_lo, a_lo = lo[:, :RWKV_LORA], lo[:, RWKV_LORA:]

    x = w0_ref[...] + _dot(jnp.tanh(w_lo).astype(BF16), w2_ref[...])
    y = -x
    softplus = jnp.maximum(y, 0.0) + jnp.log(1.0 + jnp.exp(-jnp.abs(y)))
    ld = -jnp.exp(-softplus - 0.5)
    a = _sigmoid(a0_ref[...] + _dot(a_lo.astype(BF16), a2_ref[...]))

    eblk = eblk_ref[...]

    def headsum(t):
        return _dot(t.astype(BF16), eblk)

    kk = k * kk_ref[...]
    kk = kk * lax.rsqrt(jnp.maximum(headsum(kk * kk), 1e-24))
    k2 = k * (1.0 + (a - 1.0) * ka_ref[...])
    rep = kk * a
    bonus_ref[...] = headsum(r * k2 * rk_ref[...]) * v
    sg_ref[...] = _silu(g)
    vv_ref[...] = v.astype(BF16)

    ri = lax.broadcasted_iota(jnp.int32, (tb, tb), 0)
    ci = lax.broadcasted_iota(jnp.int32, (tb, tb), 1)
    lincl = jnp.where(((ri // ch) == (ci // ch)) & (ci <= ri), 1.0, 0.0).astype(BF16)
    hi = ld.astype(BF16)
    r1 = ld - hi.astype(F32)
    mid = r1.astype(BF16)
    low = (r1 - mid.astype(F32)).astype(BF16)
    parts = _dot(lincl, jnp.concatenate([hi, mid, low], axis=1))
    cum = parts[:, :width] + parts[:, width:2 * width] + parts[:, 2 * width:]
    pinv = jnp.exp(-cum)
    pend = []
    for s in range(nch):
        pc = jnp.exp(cum[(s + 1) * ch - 1:(s + 1) * ch, :])
        pc_ref[8 * s:8 * s + 8, :] = jnp.broadcast_to(pc, (8, width))
        pend.append(pc * pinv[s * ch:(s + 1) * ch, :])
    pend = jnp.concatenate(pend, axis=0)
    rx_ref[...] = (r * jnp.exp(cum)).astype(BF16)
    remx_ref[...] = (-kk * jnp.exp(cum - ld)).astype(BF16)
    repd_ref[...] = (rep * pinv).astype(BF16)
    kd_ref[...] = (k2 * pinv).astype(BF16)
    repe_ref[...] = (rep * pend).astype(BF16)
    ke_ref[...] = (k2 * pend).astype(BF16)

    c2 = 2 * ch
    pi = lax.broadcasted_iota(jnp.int32, (c2, c2), 0)
    pj = lax.broadcasted_iota(jnp.int32, (c2, c2), 1)
    blk = (pi // ch) == (pj // ch)
    strict = blk & (pj < pi)
    incl = blk & (pj <= pi)
    lane = lax.broadcasted_iota(jnp.int32, (ch, LANES), 1)
    m0 = lane < RWKV_HEAD
    nsq = int(math.log2(ch)) - 1

    def by_head(t):
        z = jnp.zeros_like(t)
        return jnp.concatenate([jnp.where(m0, t, z), jnp.where(m0, z, t)], axis=0)

    def chunk(c, _):
        rows = pl.ds(pl.multiple_of(c * ch, ch), ch)
        pairs = range(npair)
        cols = [slice(p * LANES, (p + 1) * LANES) for p in pairs]
        cat = jnp.concatenate
        lhs = [cat([by_head(remx_ref[rows, cols[p]]), by_head(rx_ref[rows, cols[p]])], axis=0)
               for p in pairs]
        m1 = [_dot_nt(lhs[p], cat([repd_ref[rows, cols[p]]] * 2 + [kd_ref[rows, cols[p]]] * 2, axis=0))
              for p in pairs]
        s_old = [state_ref[p] for p in pairs]
        gs = [_dot_nt(lhs[p], s_old[p].astype(BF16)) for p in pairs]
        v_bd = [by_head(vv_ref[rows, cols[p]]) for p in pairs]
        sa = [gs[p][:c2] + _dot(jnp.where(strict, m1[p][:c2, c2:], 0.0).astype(BF16), v_bd[p])
              for p in pairs]
        pw = [jnp.where(strict, m1[p][:c2, :c2], 0.0) for p in pairs]
        for _i in range(nsq):
            both = [_dot(pw[p].astype(BF16), cat([pw[p], sa[p]], axis=1).astype(BF16)) for p in pairs]
            pw = [both[p][:, :c2] for p in pairs]
            sa = [sa[p] + both[p][:, c2:] for p in pairs]
        sa = [(sa[p] + _dot(pw[p].astype(BF16), sa[p].astype(BF16))).astype(BF16) for p in pairs]
        vals = [cat([sa[p], v_bd[p]], axis=0) for p in pairs]
        for p in pairs:
            arbr = cat([jnp.where(incl, m1[p][c2:, :c2], 0.0), jnp.where(incl, m1[p][c2:, c2:], 0.0)],
                       axis=1).astype(BF16)
            y_bd = gs[p][c2:] + _dot(arbr, vals[p])
            y_ref[rows, cols[p]] = y_bd[:ch] + y_bd[ch:]
        for p in pairs:
            upd = _dot_tn(vals[p], cat([by_head(repe_ref[rows, cols[p]]), by_head(ke_ref[rows, cols[p]])],
                                       axis=0))
            pc = pc_ref[pl.ds(pl.multiple_of(c * 8, 8), 8), cols[p]][0:1, :]
            state_ref[p] = s_old[p] * pc + upd
        return 0

    lax.fori_loop(0, nch, chunk, 0)

    yv = y_ref[...]
    inv_n = 1.0 / RWKV_HEAD
    mean = headsum(yv) * inv_n
    dlt = yv - mean
    var = headsum(dlt * dlt) * inv_n
    yn = dlt * lax.rsqrt(var + LNX_EPS) * lng_ref[...] + lnb_ref[...]
    o_ref[0] = ((yn + bonus_ref[...]) * sg_ref[...]).astype(o_ref.dtype)


def _rwkv_call(p, mu, mulo, w0, w2, a0, a2, k_k, k_a, r_k, lnx_g, lnx_b, eblk, tb, ch):
    bsz, seq, _ = p.shape
    width = w0.shape[-1]
    npair = width // LANES
    lo_blk = p.shape[-1] // LANES - 1
    act = lambda j: pl.BlockSpec((1, tb, width), lambda b, t, j=j: (b, t, j))
    row = lambda a: a.reshape(1, width)
    bf = lambda: pltpu.VMEM((tb, width), BF16)
    f32 = lambda: pltpu.VMEM((tb, width), F32)
    return pl.pallas_call(
        functools.partial(_rwkv_kernel, tb=tb, ch=ch),
        grid=(bsz, seq // tb),
        in_specs=[act(0), act(1), act(2), act(3),
                  pl.BlockSpec((1, tb, LANES), lambda b, t: (b, t, lo_blk)),
                  _resident((4, width)), _resident((1, LANES)),
                  _resident((1, width)), _resident((RWKV_LORA, width)),
                  _resident((1, width)), _resident((RWKV_LORA, width)),
                  _resident((1, width)), _resident((1, width)), _resident((1, width)),
                  _resident((1, width)), _resident((1, width)),
                  _resident((width, width))],
        out_specs=pl.BlockSpec((1, tb, width), lambda b, t: (b, t, 0)),
        out_shape=jax.ShapeDtypeStruct((bsz, seq, width), BF16),
        scratch_shapes=[pltpu.VMEM((8, width), F32), pltpu.VMEM((8, LANES), F32),
                        pltpu.VMEM((npair, LANES, LANES), F32),
                        bf(), bf(), bf(), bf(), bf(), bf(), bf(),
                        pltpu.VMEM((8 * (tb // ch), width), F32), f32(), f32(), f32()],
        compiler_params=_params(("parallel", "arbitrary")),
        name="rwkv",
    )(p, p, p, p, p, mu, mulo, row(w0), w2, row(a0), a2, row(k_k), row(k_a), row(r_k),
      row(lnx_g), row(lnx_b), eblk)


def _s5_kernel(u_ref, gate_ref, bblk_ref, cblk_ref, dnr_ref, dni_ref, dpr_ref, dpi_ref,
               lre_ref, lim_ref, d_ref, gw_ref, gb_ref, o_ref, xr_ref, xi_ref, y_ref, *, ts, cs):
    @pl.when(pl.program_id(1) == 0)
    def _():
        xr_ref[...] = jnp.zeros_like(xr_ref)
        xi_ref[...] = jnp.zeros_like(xi_ref)

    ntile, ucols, scols2 = bblk_ref.shape
    scols = scols2 // 2
    nsub = ts // cs
    u_bf = u_ref[0]
    ri = lax.broadcasted_iota(jnp.int32, (ts, ts), 0)
    ci = lax.broadcasted_iota(jnp.int32, (ts, ts), 1)
    ltri = jnp.where(((ri // cs) == (ci // cs)) & (ci <= ri), 1.0, 0.0).astype(BF16)

    for t in range(ntile):
        sc = slice(t * scols, (t + 1) * scols)
        bu = _dot(u_bf[:, t * ucols:(t + 1) * ucols], bblk_ref[t])
        dnr, dni = dnr_ref[:, sc], dni_ref[:, sc]
        dpr, dpi = dpr_ref[:, sc], dpi_ref[:, sc]
        zr, zi = [], []
        for s in range(nsub):
            br = bu[s * cs:(s + 1) * cs, :scols]
            bi = bu[s * cs:(s + 1) * cs, scols:]
            zr.append(br * dnr - bi * dni)
            zi.append(br * dni + bi * dnr)
        z = jnp.concatenate([jnp.concatenate(zr, axis=0), jnp.concatenate(zi, axis=0)], axis=1)
        csum = _dot(ltri, z.astype(BF16))
        lre, lim = lre_ref[:, sc], lim_ref[:, sc]
        pr, pi_ = xr_ref[0:1, sc], xi_ref[0:1, sc]
        xr, xi = [], []
        for s in range(nsub):
            ar = lre * pr - lim * pi_
            ai = lre * pi_ + lim * pr
            cr = csum[s * cs:(s + 1) * cs, :scols] + ar
            cim = csum[s * cs:(s + 1) * cs, scols:] + ai
            xs_r = cr * dpr - cim * dpi
            xs_i = cr * dpi + cim * dpr
            pr, pi_ = xs_r[cs - 1:cs, :], xs_i[cs - 1:cs, :]
            xr.append(xs_r)
            xi.append(xs_i)
        xr_ref[0:1, sc] = pr
        xi_ref[0:1, sc] = pi_
        xfull = jnp.concatenate([jnp.concatenate(xr, axis=0), jnp.concatenate(xi, axis=0)], axis=1)
        y_ref[:, t * ucols:(t + 1) * ucols] = _dot(xfull.astype(BF16), cblk_ref[t])

    y = y_ref[...] + d_ref[...] * u_bf.astype(F32)
    cdf = 0.5 * (1.0 + jnp.tanh(math.sqrt(2.0 / math.pi) * (y + 0.044715 * (y * y * y))))
    y = y * cdf
    glu = _dot(y.astype(BF16), gw_ref[...]) + gb_ref[...]
    o_ref[0] = (y * _sigmoid(glu) * _silu(gate_ref[0].astype(F32))).astype(o_ref.dtype)


def _s5_tables(lam_re, lam_im, log_dt, b_re, b_im, c_re, c_im, cs):
    g, n = lam_re.shape
    pch = b_re.shape[-1]
    nt = g // S5_GT
    dt = jnp.exp(log_dt)[:, None]
    mag = jnp.exp(lam_re * dt)
    e_re, e_im = mag * jnp.cos(lam_im * dt), mag * jnp.sin(lam_im * dt)
    den = lam_re * lam_re + lam_im * lam_im
    coef_re = ((e_re - 1.0) * lam_re + e_im * lam_im) / den
    coef_im = (e_im * lam_re - (e_re - 1.0) * lam_im) / den
    bb_re = coef_re[..., None] * b_re - coef_im[..., None] * b_im
    bb_im = coef_re[..., None] * b_im + coef_im[..., None] * b_re
    eye = jnp.eye(S5_GT, dtype=F32)

    def btile(bb):
        return jnp.einsum('tgnq,gh->tgqhn', bb.reshape(nt, S5_GT, n, pch), eye).reshape(
            nt, S5_GT * pch, S5_GT * n)

    def ctile(cc):
        return jnp.einsum('tgpn,gh->tgnhp', cc.reshape(nt, S5_GT, pch, n), eye).reshape(
            nt, S5_GT * n, S5_GT * pch)

    bblk = jnp.concatenate([btile(bb_re), btile(bb_im)], axis=-1).astype(BF16)
    cblk = jnp.concatenate([ctile(c_re), -ctile(c_im)], axis=1).astype(BF16)
    j = jnp.arange(cs, dtype=F32)[:, None, None]
    lr, li = (lam_re * dt)[None], (lam_im * dt)[None]
    flat = lambda t: t.reshape(t.shape[0], g * n)
    dpr, dpi = flat(jnp.exp(j * lr) * jnp.cos(j * li)), flat(jnp.exp(j * lr) * jnp.sin(j * li))
    dnr, dni = flat(jnp.exp(-j * lr) * jnp.cos(j * li)), flat(-jnp.exp(-j * lr) * jnp.sin(j * li))
    return bblk, cblk, dnr, dni, dpr, dpi, e_re.reshape(1, g * n), e_im.reshape(1, g * n)


def _s5_call(p, tables, d, glu_w, glu_b, u_blk, gate_blk, ts, cs):
    bsz, seq, _ = p.shape
    width = d.shape[-1]
    bblk, cblk, dnr, dni, dpr, dpi, lre, lim = tables
    nstate = lre.shape[-1]
    res = lambda a: _resident(a.shape)
    return pl.pallas_call(
        functools.partial(_s5_kernel, ts=ts, cs=cs),
        grid=(bsz, seq // ts),
        in_specs=[pl.BlockSpec((1, ts, width), lambda b, t: (b, t, u_blk)),
                  pl.BlockSpec((1, ts, width), lambda b, t: (b, t, gate_blk)),
                  res(bblk), res(cblk), res(dnr), res(dni), res(dpr), res(dpi), res(lre), res(lim),
                  _resident((1, width)), res(glu_w), _resident((1, width))],
        out_specs=pl.BlockSpec((1, ts, width), lambda b, t: (b, t, 0)),
        out_shape=jax.ShapeDtypeStruct((bsz, seq, width), BF16),
        scratch_shapes=[pltpu.VMEM((8, nstate), F32), pltpu.VMEM((8, nstate), F32),
                        pltpu.VMEM((ts, width), F32)],
        compiler_params=_params(("parallel", "arbitrary")),
        name="s5",
    )(p, p, bblk, cblk, dnr, dni, dpr, dpi, lre, lim, d.reshape(1, width), glu_w,
      glu_b.reshape(1, width))


def _rope128(x, cos, sin):
    lane = lax.broadcasted_iota(jnp.int32, x.shape, 1)
    half = QK_ROPE // 2
    partner = jnp.where(lane < half, pltpu.roll(x, LANES - half, axis=1), pltpu.roll(x, half, axis=1))
    return x * cos + partner * sin


def _qkv_kernel(cq_ref, ckv_ref, kpe_ref, cos_ref, sin_ref, qn_ref, kvn_ref, wq_ref, wkv_ref,
                q_ref, k_ref, v_ref, *, scale):
    def latent(ref, gain_ref):
        x = ref[0].astype(F32)
        ms = jnp.mean(x * x, axis=-1, keepdims=True)
        return (x * lax.rsqrt(ms + NORM_EPS) * gain_ref[...]).astype(BF16)

    cq = latent(cq_ref, qn_ref)
    ckv = latent(ckv_ref, kvn_ref)
    cos, sin = cos_ref[0], sin_ref[0]
    kpe = _rope128(kpe_ref[0].astype(F32), cos, sin).astype(BF16)
    hw = 2 * LANES
    for h in range(q_ref.shape[1]):
        qh = _dot(cq, wq_ref[:, h * hw:(h + 1) * hw]) * scale
        q_ref[0, h] = jnp.concatenate(
            [qh[:, :LANES], _rope128(qh[:, LANES:], cos, sin)], axis=1).astype(BF16)
        kvh = _dot(ckv, wkv_ref[:, h * hw:(h + 1) * hw])
        k_ref[0, h] = jnp.concatenate([kvh[:, :LANES].astype(BF16), kpe], axis=1)
        v_ref[0, h] = kvh[:, LANES:].astype(BF16)


def _qkv_call(p, cos, sin, q_norm, kv_norm, wq, wkv, cq_blk, ckv_blk, kpe_blk, tm):
    bsz, seq, _ = p.shape
    nh = MLA_HEADS
    hw = 2 * LANES
    scale = math.log2(math.e) / math.sqrt(QK_NOPE + QK_ROPE)
    head_out = lambda w: pl.BlockSpec((1, nh, tm, w), lambda b, t: (b, 0, t, 0))
    return pl.pallas_call(
        functools.partial(_qkv_kernel, scale=scale),
        grid=(bsz, seq // tm),
        in_specs=[pl.BlockSpec((1, tm, Q_LORA), lambda b, t: (b, t, cq_blk)),
                  pl.BlockSpec((1, tm, KV_LORA), lambda b, t: (b, t, ckv_blk)),
                  pl.BlockSpec((1, tm, LANES), lambda b, t: (b, t, kpe_blk)),
                  pl.BlockSpec((1, tm, LANES), lambda b, t: (b, t, 0)),
                  pl.BlockSpec((1, tm, LANES), lambda b, t: (b, t, 0)),
                  _resident((1, Q_LORA)), _resident((1, KV_LORA)),
                  _resident(wq.shape), _resident(wkv.shape)],
        out_specs=[head_out(hw), head_out(hw), head_out(V_HEAD)],
        out_shape=[jax.ShapeDtypeStruct((bsz, nh, seq, hw), BF16),
                   jax.ShapeDtypeStruct((bsz, nh, seq, hw), BF16),
                   jax.ShapeDtypeStruct((bsz, nh, seq, V_HEAD), BF16)],
        compiler_params=_params(("parallel", "parallel")),
        name="qkv",
    )(p, p, p, cos, sin, q_norm.reshape(1, Q_LORA), kv_norm.reshape(1, KV_LORA), wq, wkv)


def _attn_kernel(q_ref, k_ref, v_ref, o_ref, *, tq, tk):
    qi = pl.program_id(2)
    nsub = tq // tk
    qs = [q_ref[0, 0, i * tk:(i + 1) * tk, :] for i in range(nsub)]
    diag = (lax.broadcasted_iota(jnp.int32, (tk, tk), 0)
            >= lax.broadcasted_iota(jnp.int32, (tk, tk), 1))

    def block(j, carries, kinds):
        rows = pl.ds(pl.multiple_of(j * tk, tk), tk)
        kb, vb = k_ref[0, 0, rows, :], v_ref[0, 0, rows, :]
        ss = [None if kinds[i] is None else _dot_nt(qs[i], kb) for i in range(nsub)]
        out = []
        for i in range(nsub):
            if kinds[i] is None:
                out.append(carries[i])
                continue
            m, l, acc = carries[i]
            s = jnp.where(diag, ss[i], -1e30) if kinds[i] else ss[i]
            m_new = jnp.maximum(m, jnp.max(s, axis=-1, keepdims=True))
            pexp = jnp.exp2(s - m_new)
            alpha = jnp.exp2(m - m_new)
            l = alpha * l + jnp.sum(pexp, axis=-1, keepdims=True)
            acc = alpha * acc + _dot(pexp.astype(BF16), vb)
            out.append((m_new, l, acc))
        return tuple(out)

    init = tuple((jnp.full((tk, 1), -1e30, F32), jnp.zeros((tk, 1), F32),
                  jnp.zeros((tk, v_ref.shape[3]), F32)) for _ in range(nsub))
    nfull = qi * nsub
    carries = lax.fori_loop(0, nfull, lambda j, c: block(j, c, [False] * nsub), init)
    for d in range(nsub):
        kinds = [None if i < d else (i == d) for i in range(nsub)]
        carries = block(nfull + d, carries, kinds)
    for i in range(nsub):
        _, l, acc = carries[i]
        o_ref[0, i * tk:(i + 1) * tk, :] = (acc / l).astype(o_ref.dtype)


def _attn_call(q, k, v, tq, tk):
    bsz, nh, seq, hw = q.shape
    vd = v.shape[3]
    return pl.pallas_call(
        functools.partial(_attn_kernel, tq=tq, tk=tk),
        grid=(bsz, nh, seq // tq),
        in_specs=[pl.BlockSpec((1, 1, tq, hw), lambda b, h, i: (b, h, i, 0)),
                  pl.BlockSpec((1, 1, seq, hw), lambda b, h, i: (b, h, 0, 0)),
                  pl.BlockSpec((1, 1, seq, vd), lambda b, h, i: (b, h, 0, 0))],
        out_specs=pl.BlockSpec((1, tq, vd), lambda b, h, i: (b, i, h)),
        out_shape=jax.ShapeDtypeStruct((bsz, seq, nh * vd), BF16),
        compiler_params=_params(("parallel", "parallel", "arbitrary")),
        name="attn",
    )(q, k, v)


def _rope_tables(positions):
    inv_freq = 1.0 / (ROPE_BASE ** (jnp.arange(0, QK_ROPE, 2, dtype=F32) / QK_ROPE))
    ang = positions.astype(F32)[..., None] * inv_freq
    cos, sin = jnp.cos(ang), jnp.sin(ang)
    zero = jnp.zeros_like(cos)
    return (jnp.concatenate([cos, cos, zero, zero], axis=-1),
            jnp.concatenate([-sin, sin, zero, zero], axis=-1))


def _modulation(ada, j, bsz, d):
    m = ada[j, :bsz]
    part = lambda i: m[:, i * d:(i + 1) * d].reshape(bsz, 1, d)
    return part(0), part(1), part(2)


def kernel(x, c, positions, ev_ada_w, ev_ada_b, ev_norm_pre, ev_norm_post, ev_w_in, ev_mu, ev_w0, ev_w2, ev_a0, ev_a2, ev_k_k, ev_k_a, ev_r_k, ev_lnx_g, ev_lnx_b, ev_lam_re, ev_lam_im, ev_log_dt, ev_b_re, ev_b_im, ev_c_re, ev_c_im, ev_d, ev_glu_w, ev_glu_b, ev_w_out, od_ada_w, od_ada_b, od_norm_pre, od_norm_post, od_w_in, od_q_norm, od_w_q_up, od_kv_norm, od_w_kv_up, od_w_out):
    bsz, seq, d = x.shape
    n_even, n_odd = ev_w_in.shape[0], od_w_in.shape[0]
    depth = n_even + n_odd
    rw = ev_w0.shape[-1]
    sw = ev_d.shape[-1]
    a_in = 4 * rw + 2 * RWKV_LORA
    h = x.astype(F32)

    c8 = jnp.zeros((8, d), F32).at[:bsz].set(c.astype(F32))
    ev_ada = _ada_call(c8, ev_ada_w, ev_ada_b)
    od_ada = _ada_call(c8, od_ada_w, od_ada_b)
    cos, sin = _rope_tables(positions)
    eblk = jnp.kron(jnp.eye(rw // RWKV_HEAD, dtype=F32),
                    jnp.ones((RWKV_HEAD, RWKV_HEAD), F32)).astype(BF16)

    tm_out = min(512, seq)
    for i in range(depth):
        j = i // 2
        if i % 2 == 0:
            shift, scale, gate = _modulation(ev_ada, j, bsz, d)
            w = ev_w_in[j]
            w_in = jnp.concatenate([w[:, :4 * rw], w[:, a_in:], w[:, 4 * rw:a_in]], axis=1).astype(BF16)
            p = _norm_proj_call(h, ev_norm_pre[j], scale, shift, w_in, tm=min(256, seq), tn=896)
            mu = ev_mu[j]
            y_a = _rwkv_call(p, mu[:4 * rw].reshape(4, rw), mu[4 * rw:].reshape(1, 2 * RWKV_LORA),
                             ev_w0[j], ev_w2[j].astype(BF16), ev_a0[j], ev_a2[j].astype(BF16),
                             ev_k_k[j], ev_k_a[j],
                             ev_r_k[j], ev_lnx_g[j], ev_lnx_b[j], eblk,
                             tb=min(RWKV_TB, seq), ch=RWKV_CHUNK)
            tables = _s5_tables(ev_lam_re[j], ev_lam_im[j], ev_log_dt[j], ev_b_re[j], ev_b_im[j],
                                ev_c_re[j], ev_c_im[j], S5_CHUNK)
            y_b = _s5_call(p, tables, ev_d[j], ev_glu_w[j].astype(BF16), ev_glu_b[j],
                           u_blk=(4 * rw) // sw, gate_blk=(4 * rw) // sw + 1,
                           ts=min(S5_TS, seq), cs=S5_CHUNK)
            half = lambda t: pl.BlockSpec((1, tm_out, rw), lambda b, t_: (b, t_, 0))
            h = _out_call(_out_even_kernel, [y_a, y_b], [half(0), half(1)],
                          ev_w_out[j].astype(BF16), ev_norm_post[j], gate, h, tm_out)
        else:
            shift, scale, gate = _modulation(od_ada, j, bsz, d)
            w = od_w_in[j]
            o1, o2, o3 = Q_LORA, Q_LORA + KV_LORA, Q_LORA + KV_LORA + QK_ROPE
            mw = w.shape[1] - o3
            w_in = jnp.concatenate([w[:, o3:], w[:, :o3],
                                    jnp.zeros((d, LANES - QK_ROPE), w.dtype)], axis=1).astype(BF16)
            p = _norm_proj_call(h, od_norm_pre[j], scale, shift, w_in, tm=min(512, seq), tn=640)
            nh = MLA_HEADS
            wq = od_w_q_up[j].reshape(Q_LORA, nh, QK_NOPE + QK_ROPE)
            wq = jnp.pad(wq, ((0, 0), (0, 0), (0, 2 * LANES - QK_NOPE - QK_ROPE)))
            wq = wq.reshape(Q_LORA, nh * 2 * LANES).astype(BF16)
            wkv = od_w_kv_up[j].astype(BF16)
            q, k, v = _qkv_call(p, cos, sin, od_q_norm[j], od_kv_norm[j], wq, wkv,
                                cq_blk=mw // Q_LORA, ckv_blk=mw // KV_LORA + 1,
                                kpe_blk=(mw + o2) // LANES, tm=min(512, seq))
            o = _attn_call(q, k, v, tq=min(1024, seq), tk=min(512, seq))
            full = lambda jblk: pl.BlockSpec((1, tm_out, mw), lambda b, t_, jblk=jblk: (b, t_, jblk))
            h = _out_call(_out_odd_kernel, [o, p], [full(0), full(0)],
                          od_w_out[j].astype(BF16), od_norm_post[j], gate, h, tm_out)
    return h.astype(x.dtype)
```

```python
import functools
import math

import jax
import jax.numpy as jnp
from jax import lax
from jax.experimental import pallas as pl
from jax.experimental.pallas import tpu as pltpu

F32 = jnp.float32
BF16 = jnp.bfloat16
HIGHEST = lax.Precision.HIGHEST

NORM_EPS = 1e-6
LNX_EPS = 64e-5
ROPE_BASE = 10000.0

RWKV_HEAD = 64
RWKV_LORA = 64
S5_GROUP = 16
S5_STATE = 64
MLA_HEADS = 16
QK_NOPE = 128
QK_ROPE = 64
V_HEAD = 128
Q_LORA = 512
KV_LORA = 512

LANES = 128
V7X_VMEM_LIMIT = 56 * 1024 * 1024

RWKV_CHUNK = 64
RWKV_TB = 256
S5_CHUNK = 64
S5_TS = 256
S5_GT = 16


def _params(sem, vmem=V7X_VMEM_LIMIT):
    return pltpu.CompilerParams(dimension_semantics=sem, vmem_limit_bytes=vmem)


def _sigmoid(x):
    return 1.0 / (1.0 + jnp.exp(-x))


def _silu(x):
    return x * _sigmoid(x)


def _dot(a, b):
    return jnp.dot(a, b, preferred_element_type=F32)


def _dot_nt(a, b):
    return lax.dot_general(a, b, (((1,), (1,)), ((), ())), preferred_element_type=F32)


def _dot_tn(a, b):
    return lax.dot_general(a, b, (((0,), (0,)), ((), ())), preferred_element_type=F32)


def _resident(shape):
    nd = len(shape)
    return pl.BlockSpec(shape, lambda *_: (0,) * nd, pipeline_mode=pl.Buffered(1))


def _ada_kernel(c_ref, w_ref, b_ref, o_ref):
    s = _silu(c_ref[...]).astype(BF16)
    o_ref[0] = _dot(s, w_ref[0].astype(BF16)) + b_ref[0]


def _ada_call(c8, w, b):
    nl, d, n3 = w.shape
    tn = 768
    return pl.pallas_call(
        _ada_kernel,
        grid=(nl, n3 // tn),
        in_specs=[pl.BlockSpec((8, d), lambda l, n: (0, 0)),
                  pl.BlockSpec((1, d, tn), lambda l, n: (l, 0, n)),
                  pl.BlockSpec((1, 1, tn), lambda l, n: (l, 0, n))],
        out_specs=pl.BlockSpec((1, 8, tn), lambda l, n: (l, 0, n)),
        out_shape=jax.ShapeDtypeStruct((nl, 8, n3), F32),
        compiler_params=_params(("parallel", "parallel")),
        name="ada",
    )(c8, w, b.reshape(nl, 1, n3))


def _norm_proj_kernel(h_ref, g_ref, sc_ref, sh_ref, w_ref, o_ref, *, tn):
    x = h_ref[0]
    ms = jnp.mean(x * x, axis=-1, keepdims=True)
    z = (x * lax.rsqrt(ms + NORM_EPS) * g_ref[...]) * (1.0 + sc_ref[0]) + sh_ref[0]
    z = z.astype(BF16)
    n = w_ref.shape[1]
    for n0 in range(0, n, tn):
        o_ref[0, :, n0:n0 + tn] = _dot(z, w_ref[:, n0:n0 + tn]).astype(o_ref.dtype)


def _norm_proj_call(h, g, scale, shift, w, tm, tn):
    bsz, seq, d = h.shape
    n = w.shape[1]
    return pl.pallas_call(
        functools.partial(_norm_proj_kernel, tn=tn),
        grid=(bsz, seq // tm),
        in_specs=[pl.BlockSpec((1, tm, d), lambda b, t: (b, t, 0)),
                  _resident((1, d)),
                  pl.BlockSpec((1, 1, d), lambda b, t: (b, 0, 0)),
                  pl.BlockSpec((1, 1, d), lambda b, t: (b, 0, 0)),
                  _resident((d, n))],
        out_specs=pl.BlockSpec((1, tm, n), lambda b, t: (b, t, 0)),
        out_shape=jax.ShapeDtypeStruct((bsz, seq, n), BF16),
        compiler_params=_params(("parallel", "parallel")),
        name="norm_proj",
    )(h, g.reshape(1, d), scale, shift, w)


def _finish(y, post_ref, gate_ref, h_ref, o_ref):
    ms = jnp.mean(y * y, axis=-1, keepdims=True)
    yn = y * lax.rsqrt(ms + NORM_EPS) * post_ref[...]
    o_ref[0] = h_ref[0] + gate_ref[0] * yn


def _out_even_kernel(ya_ref, yb_ref, w_ref, post_ref, gate_ref, h_ref, o_ref):
    half = ya_ref.shape[2]
    y = _dot(ya_ref[0], w_ref[:half, :]) + _dot(yb_ref[0], w_ref[half:, :])
    _finish(y, post_ref, gate_ref, h_ref, o_ref)


def _out_odd_kernel(o_in_ref, g_ref, w_ref, post_ref, gate_ref, h_ref, o_ref):
    yin = o_in_ref[0].astype(F32) * _silu(g_ref[0].astype(F32))
    y = _dot(yin.astype(BF16), w_ref[...])
    _finish(y, post_ref, gate_ref, h_ref, o_ref)


def _out_call(kernel, acts, act_specs, w, post, gate, h, tm):
    bsz, seq, d = h.shape
    return pl.pallas_call(
        kernel,
        grid=(bsz, seq // tm),
        in_specs=act_specs + [
            _resident(w.shape),
            _resident((1, d)),
            pl.BlockSpec((1, 1, d), lambda b, t: (b, 0, 0)),
            pl.BlockSpec((1, tm, d), lambda b, t: (b, t, 0))],
        out_specs=pl.BlockSpec((1, tm, d), lambda b, t: (b, t, 0)),
        out_shape=jax.ShapeDtypeStruct((bsz, seq, d), F32),
        compiler_params=_params(("parallel", "parallel")),
        name=kernel.__name__.strip("_"),
    )(*acts, w, post.reshape(1, d), gate, h)


def _rwkv_kernel(r_ref, k_ref, v_ref, g_ref, lo_ref,
                 mu_ref, mulo_ref, w0_ref, w2_ref, a0_ref, a2_ref, kk_ref, ka_ref, rk_ref,
                 lng_ref, lnb_ref, eblk_ref,
                 o_ref,
                 carry_ref, carrylo_ref, state_ref,
                 rx_ref, remx_ref, repd_ref, kd_ref, repe_ref, ke_ref, vv_ref,
                 pc_ref, y_ref, bonus_ref, sg_ref, *, tb, ch):
    width = r_ref.shape[2]
    npair = width // LANES
    nch = tb // ch

    @pl.when(pl.program_id(1) == 0)
    def _():
        carry_ref[...] = jnp.zeros_like(carry_ref)
        carrylo_ref[...] = jnp.zeros_like(carrylo_ref)
        state_ref[...] = jnp.zeros_like(state_ref)

    row0 = lax.broadcasted_iota(jnp.int32, (tb, 1), 0) == 0

    def shift(x, cref, slot, mu):
        prev = jnp.where(row0, cref[slot:slot + 1, :], pltpu.roll(x, 1, axis=0))
        cref[slot:slot + 1, :] = x[tb - 1:tb, :]
        return x + (prev - x) * mu

    r = shift(r_ref[0].astype(F32), carry_ref, 0, mu_ref[0:1, :])
    k = shift(k_ref[0].astype(F32), carry_ref, 1, mu_ref[1:2, :])
    v = shift(v_ref[0].astype(F32), carry_ref, 2, mu_ref[2:3, :])
    g = shift(g_ref[0].astype(F32), carry_ref, 3, mu_ref[3:4, :])
    lo = shift(lo_ref[0].astype(F32), carrylo_ref, 0, mulo_ref[...])
    w_lo, a_lo = lo[:, :RWKV_LORA], lo[:, RWKV_LORA:]

    x = w0_ref[...] + _dot(jnp.tanh(w_lo).astype(BF16), w2_ref[...])
    y = -x
    softplus = jnp.maximum(y, 0.0) + jnp.log(1.0 + jnp.exp(-jnp.abs(y)))
    ld = -jnp.exp(-softplus - 0.5)
    a = _sigmoid(a0_ref[...] + _dot(a_lo.astype(BF16), a2_ref[...]))

    eblk = eblk_ref[...]

    def headsum(t):
        return _dot(t.astype(BF16), eblk)

    kk = k * kk_ref[...]
    kk = kk * lax.rsqrt(jnp.maximum(headsum(kk * kk), 1e-24))
    k2 = k * (1.0 + (a - 1.0) * ka_ref[...])
    rep = kk * a
    bonus_ref[...] = headsum(r * k2 * rk_ref[...]) * v
    sg_ref[...] = _silu(g)
    vv_ref[...] = v.astype(BF16)

    ri = lax.broadcasted_iota(jnp.int32, (tb, tb), 0)
    ci = lax.broadcasted_iota(jnp.int32, (tb, tb), 1)
    lincl = jnp.where(((ri // ch) == (ci // ch)) & (ci <= ri), 1.0, 0.0).astype(BF16)
    hi = ld.astype(BF16)
    r1 = ld - hi.astype(F32)
    mid = r1.astype(BF16)
    low = (r1 - mid.astype(F32)).astype(BF16)
    parts = _dot(lincl, jnp.concatenate([hi, mid, low], axis=1))
    cum = parts[:, :width] + parts[:, width:2 * width] + parts[:, 2 * width:]
    pinv = jnp.exp(-cum)
    pend = []
    for s in range(nch):
        pc = jnp.exp(cum[(s + 1) * ch - 1:(s + 1) * ch, :])
        pc_ref[8 * s:8 * s + 8, :] = jnp.broadcast_to(pc, (8, width))
        pend.append(pc * pinv[s * ch:(s + 1) * ch, :])
    pend = jnp.concatenate(pend, axis=0)
    rx_ref[...] = (r * jnp.exp(cum)).astype(BF16)
    remx_ref[...] = (-kk * jnp.exp(cum - ld)).astype(BF16)
    repd_ref[...] = (rep * pinv).astype(BF16)
    kd_ref[...] = (k2 * pinv).astype(BF16)
    repe_ref[...] = (rep * pend).astype(BF16)
    ke_ref[...] = (k2 * pend).astype(BF16)

    c2 = 2 * ch
    pi = lax.broadcasted_iota(jnp.int32, (c2, c2), 0)
    pj = lax.broadcasted_iota(jnp.int32, (c2, c2), 1)
    blk = (pi // ch) == (pj // ch)
    strict = blk & (pj < pi)
    incl = blk & (pj <= pi)
    lane = lax.broadcasted_iota(jnp.int32, (ch, LANES), 1)
    m0 = lane < RWKV_HEAD
    nsq = int(math.log2(ch)) - 1

    def by_head(t):
        z = jnp.zeros_like(t)
        return jnp.concatenate([jnp.where(m0, t, z), jnp.where(m0, z, t)], axis=0)

    def chunk(c, _):
        rows = pl.ds(pl.multiple_of(c * ch, ch), ch)
        pairs = range(npair)
        cols = [slice(p * LANES, (p + 1) * LANES) for p in pairs]
        cat = jnp.concatenate
        lhs = [cat([by_head(remx_ref[rows, cols[p]]), by_head(rx_ref[rows, cols[p]])], axis=0)
               for p in pairs]
        m1 = [_dot_nt(lhs[p], cat([repd_ref[rows, cols[p]]] * 2 + [kd_ref[rows, cols[p]]] * 2, axis=0))
              for p in pairs]
        s_old = [state_ref[p] for p in pairs]
        gs = [_dot_nt(lhs[p], s_old[p].astype(BF16)) for p in pairs]
        v_bd = [by_head(vv_ref[rows, cols[p]]) for p in pairs]
        sa = [gs[p][:c2] + _dot(jnp.where(strict, m1[p][:c2, c2:], 0.0).astype(BF16), v_bd[p])
              for p in pairs]
        pw = [jnp.where(strict, m1[p][:c2, :c2], 0.0) for p in pairs]
        for _i in range(nsq):
            both = [_dot(pw[p].astype(BF16), cat([pw[p], sa[p]], axis=1).astype(BF16)) for p in pairs]
            pw = [both[p][:, :c2] for p in pairs]
            sa = [sa[p] + both[p][:, c2:] for p in pairs]
        sa = [(sa[p] + _dot(pw[p].astype(BF16), sa[p].astype(BF16))).astype(BF16) for p in pairs]
        vals = [cat([sa[p], v_bd[p]], axis=0) for p in pairs]
        for p in pairs:
            arbr = cat([jnp.where(incl, m1[p][c2:, :c2], 0.0), jnp.where(incl, m1[p][c2:, c2:], 0.0)],
                       axis=1).astype(BF16)
            y_bd = gs[p][c2:] + _dot(arbr, vals[p])
            y_ref[rows, cols[p]] = y_bd[:ch] + y_bd[ch:]
        for p in pairs:
            upd = _dot_tn(vals[p], cat([by_head(repe_ref[rows, cols[p]]), by_head(ke_ref[rows, cols[p]])],
                                       axis=0))
            pc = pc_ref[pl.ds(pl.multiple_of(c * 8, 8), 8), cols[p]][0:1, :]
            state_ref[p] = s_old[p] * pc + upd
        return 0

    lax.fori_loop(0, nch, chunk, 0)

    yv = y_ref[...]
    inv_n = 1.0 / RWKV_HEAD
    mean = headsum(yv) * inv_n
    dlt = yv - mean
    var = headsum(dlt * dlt) * inv_n
    yn = dlt * lax.rsqrt(var + LNX_EPS) * lng_ref[...] + lnb_ref[...]
    o_ref[0] = ((yn + bonus_ref[...]) * sg_ref[...]).astype(o_ref.dtype)


def _rwkv_call(p, mu, mulo, w0, w2, a0, a2, k_k, k_a, r_k, lnx_g, lnx_b, eblk, tb, ch):
    bsz, seq, _ = p.shape
    width = w0.shape[-1]
    npair = width // LANES
    lo_blk = p.shape[-1] // LANES - 1
    act = lambda j: pl.BlockSpec((1, tb, width), lambda b, t, j=j: (b, t, j))
    row = lambda a: a.reshape(1, width)
    bf = lambda: pltpu.VMEM((tb, width), BF16)
    f32 = lambda: pltpu.VMEM((tb, width), F32)
    return pl.pallas_call(
        functools.partial(_rwkv_kernel, tb=tb, ch=ch),
        grid=(bsz, seq // tb),
        in_specs=[act(0), act(1), act(2), act(3),
                  pl.BlockSpec((1, tb, LANES), lambda b, t: (b, t, lo_blk)),
                  _resident((4, width)), _resident((1, LANES)),
                  _resident((1, width)), _resident((RWKV_LORA, width)),
                  _resident((1, width)), _resident((RWKV_LORA, width)),
                  _resident((1, width)), _resident((1, width)), _resident((1, width)),
                  _resident((1, width)), _resident((1, width)),
                  _resident((width, width))],
        out_specs=pl.BlockSpec((1, tb, width), lambda b, t: (b, t, 0)),
        out_shape=jax.ShapeDtypeStruct((bsz, seq, width), BF16),
        scratch_shapes=[pltpu.VMEM((8, width), F32), pltpu.VMEM((8, LANES), F32),
                        pltpu.VMEM((npair, LANES, LANES), F32),
                        bf(), bf(), bf(), bf(), bf(), bf(), bf(),
                        pltpu.VMEM((8 * (tb // ch), width), F32), f32(), f32(), f32()],
        compiler_params=_params(("parallel", "arbitrary")),
        name="rwkv",
    )(p, p, p, p, p, mu, mulo, row(w0), w2, row(a0), a2, row(k_k), row(k_a), row(r_k),
      row(lnx_g), row(lnx_b), eblk)


def _s5_kernel(u_ref, gate_ref, bblk_ref, cblk_ref, dnr_ref, dni_ref, dpr_ref, dpi_ref,
               lre_ref, lim_ref, d_ref, gw_ref, gb_ref, o_ref, xr_ref, xi_ref, y_ref, *, ts, cs):
    @pl.when(pl.program_id(1) == 0)
    def _():
        xr_ref[...] = jnp.zeros_like(xr_ref)
        xi_ref[...] = jnp.zeros_like(xi_ref)

    ntile, ucols, scols2 = bblk_ref.shape
    scols = scols2 // 2
    nsub = ts // cs
    u_bf = u_ref[0]
    ri = lax.broadcasted_iota(jnp.int32, (ts, ts), 0)
    ci = lax.broadcasted_iota(jnp.int32, (ts, ts), 1)
    ltri = jnp.where(((ri // cs) == (ci // cs)) & (ci <= ri), 1.0, 0.0).astype(BF16)

    for t in range(ntile):
        sc = slice(t * scols, (t + 1) * scols)
        bu = _dot(u_bf[:, t * ucols:(t + 1) * ucols], bblk_ref[t])
        dnr, dni = dnr_ref[:, sc], dni_ref[:, sc]
        dpr, dpi = dpr_ref[:, sc], dpi_ref[:, sc]
        zr, zi = [], []
        for s in range(nsub):
            br = bu[s * cs:(s + 1) * cs, :scols]
            bi = bu[s * cs:(s + 1) * cs, scols:]
            zr.append(br * dnr - bi * dni)
            zi.append(br * dni + bi * dnr)
        z = jnp.concatenate([jnp.concatenate(zr, axis=0), jnp.concatenate(zi, axis=0)], axis=1)
        csum = _dot(ltri, z.astype(BF16))
        lre, lim = lre_ref[:, sc], lim_ref[:, sc]
        pr, pi_ = xr_ref[0:1, sc], xi_ref[0:1, sc]
        xr, xi = [], []
        for s in range(nsub):
            ar = lre * pr - lim * pi_
            ai = lre * pi_ + lim * pr
            cr = csum[s * cs:(s + 1) * cs, :scols] + ar
            cim = csum[s * cs:(s + 1) * cs, scols:] + ai
            xs_r = cr * dpr - cim * dpi
            xs_i = cr * dpi + cim * dpr
            pr, pi_ = xs_r[cs - 1:cs, :], xs_i[cs - 1:cs, :]
            xr.append(xs_r)
            xi.append(xs_i)
        xr_ref[0:1, sc] = pr
        xi_ref[0:1, sc] = pi_
        xfull = jnp.concatenate([jnp.concatenate(xr, axis=0), jnp.concatenate(xi, axis=0)], axis=1)
        y_ref[:, t * ucols:(t + 1) * ucols] = _dot(xfull.astype(BF16), cblk_ref[t])

    y = y_ref[...] + d_ref[...] * u_bf.astype(F32)
    cdf = 0.5 * (1.0 + jnp.tanh(math.sqrt(2.0 / math.pi) * (y + 0.044715 * (y * y * y))))
    y = y * cdf
    glu = _dot(y.astype(BF16), gw_ref[...]) + gb_ref[...]
    o_ref[0] = (y * _sigmoid(glu) * _silu(gate_ref[0].astype(F32))).astype(o_ref.dtype)


def _s5_tables(lam_re, lam_im, log_dt, b_re, b_im, c_re, c_im, cs):
    g, n = lam_re.shape
    pch = b_re.shape[-1]
    nt = g // S5_GT
    dt = jnp.exp(log_dt)[:, None]
    mag = jnp.exp(lam_re * dt)
    e_re, e_im = mag * jnp.cos(lam_im * dt), mag * jnp.sin(lam_im * dt)
    den = lam_re * lam_re + lam_im * lam_im
    coef_re = ((e_re - 1.0) * lam_re + e_im * lam_im) / den
    coef_im = (e_im * lam_re - (e_re - 1.0) * lam_im) / den
    bb_re = coef_re[..., None] * b_re - coef_im[..., None] * b_im
    bb_im = coef_re[..., None] * b_im + coef_im[..., None] * b_re
    eye = jnp.eye(S5_GT, dtype=F32)

    def btile(bb):
        return jnp.einsum('tgnq,gh->tgqhn', bb.reshape(nt, S5_GT, n, pch), eye).reshape(
            nt, S5_GT * pch, S5_GT * n)

    def ctile(cc):
        return jnp.einsum('tgpn,gh->tgnhp', cc.reshape(nt, S5_GT, pch, n), eye).reshape(
            nt, S5_GT * n, S5_GT * pch)

    bblk = jnp.concatenate([btile(bb_re), btile(bb_im)], axis=-1).astype(BF16)
    cblk = jnp.concatenate([ctile(c_re), -ctile(c_im)], axis=1).astype(BF16)
    j = jnp.arange(cs, dtype=F32)[:, None, None]
    lr, li = (lam_re * dt)[None], (lam_im * dt)[None]
    flat = lambda t: t.reshape(t.shape[0], g * n)
    dpr, dpi = flat(jnp.exp(j * lr) * jnp.cos(j * li)), flat(jnp.exp(j * lr) * jnp.sin(j * li))
    dnr, dni = flat(jnp.exp(-j * lr) * jnp.cos(j * li)), flat(-jnp.exp(-j * lr) * jnp.sin(j * li))
    return bblk, cblk, dnr, dni, dpr, dpi, e_re.reshape(1, g * n), e_im.reshape(1, g * n)


def _s5_call(p, tables, d, glu_w, glu_b, u_blk, gate_blk, ts, cs):
    bsz, seq, _ = p.shape
    width = d.shape[-1]
    bblk, cblk, dnr, dni, dpr, dpi, lre, lim = tables
    nstate = lre.shape[-1]
    res = lambda a: _resident(a.shape)
    return pl.pallas_call(
        functools.partial(_s5_kernel, ts=ts, cs=cs),
        grid=(bsz, seq // ts),
        in_specs=[pl.BlockSpec((1, ts, width), lambda b, t: (b, t, u_blk)),
                  pl.BlockSpec((1, ts, width), lambda b, t: (b, t, gate_blk)),
                  res(bblk), res(cblk), res(dnr), res(dni), res(dpr), res(dpi), res(lre), res(lim),
                  _resident((1, width)), res(glu_w), _resident((1, width))],
        out_specs=pl.BlockSpec((1, ts, width), lambda b, t: (b, t, 0)),
        out_shape=jax.ShapeDtypeStruct((bsz, seq, width), BF16),
        scratch_shapes=[pltpu.VMEM((8, nstate), F32), pltpu.VMEM((8, nstate), F32),
                        pltpu.VMEM((ts, width), F32)],
        compiler_params=_params(("parallel", "arbitrary")),
        name="s5",
    )(p, p, bblk, cblk, dnr, dni, dpr, dpi, lre, lim, d.reshape(1, width), glu_w,
      glu_b.reshape(1, width))


def _rope128(x, cos, sin):
    lane = lax.broadcasted_iota(jnp.int32, x.shape, 1)
    half = QK_ROPE // 2
    partner = jnp.where(lane < half, pltpu.roll(x, LANES - half, axis=1), pltpu.roll(x, half, axis=1))
    return x * cos + partner * sin


def _qkv_kernel(cq_ref, ckv_ref, kpe_ref, cos_ref, sin_ref, qn_ref, kvn_ref, wq_ref, wkv_ref,
                q_ref, k_ref, v_ref, *, scale):
    def latent(ref, gain_ref):
        x = ref[0].astype(F32)
        ms = jnp.mean(x * x, axis=-1, keepdims=True)
        return (x * lax.rsqrt(ms + NORM_EPS) * gain_ref[...]).astype(BF16)

    cq = latent(cq_ref, qn_ref)
    ckv = latent(ckv_ref, kvn_ref)
    cos, sin = cos_ref[0], sin_ref[0]
    kpe = _rope128(kpe_ref[0].astype(F32), cos, sin).astype(BF16)
    hw = 2 * LANES
    for h in range(q_ref.shape[1]):
        qh = _dot(cq, wq_ref[:, h * hw:(h + 1) * hw]) * scale
        q_ref[0, h] = jnp.concatenate(
            [qh[:, :LANES], _rope128(qh[:, LANES:], cos, sin)], axis=1).astype(BF16)
        kvh = _dot(ckv, wkv_ref[:, h * hw:(h + 1) * hw])
        k_ref[0, h] = jnp.concatenate([kvh[:, :LANES].astype(BF16), kpe], axis=1)
        v_ref[0, h] = kvh[:, LANES:].astype(BF16)


def _qkv_call(p, cos, sin, q_norm, kv_norm, wq, wkv, cq_blk, ckv_blk, kpe_blk, tm):
    bsz, seq, _ = p.shape
    nh = MLA_HEADS
    hw = 2 * LANES
    scale = math.log2(math.e) / math.sqrt(QK_NOPE + QK_ROPE)
    head_out = lambda w: pl.BlockSpec((1, nh, tm, w), lambda b, t: (b, 0, t, 0))
    return pl.pallas_call(
        functools.partial(_qkv_kernel, scale=scale),
        grid=(bsz, seq // tm),
        in_specs=[pl.BlockSpec((1, tm, Q_LORA), lambda b, t: (b, t, cq_blk)),
                  pl.BlockSpec((1, tm, KV_LORA), lambda b, t: (b, t, ckv_blk)),
                  pl.BlockSpec((1, tm, LANES), lambda b, t: (b, t, kpe_blk)),
                  pl.BlockSpec((1, tm, LANES), lambda b, t: (b, t, 0)),
                  pl.BlockSpec((1, tm, LANES), lambda b, t: (b, t, 0)),
                  _resident((1, Q_LORA)), _resident((1, KV_LORA)),
                  _resident(wq.shape), _resident(wkv.shape)],
        out_specs=[head_out(hw), head_out(hw), head_out(V_HEAD)],
        out_shape=[jax.ShapeDtypeStruct((bsz, nh, seq, hw), BF16),
                   jax.ShapeDtypeStruct((bsz, nh, seq, hw), BF16),
                   jax.ShapeDtypeStruct((bsz, nh, seq, V_HEAD), BF16)],
        compiler_params=_params(("parallel", "parallel")),
        name="qkv",
    )(p, p, p, cos, sin, q_norm.reshape(1, Q_LORA), kv_norm.reshape(1, KV_LORA), wq, wkv)


def _attn_kernel(q_ref, k_ref, v_ref, o_ref, *, tq, tk):
    qi = pl.program_id(2)
    nsub = tq // tk
    vd = v_ref.shape[3]
    qs =[q_ref[0, 0, i * tk:(i + 1) * tk, :] for i in range(nsub)]
    diag = (lax.broadcasted_iota(jnp.int32, (tk, tk), 0)
            >= lax.broadcasted_iota(jnp.int32, (tk, tk), 1))

    def scores(j, kinds):
        rows = pl.ds(pl.multiple_of(j * tk, tk), tk)
        kb = k_ref[0, 0, rows, :]
        return [None if kinds[i] is None else _dot_nt(qs[i], kb) for i in range(nsub)]

    def update(j, carries, kinds, ss):
        rows = pl.ds(pl.multiple_of(j * tk, tk), tk)
        vb = jnp.concatenate([v_ref[0, 0, rows, :], jnp.ones((tk, vd), BF16)], axis=1)
        out = []
        for i in range(nsub):
            if kinds[i] is None:
                out.append(carries[i])
                continue
            m, acc = carries[i]
            s = jnp.where(diag, ss[i], -1e30) if kinds[i] else ss[i]
            m_new = jnp.maximum(m, jnp.max(s, axis=-1, keepdims=True))
            pexp = jnp.exp2(s - m_new).astype(BF16)
            acc = jnp.exp2(m - m_new) * acc + _dot(pexp, vb)
            out.append((m_new, acc))
        return tuple(out)

    def blocks(js, carries, kinds_list):
        ss = [scores(j, kinds) for j, kinds in zip(js, kinds_list)]
        for j, kinds, s in zip(js, kinds_list, ss):
            carries = update(j, carries, kinds, s)
        return carries

    init = tuple((jnp.full((tk, 1), -1e30, F32), jnp.zeros((tk, 2 * vd), F32)) for _ in range(nsub))
    nfull = qi * nsub
    visible = [[False] * nsub] * nsub
    carries = lax.fori_loop(
        0, qi, lambda jq, c: blocks([jq * nsub + d for d in range(nsub)], c, visible), init)
    tail = [[None if i < d else (i == d) for i in range(nsub)] for d in range(nsub)]
    carries = blocks([nfull + d for d in range(nsub)], carries, tail)
    for i in range(nsub):
        _, acc = carries[i]
        o_ref[0, i * tk:(i + 1) * tk, :] = (acc[:, :vd] / acc[:, vd:]).astype(o_ref.dtype)


def _attn_call(q, k, v, tq, tk):
    bsz, nh, seq, hw = q.shape
    vd = v.shape[3]
    return pl.pallas_call(
        functools.partial(_attn_kernel, tq=tq, tk=tk),
        grid=(bsz, nh, seq // tq),
        in_specs=[pl.BlockSpec((1, 1, tq, hw), lambda b, h, i: (b, h, i, 0)),
                  pl.BlockSpec((1, 1, seq, hw), lambda b, h, i: (b, h, 0, 0)),
                  pl.BlockSpec((1, 1, seq, vd), lambda b, h, i: (b, h, 0, 0))],
        out_specs=pl.BlockSpec((1, tq, vd), lambda b, h, i: (b, i, h)),
        out_shape=jax.ShapeDtypeStruct((bsz, seq, nh * vd), BF16),
        compiler_params=_params(("parallel", "parallel", "arbitrary")),
        name="attn",
    )(q, k, v)


def _rope_tables(positions):
    inv_freq = 1.0 / (ROPE_BASE ** (jnp.arange(0, QK_ROPE, 2, dtype=F32) / QK_ROPE))
    ang = positions.astype(F32)[..., None] * inv_freq
    cos, sin = jnp.cos(ang), jnp.sin(ang)
    zero = jnp.zeros_like(cos)
    return (jnp.concatenate([cos, cos, zero, zero], axis=-1),
            jnp.concatenate([-sin, sin, zero, zero], axis=-1))


def _modulation(ada, j, bsz, d):
    m = ada[j, :bsz]
    part = lambda i: m[:, i * d:(i + 1) * d].reshape(bsz, 1, d)
    return part(0), part(1), part(2)


def kernel(x, c, positions, ev_ada_w, ev_ada_b, ev_norm_pre, ev_norm_post, ev_w_in, ev_mu, ev_w0, ev_w2, ev_a0, ev_a2, ev_k_k, ev_k_a, ev_r_k, ev_lnx_g, ev_lnx_b, ev_lam_re, ev_lam_im, ev_log_dt, ev_b_re, ev_b_im, ev_c_re, ev_c_im, ev_d, ev_glu_w, ev_glu_b, ev_w_out, od_ada_w, od_ada_b, od_norm_pre, od_norm_post, od_w_in, od_q_norm, od_w_q_up, od_kv_norm, od_w_kv_up, od_w_out):
    bsz, seq, d = x.shape
    n_even, n_odd = ev_w_in.shape[0], od_w_in.shape[0]
    depth = n_even + n_odd
    rw = ev_w0.shape[-1]
    sw = ev_d.shape[-1]
    a_in = 4 * rw + 2 * RWKV_LORA
    h = x.astype(F32)

    c8 = jnp.zeros((8, d), F32).at[:bsz].set(c.astype(F32))
    ev_ada = _ada_call(c8, ev_ada_w, ev_ada_b)
    od_ada = _ada_call(c8, od_ada_w, od_ada_b)
    cos, sin = _rope_tables(positions)
    eblk = jnp.kron(jnp.eye(rw // RWKV_HEAD, dtype=F32),
                    jnp.ones((RWKV_HEAD, RWKV_HEAD), F32)).astype(BF16)

    tm_out = min(512, seq)
    for i in range(depth):
        j = i // 2
        if i % 2 == 0:
            shift, scale, gate = _modulation(ev_ada, j, bsz, d)
            w = ev_w_in[j]
            w_in = jnp.concatenate([w[:, :4 * rw], w[:, a_in:], w[:, 4 * rw:a_in]], axis=1).astype(BF16)
            p = _norm_proj_call(h, ev_norm_pre[j], scale, shift, w_in, tm=min(256, seq), tn=896)
            mu = ev_mu[j]
            y_a = _rwkv_call(p, mu[:4 * rw].reshape(4, rw), mu[4 * rw:].reshape(1, 2 * RWKV_LORA),
                             ev_w0[j], ev_w2[j].astype(BF16), ev_a0[j], ev_a2[j].astype(BF16),
                             ev_k_k[j], ev_k_a[j],
                             ev_r_k[j], ev_lnx_g[j], ev_lnx_b[j], eblk,
                             tb=min(RWKV_TB, seq), ch=RWKV_CHUNK)
            tables = _s5_tables(ev_lam_re[j], ev_lam_im[j], ev_log_dt[j], ev_b_re[j], ev_b_im[j],
                                ev_c_re[j], ev_c_im[j], S5_CHUNK)
            y_b = _s5_call(p, tables, ev_d[j], ev_glu_w[j].astype(BF16), ev_glu_b[j],
                           u_blk=(4 * rw) // sw, gate_blk=(4 * rw) // sw + 1,
                           ts=min(S5_TS, seq), cs=S5_CHUNK)
            half = lambda t: pl.BlockSpec((1, tm_out, rw), lambda b, t_: (b, t_, 0))
            h = _out_call(_out_even_kernel, [y_a, y_b], [half(0), half(1)],
                          ev_w_out[j].astype(BF16), ev_norm_post[j], gate, h, tm_out)
        else:
            shift, scale, gate = _modulation(od_ada, j, bsz, d)
            w = od_w_in[j]
            o1, o2, o3 = Q_LORA, Q_LORA + KV_LORA, Q_LORA + KV_LORA + QK_ROPE
            mw = w.shape[1] - o3
            w_in = jnp.concatenate([w[:, o3:], w[:, :o3],
                                    jnp.zeros((d, LANES - QK_ROPE), w.dtype)], axis=1).astype(BF16)
            p = _norm_proj_call(h, od_norm_pre[j], scale, shift, w_in, tm=min(512, seq), tn=640)
            nh = MLA_HEADS
            wq = od_w_q_up[j].reshape(Q_LORA, nh, QK_NOPE + QK_ROPE)
            wq = jnp.pad(wq, ((0, 0), (0, 0), (0, 2 * LANES - QK_NOPE - QK_ROPE)))
            wq = wq.reshape(Q_LORA, nh * 2 * LANES).astype(BF16)
            wkv = od_w_kv_up[j].astype(BF16)
            q, k, v = _qkv_call(p, cos, sin, od_q_norm[j], od_kv_norm[j], wq, wkv,
                                cq_blk=mw // Q_LORA, ckv_blk=mw // KV_LORA + 1,
                                kpe_blk=(mw + o2) // LANES, tm=min(512, seq))
            o = _attn_call(q, k, v, tq=min(2048, seq), tk=min(512, seq))
            full = lambda jblk: pl.BlockSpec((1, tm_out, mw), lambda b, t_, jblk=jblk: (b, t_, jblk))
            h = _out_call(_out_odd_kernel, [o, p], [full(0), full(0)],
                          od_w_out[j].astype(BF16), od_norm_post[j], gate, h, tm_out)
    return h.astype(x.dtype)
```

```python
import functools
import math

import jax
import jax.numpy as jnp
from jax import lax
from jax.experimental import pallas as pl
from jax.experimental.pallas import tpu as pltpu

F32 = jnp.float32
BF16 = jnp.bfloat16

NORM_EPS = 1e-6
LNX_EPS = 64e-5
ROPE_BASE = 10000.0

RWKV_HEAD = 64
RWKV_LORA = 64
S5_GROUP = 16
S5_STATE = 64
MLA_HEADS = 16
QK_NOPE = 128
QK_ROPE = 64
V_HEAD = 128
Q_LORA = 512
KV_LORA = 512

LANES = 128
V7X_VMEM_LIMIT = 56 * 1024 * 1024

RWKV_CHUNK = 64
RWKV_TB = 256
S5_CHUNK = 64
S5_TS = 256
S5_GT = 16
PROJ_TN = 1024


def _params(sem, vmem=V7X_VMEM_LIMIT):
    return pltpu.CompilerParams(dimension_semantics=sem, vmem_limit_bytes=vmem)


def _sigmoid(x):
    return 1.0 / (1.0 + jnp.exp(-x))


def _silu(x):
    return x * _sigmoid(x)


def _dot(a, b):
    return jnp.dot(a, b, preferred_element_type=F32)


def _dot_nt(a, b):
    return lax.dot_general(a, b, (((1,), (1,)), ((), ())), preferred_element_type=F32)


def _dot_tn(a, b):
    return lax.dot_general(a, b, (((0,), (0,)), ((), ())), preferred_element_type=F32)


def _resident(shape):
    nd = len(shape)
    return pl.BlockSpec(shape, lambda *_: (0,) * nd, pipeline_mode=pl.Buffered(1))


def _ada_kernel(c_ref, w_ref, b_ref, o_ref):
    s = _silu(c_ref[...]).astype(BF16)
    o_ref[0] = _dot(s, w_ref[0].astype(BF16)) + b_ref[0]


def _ada_call(c8, w, b):
    nl, d, n3 = w.shape
    tn = 768
    return pl.pallas_call(
        _ada_kernel,
        grid=(nl, n3 // tn),
        in_specs=[pl.BlockSpec((8, d), lambda l, n: (0, 0)),
                  pl.BlockSpec((1, d, tn), lambda l, n: (l, 0, n)),
                  pl.BlockSpec((1, 1, tn), lambda l, n: (l, 0, n))],
        out_specs=pl.BlockSpec((1, 8, tn), lambda l, n: (l, 0, n)),
        out_shape=jax.ShapeDtypeStruct((nl, 8, n3), F32),
        compiler_params=_params(("parallel", "parallel")),
        name="ada",
    )(c8, w, b.reshape(nl, 1, n3))


def _norm_proj_kernel(h_ref, g_ref, sc_ref, sh_ref, w_ref, o_ref, *, tn):
    x = h_ref[0]
    ms = jnp.mean(x * x, axis=-1, keepdims=True)
    z = (x * lax.rsqrt(ms + NORM_EPS) * g_ref[...]) * (1.0 + sc_ref[0]) + sh_ref[0]
    z = z.astype(BF16)
    n = w_ref.shape[1]
    for n0 in range(0, n, tn):
        n1 = min(n0 + tn, n)
        o_ref[0, :, n0:n1] = _dot(z, w_ref[:, n0:n1]).astype(o_ref.dtype)


def _norm_proj_call(h, g, scale, shift, w, tm, tn):
    bsz, seq, d = h.shape
    n = w.shape[1]
    return pl.pallas_call(
        functools.partial(_norm_proj_kernel, tn=tn),
        grid=(bsz, seq // tm),
        in_specs=[pl.BlockSpec((1, tm, d), lambda b, t: (b, t, 0)),
                  _resident((1, d)),
                  pl.BlockSpec((1, 1, d), lambda b, t: (b, 0, 0)),
                  pl.BlockSpec((1, 1, d), lambda b, t: (b, 0, 0)),
                  _resident((d, n))],
        out_specs=pl.BlockSpec((1, tm, n), lambda b, t: (b, t, 0)),
        out_shape=jax.ShapeDtypeStruct((bsz, seq, n), BF16),
        compiler_params=_params(("parallel", "parallel")),
        name="norm_proj",
    )(h, g.reshape(1, d), scale, shift, w)


def _finish(y, post_ref, gate_ref, h_ref, o_ref):
    ms = jnp.mean(y * y, axis=-1, keepdims=True)
    yn = y * lax.rsqrt(ms + NORM_EPS) * post_ref[...]
    o_ref[0] = h_ref[0] + gate_ref[0] * yn


def _out_even_kernel(ya_ref, yb_ref, w_ref, post_ref, gate_ref, h_ref, o_ref):
    half = ya_ref.shape[2]
    y = _dot(ya_ref[0], w_ref[:half, :]) + _dot(yb_ref[0], w_ref[half:, :])
    _finish(y, post_ref, gate_ref, h_ref, o_ref)


def _out_odd_kernel(o_in_ref, g_ref, w_ref, post_ref, gate_ref, h_ref, o_ref):
    yin = o_in_ref[0].astype(F32) * _silu(g_ref[0].astype(F32))
    y = _dot(yin.astype(BF16), w_ref[...])
    _finish(y, post_ref, gate_ref, h_ref, o_ref)


def _out_call(kernel, acts, act_specs, w, post, gate, h, tm):
    bsz, seq, d = h.shape
    return pl.pallas_call(
        kernel,
        grid=(bsz, seq // tm),
        in_specs=act_specs + [
            _resident(w.shape),
            _resident((1, d)),
            pl.BlockSpec((1, 1, d), lambda b, t: (b, 0, 0)),
            pl.BlockSpec((1, tm, d), lambda b, t: (b, t, 0))],
        out_specs=pl.BlockSpec((1, tm, d), lambda b, t: (b, t, 0)),
        out_shape=jax.ShapeDtypeStruct((bsz, seq, d), F32),
        compiler_params=_params(("parallel", "parallel")),
        name=kernel.__name__.strip("_"),
    )(*acts, w, post.reshape(1, d), gate, h)


def _rwkv_kernel(r_ref, k_ref, v_ref, g_ref, lo_ref,
                 mu_ref, mulo_ref, w0_ref, w2_ref, a0_ref, a2_ref, kk_ref, ka_ref, rk_ref,
                 lng_ref, lnb_ref, hsum_ref, hexp_ref,
                 o_ref,
                 carry_ref, carrylo_ref, state_ref, y_ref, *bufs, tb, ch):
    width = r_ref.shape[2]
    npair = width // LANES
    nch = tb // ch
    c2 = 2 * ch
    nsq = int(math.log2(ch)) - 1
    step_id = pl.program_id(1)
    set_a, set_b = bufs[:len(bufs) // 2], bufs[len(bufs) // 2:]

    @pl.when(step_id == 0)
    def _():
        carry_ref[...] = jnp.zeros_like(carry_ref)
        carrylo_ref[...] = jnp.zeros_like(carrylo_ref)
        for ref in set_b:
            ref[...] = jnp.zeros_like(ref)

    @pl.when(step_id <= 1)
    def _():
        state_ref[...] = jnp.zeros_like(state_ref)

    def run(wr, rd):
        rx_w, remx_w, repd_w, kd_w, repe_w, ke_w, vv_w, pc_w, bonus_w, sg_w = wr
        rx_r, remx_r, repd_r, kd_r, repe_r, ke_r, vv_r, pc_r, bonus_r, sg_r = rd
        cat = jnp.concatenate

        def headsum(t):
            sums = _dot(t.astype(BF16), hsum_ref[...])
            hi = sums.astype(BF16)
            lo = (sums - hi.astype(F32)).astype(BF16)
            return _dot(cat([hi, lo], axis=1), hexp_ref[...])

        pi = lax.broadcasted_iota(jnp.int32, (c2, c2), 0)
        pj = lax.broadcasted_iota(jnp.int32, (c2, c2), 1)
        blk = (pi // ch) == (pj // ch)
        strict = blk & (pj < pi)
        incl = blk & (pj <= pi)
        m0 = lax.broadcasted_iota(jnp.int32, (ch, LANES), 1) < RWKV_HEAD

        def by_head(t):
            z = jnp.zeros_like(t)
            return cat([jnp.where(m0, t, z), jnp.where(m0, z, t)], axis=0)

        def chunk(c):
            rows = slice(c * ch, (c + 1) * ch)
            pairs = range(npair)
            cols = [slice(p * LANES, (p + 1) * LANES) for p in pairs]
            lhs = [cat([by_head(remx_r[rows, cols[p]]), by_head(rx_r[rows, cols[p]])], axis=0)
                   for p in pairs]
            m1 = [_dot_nt(lhs[p], cat([repd_r[rows, cols[p]]] * 2 + [kd_r[rows, cols[p]]] * 2, axis=0))
                  for p in pairs]
            s_old = [state_ref[p] for p in pairs]
            gs = [_dot_nt(lhs[p], s_old[p].astype(BF16)) for p in pairs]
            v_bd = [by_head(vv_r[rows, cols[p]]) for p in pairs]
            sa = [gs[p][:c2] + _dot(jnp.where(strict, m1[p][:c2, c2:], 0.0).astype(BF16), v_bd[p])
                  for p in pairs]
            pw = [jnp.where(strict, m1[p][:c2, :c2], 0.0) for p in pairs]
            for _i in range(nsq):
                both = [_dot(pw[p].astype(BF16), cat([pw[p], sa[p]], axis=1).astype(BF16)) for p in pairs]
                pw = [both[p][:, :c2] for p in pairs]
                sa = [sa[p] + both[p][:, c2:] for p in pairs]
            sa = [(sa[p] + _dot(pw[p].astype(BF16), sa[p].astype(BF16))).astype(BF16) for p in pairs]
            vals = [cat([sa[p], v_bd[p]], axis=0) for p in pairs]
            for p in pairs:
                arbr = cat([jnp.where(incl, m1[p][c2:, :c2], 0.0), jnp.where(incl, m1[p][c2:, c2:], 0.0)],
                           axis=1).astype(BF16)
                y_bd = gs[p][c2:] + _dot(arbr, vals[p])
                y_ref[rows, cols[p]] = y_bd[:ch] + y_bd[ch:]
            for p in pairs:
                upd = _dot_tn(vals[p], cat([by_head(repe_r[rows, cols[p]]), by_head(ke_r[rows, cols[p]])],
                                           axis=0))
                state_ref[p] = s_old[p] * pc_r[8 * c:8 * c + 1, cols[p]] + upd

        row0 = lax.broadcasted_iota(jnp.int32, (tb, 1), 0) == 0
        env = {}

        def shift(x, cref, slot, mu):
            prev = jnp.where(row0, cref[slot:slot + 1, :], pltpu.roll(x, 1, axis=0))
            cref[slot:slot + 1, :] = x[tb - 1:tb, :]
            return x + (prev - x) * mu

        def stage_shift():
            env["r"] = shift(r_ref[0].astype(F32), carry_ref, 0, mu_ref[0:1, :])
            env["k"] = shift(k_ref[0].astype(F32), carry_ref, 1, mu_ref[1:2, :])
            v = shift(v_ref[0].astype(F32), carry_ref, 2, mu_ref[2:3, :])
            g = shift(g_ref[0].astype(F32), carry_ref, 3, mu_ref[3:4, :])
            lo = shift(lo_ref[0].astype(F32), carrylo_ref, 0, mulo_ref[...])
            env["v"] = v
            vv_w[...] = v.astype(BF16)
            sg_w[...] = _silu(g)
            x = w0_ref[...] + _dot(jnp.tanh(lo[:, :RWKV_LORA]).astype(BF16), w2_ref[...])
            y = -x
            softplus = jnp.maximum(y, 0.0) + jnp.log(1.0 + jnp.exp(-jnp.abs(y)))
            env["ld"] = -jnp.exp(-softplus - 0.5)
            env["a"] = _sigmoid(a0_ref[...] + _dot(lo[:, RWKV_LORA:].astype(BF16), a2_ref[...]))

        def stage_keys():
            r, k, v, a = env["r"], env["k"], env["v"], env["a"]
            kk = k * kk_ref[...]
            k2 = k * (1.0 + (a - 1.0) * ka_ref[...])
            sums = headsum(cat([kk * kk, r * k2 * rk_ref[...]], axis=0))
            kk = kk * lax.rsqrt(jnp.maximum(sums[:tb], 1e-24))
            bonus_w[...] = sums[tb:] * v
            env["kk"], env["k2"], env["rep"] = kk, k2, kk * a

        def stage_decay():
            ld = env["ld"]
            ri = lax.broadcasted_iota(jnp.int32, (tb, tb), 0)
            ci = lax.broadcasted_iota(jnp.int32, (tb, tb), 1)
            lincl = jnp.where(((ri // ch) == (ci // ch)) & (ci <= ri), 1.0, 0.0).astype(BF16)
            hi = ld.astype(BF16)
            r1 = ld - hi.astype(F32)
            mid = r1.astype(BF16)
            low = (r1 - mid.astype(F32)).astype(BF16)
            parts = _dot(lincl, cat([hi, mid, low], axis=1))
            env["cum"] = parts[:, :width] + parts[:, width:2 * width] + parts[:, 2 * width:]

        def stage_operands():
            r, kk, k2, rep, ld, cum = (env[n] for n in ("r", "kk", "k2", "rep", "ld", "cum"))
            pinv = jnp.exp(-cum)
            pend = []
            for s in range(nch):
                pc = jnp.exp(cum[(s + 1) * ch - 1:(s + 1) * ch, :])
                pc_w[8 * s:8 * s + 8, :] = jnp.broadcast_to(pc, (8, width))
                pend.append(pc * pinv[s * ch:(s + 1) * ch, :])
            pend = cat(pend, axis=0)
            rx_w[...] = (r * jnp.exp(cum)).astype(BF16)
            remx_w[...] = (-kk * jnp.exp(cum - ld)).astype(BF16)
            repd_w[...] = (rep * pinv).astype(BF16)
            kd_w[...] = (k2 * pinv).astype(BF16)
            repe_w[...] = (rep * pend).astype(BF16)
            ke_w[...] = (k2 * pend).astype(BF16)

        stages = [stage_shift, stage_keys, stage_decay, stage_operands]
        for c in range(nch):
            chunk(c)
            if c < len(stages):
                stages[c]()
        for stage in stages[nch:]:
            stage()

        yv = y_ref[...]
        inv_n = 1.0 / RWKV_HEAD
        mean = headsum(yv) * inv_n
        dlt = yv - mean
        var = headsum(dlt * dlt) * inv_n
        yn = dlt * lax.rsqrt(var + LNX_EPS) * lng_ref[...] + lnb_ref[...]
        o_ref[0] = ((yn + bonus_r[...]) * sg_r[...]).astype(o_ref.dtype)

    @pl.when(step_id % 2 == 0)
    def _():
        run(set_a, set_b)

    @pl.when(step_id % 2 == 1)
    def _():
        run(set_b, set_a)


def _rwkv_call(p, mu, mulo, w0, w2, a0, a2, k_k, k_a, r_k, lnx_g, lnx_b, tb, ch):
    bsz, seq, _ = p.shape
    width = w0.shape[-1]
    head_of = jnp.arange(width) // RWKV_HEAD
    hsum = (head_of[:, None] == jnp.arange(LANES)[None, :]).astype(BF16)
    hexp = jnp.concatenate([hsum.T, hsum.T], axis=0)
    npair = width // LANES
    nblk = seq // tb
    lo_blk = p.shape[-1] // LANES - 1
    act = lambda j: pl.BlockSpec((1, tb, width), lambda b, s, j=j: (b, jnp.minimum(s, nblk - 1), j))
    row = lambda a: a.reshape(1, width)
    bf = lambda: pltpu.VMEM((tb, width), BF16)
    f32 = lambda: pltpu.VMEM((tb, width), F32)
    buf_set = lambda: [bf(), bf(), bf(), bf(), bf(), bf(), bf(),
                       pltpu.VMEM((8 * (tb // ch), width), F32), f32(), f32()]
    return pl.pallas_call(
        functools.partial(_rwkv_kernel, tb=tb, ch=ch),
        grid=(bsz, nblk + 1),
        in_specs=[act(0), act(1), act(2), act(3),
                  pl.BlockSpec((1, tb, LANES), lambda b, s: (b, jnp.minimum(s, nblk - 1), lo_blk)),
                  _resident((4, width)), _resident((1, LANES)),
                  _resident((1, width)), _resident((RWKV_LORA, width)),
                  _resident((1, width)), _resident((RWKV_LORA, width)),
                  _resident((1, width)), _resident((1, width)), _resident((1, width)),
                  _resident((1, width)), _resident((1, width)),
                  _resident(hsum.shape), _resident(hexp.shape)],
        out_specs=pl.BlockSpec((1, tb, width), lambda b, s: (b, jnp.maximum(s - 1, 0), 0)),
        out_shape=jax.ShapeDtypeStruct((bsz, seq, width), BF16),
        scratch_shapes=[pltpu.VMEM((8, width), F32), pltpu.VMEM((8, LANES), F32),
                        pltpu.VMEM((npair, LANES, LANES), F32), f32()] + buf_set() + buf_set(),
        compiler_params=_params(("parallel", "arbitrary")),
        name="rwkv",
    )(p, p, p, p, p, mu, mulo, row(w0), w2, row(a0), a2, row(k_k), row(k_a), row(r_k),
      row(lnx_g), row(lnx_b), hsum, hexp)


def _s5_kernel(u_ref, gate_ref, bblk_ref, cblk_ref, dnr_ref, dni_ref, dpr_ref, dpi_ref,
               lre_ref, lim_ref, d_ref, gw_ref, gb_ref, o_ref, xr_ref, xi_ref, y_ref, *, ts, cs):
    @pl.when(pl.program_id(1) == 0)
    def _():
        xr_ref[...] = jnp.zeros_like(xr_ref)
        xi_ref[...] = jnp.zeros_like(xi_ref)

    ntile, ucols, scols2 = bblk_ref.shape
    scols = scols2 // 2
    nsub = ts // cs
    u_bf = u_ref[0]
    ri = lax.broadcasted_iota(jnp.int32, (ts, ts), 0)
    ci = lax.broadcasted_iota(jnp.int32, (ts, ts), 1)
    ltri = jnp.where(((ri // cs) == (ci // cs)) & (ci <= ri), 1.0, 0.0).astype(BF16)

    for t in range(ntile):
        sc = slice(t * scols, (t + 1) * scols)
        bu = _dot(u_bf[:, t * ucols:(t + 1) * ucols], bblk_ref[t])
        dnr, dni = dnr_ref[:, sc], dni_ref[:, sc]
        dpr, dpi = dpr_ref[:, sc], dpi_ref[:, sc]
        zr, zi = [], []
        for s in range(nsub):
            br = bu[s * cs:(s + 1) * cs, :scols]
            bi = bu[s * cs:(s + 1) * cs, scols:]
            zr.append(br * dnr - bi * dni)
            zi.append(br * dni + bi * dnr)
        z = jnp.concatenate([jnp.concatenate(zr, axis=0), jnp.concatenate(zi, axis=0)], axis=1)
        csum = _dot(ltri, z.astype(BF16))
        lre, lim = lre_ref[:, sc], lim_ref[:, sc]
        pr, pi_ = xr_ref[0:1, sc], xi_ref[0:1, sc]
        xr, xi = [], []
        for s in range(nsub):
            ar = lre * pr - lim * pi_
            ai = lre * pi_ + lim * pr
            cr = csum[s * cs:(s + 1) * cs, :scols] + ar
            cim = csum[s * cs:(s + 1) * cs, scols:] + ai
            xs_r = cr * dpr - cim * dpi
            xs_i = cr * dpi + cim * dpr
            pr, pi_ = xs_r[cs - 1:cs, :], xs_i[cs - 1:cs, :]
            xr.append(xs_r)
            xi.append(xs_i)
        xr_ref[0:1, sc] = pr
        xi_ref[0:1, sc] = pi_
        xfull = jnp.concatenate([jnp.concatenate(xr, axis=0), jnp.concatenate(xi, axis=0)], axis=1)
        y_ref[:, t * ucols:(t + 1) * ucols] = _dot(xfull.astype(BF16), cblk_ref[t])

    y = y_ref[...] + d_ref[...] * u_bf.astype(F32)
    cdf = 0.5 * (1.0 + jnp.tanh(math.sqrt(2.0 / math.pi) * (y + 0.044715 * (y * y * y))))
    y = y * cdf
    glu = _dot(y.astype(BF16), gw_ref[...]) + gb_ref[...]
    o_ref[0] = (y * _sigmoid(glu) * _silu(gate_ref[0].astype(F32))).astype(o_ref.dtype)


def _s5_tables(lam_re, lam_im, log_dt, b_re, b_im, c_re, c_im, cs):
    g, n = lam_re.shape
    pch = b_re.shape[-1]
    nt = g // S5_GT
    dt = jnp.exp(log_dt)[:, None]
    mag = jnp.exp(lam_re * dt)
    e_re, e_im = mag * jnp.cos(lam_im * dt), mag * jnp.sin(lam_im * dt)
    den = lam_re * lam_re + lam_im * lam_im
    coef_re = ((e_re - 1.0) * lam_re + e_im * lam_im) / den
    coef_im = (e_im * lam_re - (e_re - 1.0) * lam_im) / den
    bb_re = coef_re[..., None] * b_re - coef_im[..., None] * b_im
    bb_im = coef_re[..., None] * b_im + coef_im[..., None] * b_re
    eye = jnp.eye(S5_GT, dtype=F32)

    def btile(bb):
        return jnp.einsum('tgnq,gh->tgqhn', bb.reshape(nt, S5_GT, n, pch), eye).reshape(
            nt, S5_GT * pch, S5_GT * n)

    def ctile(cc):
        return jnp.einsum('tgpn,gh->tgnhp', cc.reshape(nt, S5_GT, pch, n), eye).reshape(
            nt, S5_GT * n, S5_GT * pch)

    bblk = jnp.concatenate([btile(bb_re), btile(bb_im)], axis=-1).astype(BF16)
    cblk = jnp.concatenate([ctile(c_re), -ctile(c_im)], axis=1).astype(BF16)
    j = jnp.arange(cs, dtype=F32)[:, None, None]
    lr, li = (lam_re * dt)[None], (lam_im * dt)[None]
    flat = lambda t: t.reshape(t.shape[0], g * n)
    dpr, dpi = flat(jnp.exp(j * lr) * jnp.cos(j * li)), flat(jnp.exp(j * lr) * jnp.sin(j * li))
    dnr, dni = flat(jnp.exp(-j * lr) * jnp.cos(j * li)), flat(-jnp.exp(-j * lr) * jnp.sin(j * li))
    return bblk, cblk, dnr, dni, dpr, dpi, e_re.reshape(1, g * n), e_im.reshape(1, g * n)


def _s5_call(p, tables, d, glu_w, glu_b, u_blk, gate_blk, ts, cs):
    bsz, seq, _ = p.shape
    width = d.shape[-1]
    bblk, cblk, dnr, dni, dpr, dpi, lre, lim = tables
    nstate = lre.shape[-1]
    res = lambda a: _resident(a.shape)
    return pl.pallas_call(
        functools.partial(_s5_kernel, ts=ts, cs=cs),
        grid=(bsz, seq // ts),
        in_specs=[pl.BlockSpec((1, ts, width), lambda b, t: (b, t, u_blk)),
                  pl.BlockSpec((1, ts, width), lambda b, t: (b, t, gate_blk)),
                  res(bblk), res(cblk), res(dnr), res(dni), res(dpr), res(dpi), res(lre), res(lim),
                  _resident((1, width)), res(glu_w), _resident((1, width))],
        out_specs=pl.BlockSpec((1, ts, width), lambda b, t: (b, t, 0)),
        out_shape=jax.ShapeDtypeStruct((bsz, seq, width), BF16),
        scratch_shapes=[pltpu.VMEM((8, nstate), F32), pltpu.VMEM((8, nstate), F32),
                        pltpu.VMEM((ts, width), F32)],
        compiler_params=_params(("parallel", "arbitrary")),
        name="s5",
    )(p, p, bblk, cblk, dnr, dni, dpr, dpi, lre, lim, d.reshape(1, width), glu_w,
      glu_b.reshape(1, width))


def _rope128(x, cos, sin):
    lane = lax.broadcasted_iota(jnp.int32, x.shape, 1)
    half = QK_ROPE // 2
    partner = jnp.where(lane < half, pltpu.roll(x, LANES - half, axis=1), pltpu.roll(x, half, axis=1))
    return x * cos + partner * sin


def _qkv_kernel(cq_ref, ckv_ref, kpe_ref, cos_ref, sin_ref, qn_ref, kvn_ref, wq_ref, wkv_ref,
                q_ref, k_ref, v_ref, *, scale):
    def latent(ref, gain_ref):
        x = ref[0].astype(F32)
        ms = jnp.mean(x * x, axis=-1, keepdims=True)
        return (x * lax.rsqrt(ms + NORM_EPS) * gain_ref[...]).astype(BF16)

    cq = latent(cq_ref, qn_ref)
    ckv = latent(ckv_ref, kvn_ref)
    cos, sin = cos_ref[0], sin_ref[0]
    kpe = _rope128(kpe_ref[0].astype(F32), cos, sin).astype(BF16)
    hw = 2 * LANES
    for h in range(q_ref.shape[1]):
        qh = _dot(cq, wq_ref[:, h * hw:(h + 1) * hw]) * scale
        q_ref[0, h] = jnp.concatenate(
            [qh[:, :LANES], _rope128(qh[:, LANES:], cos, sin)], axis=1).astype(BF16)
        kvh = _dot(ckv, wkv_ref[:, h * hw:(h + 1) * hw])
        k_ref[0, h] = jnp.concatenate([kvh[:, :LANES].astype(BF16), kpe], axis=1)
        v_ref[0, h] = kvh[:, LANES:].astype(BF16)


def _qkv_call(p, cos, sin, q_norm, kv_norm, wq, wkv, cq_blk, ckv_blk, kpe_blk, tm):
    bsz, seq, _ = p.shape
    nh = MLA_HEADS
    hw = 2 * LANES
    scale = math.log2(math.e) / math.sqrt(QK_NOPE + QK_ROPE)
    head_out = lambda w: pl.BlockSpec((1, nh, tm, w), lambda b, t: (b, 0, t, 0))
    return pl.pallas_call(
        functools.partial(_qkv_kernel, scale=scale),
        grid=(bsz, seq // tm),
        in_specs=[pl.BlockSpec((1, tm, Q_LORA), lambda b, t: (b, t, cq_blk)),
                  pl.BlockSpec((1, tm, KV_LORA), lambda b, t: (b, t, ckv_blk)),
                  pl.BlockSpec((1, tm, LANES), lambda b, t: (b, t, kpe_blk)),
                  pl.BlockSpec((1, tm, LANES), lambda b, t: (b, t, 0)),
                  pl.BlockSpec((1, tm, LANES), lambda b, t: (b, t, 0)),
                  _resident((1, Q_LORA)), _resident((1, KV_LORA)),
                  _resident(wq.shape), _resident(wkv.shape)],
        out_specs=[head_out(hw), head_out(hw), head_out(V_HEAD)],
        out_shape=[jax.ShapeDtypeStruct((bsz, nh, seq, hw), BF16),
                   jax.ShapeDtypeStruct((bsz, nh, seq, hw), BF16),
                   jax.ShapeDtypeStruct((bsz, nh, seq, V_HEAD), BF16)],
        compiler_params=_params(("parallel", "parallel")),
        name="qkv",
    )(p, p, p, cos, sin, q_norm.reshape(1, Q_LORA), kv_norm.reshape(1, KV_LORA), wq, wkv)


def _attn_kernel(q_ref, k_ref, v_ref, o_ref, *, tq, tk):
    qi = pl.program_id(2)
    nsub = tq // tk
    vd = v_ref.shape[3]
    qs = [q_ref[0, 0, i * tk:(i + 1) * tk, :] for i in range(nsub)]
    diag = (lax.broadcasted_iota(jnp.int32, (tk, tk), 0)
            >= lax.broadcasted_iota(jnp.int32, (tk, tk), 1))

    def scores(j, kinds):
        rows = pl.ds(pl.multiple_of(j * tk, tk), tk)
        kb = k_ref[0, 0, rows, :]
        return [None if kinds[i] is None else _dot_nt(qs[i], kb) for i in range(nsub)]

    def update(j, carries, kinds, ss):
        rows = pl.ds(pl.multiple_of(j * tk, tk), tk)
        vb = jnp.concatenate([v_ref[0, 0, rows, :], jnp.ones((tk, vd), BF16)], axis=1)
        out = []
        for i in range(nsub):
            if kinds[i] is None:
                out.append(carries[i])
                continue
            m, acc = carries[i]
            s = jnp.where(diag, ss[i], -1e30) if kinds[i] else ss[i]
            m_new = jnp.maximum(m, jnp.max(s, axis=-1, keepdims=True))
            pexp = jnp.exp2(s - m_new).astype(BF16)
            acc = jnp.exp2(m - m_new) * acc + _dot(pexp, vb)
            out.append((m_new, acc))
        return tuple(out)

    def blocks(js, carries, kinds_list):
        ss = [scores(j, kinds) for j, kinds in zip(js, kinds_list)]
        for j, kinds, s in zip(js, kinds_list, ss):
            carries = update(j, carries, kinds, s)
        return carries

    init = tuple((jnp.full((tk, 1), -1e30, F32), jnp.zeros((tk, 2 * vd), F32)) for _ in range(nsub))
    nfull = qi * nsub
    visible = [[False] * nsub] * nsub
    carries = lax.fori_loop(
        0, qi, lambda jq, c: blocks([jq * nsub + d for d in range(nsub)], c, visible), init)
    tail = [[None if i < d else (i == d) for i in range(nsub)] for d in range(nsub)]
    carries = blocks([nfull + d for d in range(nsub)], carries, tail)
    for i in range(nsub):
        _, acc = carries[i]
        o_ref[0, i * tk:(i + 1) * tk, :] = (acc[:, :vd] / acc[:, vd:]).astype(o_ref.dtype)


def _attn_call(q, k, v, tq, tk):
    bsz, nh, seq, hw = q.shape
    vd = v.shape[3]
    return pl.pallas_call(
        functools.partial(_attn_kernel, tq=tq, tk=tk),
        grid=(bsz, nh, seq // tq),
        in_specs=[pl.BlockSpec((1, 1, tq, hw), lambda b, h, i: (b, h, i, 0)),
                  pl.BlockSpec((1, 1, seq, hw), lambda b, h, i: (b, h, 0, 0)),
                  pl.BlockSpec((1, 1, seq, vd), lambda b, h, i: (b, h, 0, 0))],
        out_specs=pl.BlockSpec((1, tq, vd), lambda b, h, i: (b, i, h)),
        out_shape=jax.ShapeDtypeStruct((bsz, seq, nh * vd), BF16),
        compiler_params=_params(("parallel", "parallel", "arbitrary")),
        name="attn",
    )(q, k, v)


def _rope_tables(positions):
    inv_freq = 1.0 / (ROPE_BASE ** (jnp.arange(0, QK_ROPE, 2, dtype=F32) / QK_ROPE))
    ang = positions.astype(F32)[..., None] * inv_freq
    cos, sin = jnp.cos(ang), jnp.sin(ang)
    zero = jnp.zeros_like(cos)
    return (jnp.concatenate([cos, cos, zero, zero], axis=-1),
            jnp.concatenate([-sin, sin, zero, zero], axis=-1))


def _modulation(ada, j, bsz, d):
    m = ada[j, :bsz]
    part = lambda i: m[:, i * d:(i + 1) * d].reshape(bsz, 1, d)
    return part(0), part(1), part(2)


def kernel(x, c, positions, ev_ada_w, ev_ada_b, ev_norm_pre, ev_norm_post, ev_w_in, ev_mu, ev_w0, ev_w2, ev_a0, ev_a2, ev_k_k, ev_k_a, ev_r_k, ev_lnx_g, ev_lnx_b, ev_lam_re, ev_lam_im, ev_log_dt, ev_b_re, ev_b_im, ev_c_re, ev_c_im, ev_d, ev_glu_w, ev_glu_b, ev_w_out, od_ada_w, od_ada_b, od_norm_pre, od_norm_post, od_w_in, od_q_norm, od_w_q_up, od_kv_norm, od_w_kv_up, od_w_out):
    bsz, seq, d = x.shape
    n_even, n_odd = ev_w_in.shape[0], od_w_in.shape[0]
    depth = n_even + n_odd
    rw = ev_w0.shape[-1]
    sw = ev_d.shape[-1]
    a_in = 4 * rw + 2 * RWKV_LORA
    h = x.astype(F32)

    c8 = jnp.zeros((8, d), F32).at[:bsz].set(c.astype(F32))
    ev_ada = _ada_call(c8, ev_ada_w, ev_ada_b)
    od_ada = _ada_call(c8, od_ada_w, od_ada_b)
    cos, sin = _rope_tables(positions)

    tm_out = min(512, seq)
    for i in range(depth):
        j = i // 2
        if i % 2 == 0:
            shift, scale, gate = _modulation(ev_ada, j, bsz, d)
            w = ev_w_in[j]
            w_in = jnp.concatenate([w[:, :4 * rw], w[:, a_in:], w[:, 4 * rw:a_in]], axis=1).astype(BF16)
            p = _norm_proj_call(h, ev_norm_pre[j], scale, shift, w_in, tm=min(256, seq), tn=PROJ_TN)
            mu = ev_mu[j]
            y_a = _rwkv_call(p, mu[:4 * rw].reshape(4, rw), mu[4 * rw:].reshape(1, 2 * RWKV_LORA),
                             ev_w0[j], ev_w2[j].astype(BF16), ev_a0[j], ev_a2[j].astype(BF16),
                             ev_k_k[j], ev_k_a[j],
                             ev_r_k[j], ev_lnx_g[j], ev_lnx_b[j],
                             tb=min(RWKV_TB, seq), ch=RWKV_CHUNK)
            tables = _s5_tables(ev_lam_re[j], ev_lam_im[j], ev_log_dt[j], ev_b_re[j], ev_b_im[j],
                                ev_c_re[j], ev_c_im[j], S5_CHUNK)
            y_b = _s5_call(p, tables, ev_d[j], ev_glu_w[j].astype(BF16), ev_glu_b[j],
                           u_blk=(4 * rw) // sw, gate_blk=(4 * rw) // sw + 1,
                           ts=min(S5_TS, seq), cs=S5_CHUNK)
            half = lambda t: pl.BlockSpec((1, tm_out, rw), lambda b, t_: (b, t_, 0))
            h = _out_call(_out_even_kernel, [y_a, y_b], [half(0), half(1)],
                          ev_w_out[j].astype(BF16), ev_norm_post[j], gate, h, tm_out)
        else:
            shift, scale, gate = _modulation(od_ada, j, bsz, d)
            w = od_w_in[j]
            o1, o2, o3 = Q_LORA, Q_LORA + KV_LORA, Q_LORA + KV_LORA + QK_ROPE
            mw = w.shape[1] - o3
            w_in = jnp.concatenate([w[:, o3:], w[:, :o3],
                                    jnp.zeros((d, LANES - QK_ROPE), w.dtype)], axis=1).astype(BF16)
            p = _norm_proj_call(h, od_norm_pre[j], scale, shift, w_in, tm=min(512, seq), tn=PROJ_TN)
            nh = MLA_HEADS
            wq = od_w_q_up[j].reshape(Q_LORA, nh, QK_NOPE + QK_ROPE)
            wq = jnp.pad(wq, ((0, 0), (0, 0), (0, 2 * LANES - QK_NOPE - QK_ROPE)))
            wq = wq.reshape(Q_LORA, nh * 2 * LANES).astype(BF16)
            wkv = od_w_kv_up[j].astype(BF16)
            q, k, v = _qkv_call(p, cos, sin, od_q_norm[j], od_kv_norm[j], wq, wkv,
                                cq_blk=mw // Q_LORA, ckv_blk=mw // KV_LORA + 1,
                                kpe_blk=(mw + o2) // LANES, tm=min(512, seq))
            o = _attn_call(q, k, v, tq=min(2048, seq), tk=min(512, seq))
            full = lambda jblk: pl.BlockSpec((1, tm_out, mw), lambda b, t_, jblk=jblk: (b, t_, jblk))
            h = _out_call(_out_odd_kernel, [o, p], [full(0), full(0)],
                          od_w_out[j].astype(BF16), od_norm_post[j], gate, h, tm_out)
    return h.astype(x.dtype)
```

```python
import functools
import math

import jax
import jax.numpy as jnp
from jax import lax
from jax.experimental import pallas as pl
from jax.experimental.pallas import tpu as pltpu

F32 = jnp.float32
BF16 = jnp.bfloat16

NORM_EPS = 1e-6
LNX_EPS = 64e-5
ROPE_BASE = 10000.0

RWKV_HEAD = 64
RWKV_LORA = 64
S5_GROUP = 16
S5_STATE = 64
MLA_HEADS = 16
QK_NOPE = 128
QK_ROPE = 64
V_HEAD = 128
Q_LORA = 512
KV_LORA = 512

LANES = 128
V7X_VMEM_LIMIT = 56 * 1024 * 1024

RWKV_CHUNK = 64
RWKV_TB = 256
S5_CHUNK = 64
S5_TS = 256
S5_GT = 16
PROJ_TN = 1024


def _params(sem, vmem=V7X_VMEM_LIMIT):
    return pltpu.CompilerParams(dimension_semantics=sem, vmem_limit_bytes=vmem)


def _sigmoid(x):
    return 1.0 / (1.0 + jnp.exp(-x))


def _silu(x):
    return x * _sigmoid(x)


def _dot(a, b):
    return jnp.dot(a, b, preferred_element_type=F32)


def _dot_nt(a, b):
    return lax.dot_general(a, b, (((1,), (1,)), ((), ())), preferred_element_type=F32)


def _dot_tn(a, b):
    return lax.dot_general(a, b, (((0,), (0,)), ((), ())), preferred_element_type=F32)


def _resident(shape):
    nd = len(shape)
    return pl.BlockSpec(shape, lambda *_: (0,) * nd, pipeline_mode=pl.Buffered(1))


def _layer_resident(stacked, layer):
    nd = stacked.ndim
    return pl.BlockSpec((None,) + stacked.shape[1:], lambda *_: (layer,) + (0,) * (nd - 1),
                        pipeline_mode=pl.Buffered(1))


def _ada_kernel(c_ref, w_ref, b_ref, o_ref):
    s = _silu(c_ref[...]).astype(BF16)
    o_ref[0] = _dot(s, w_ref[0].astype(BF16)) + b_ref[0]


def _ada_call(c8, w, b):
    nl, d, n3 = w.shape
    tn = 768
    return pl.pallas_call(
        _ada_kernel,
        grid=(nl, n3 // tn),
        in_specs=[pl.BlockSpec((8, d), lambda l, n: (0, 0)),
                  pl.BlockSpec((1, d, tn), lambda l, n: (l, 0, n)),
                  pl.BlockSpec((1, 1, tn), lambda l, n: (l, 0, n))],
        out_specs=pl.BlockSpec((1, 8, tn), lambda l, n: (l, 0, n)),
        out_shape=jax.ShapeDtypeStruct((nl, 8, n3), F32),
        compiler_params=_params(("parallel", "parallel")),
        name="ada",
    )(c8, w, b.reshape(nl, 1, n3))


def _norm_proj_kernel(h_ref, g_ref, sc_ref, sh_ref, w_ref, o_ref, *, tn):
    x = h_ref[0]
    ms = jnp.mean(x * x, axis=-1, keepdims=True)
    z = (x * lax.rsqrt(ms + NORM_EPS) * g_ref[...]) * (1.0 + sc_ref[0]) + sh_ref[0]
    z = z.astype(BF16)
    n = w_ref.shape[1]
    for n0 in range(0, n, tn):
        n1 = min(n0 + tn, n)
        o_ref[0, :, n0:n1] = _dot(z, w_ref[:, n0:n1]).astype(o_ref.dtype)


def _norm_proj_call(h, g, scale, shift, w, layer, tm, tn):
    bsz, seq, d = h.shape
    n = w.shape[2]
    return pl.pallas_call(
        functools.partial(_norm_proj_kernel, tn=tn),
        grid=(bsz, seq // tm),
        in_specs=[pl.BlockSpec((1, tm, d), lambda b, t: (b, t, 0)),
                  _resident((1, d)),
                  pl.BlockSpec((1, 1, d), lambda b, t: (b, 0, 0)),
                  pl.BlockSpec((1, 1, d), lambda b, t: (b, 0, 0)),
                  _layer_resident(w, layer)],
        out_specs=pl.BlockSpec((1, tm, n), lambda b, t: (b, t, 0)),
        out_shape=jax.ShapeDtypeStruct((bsz, seq, n), BF16),
        compiler_params=_params(("parallel", "parallel")),
        name="norm_proj",
    )(h, g.reshape(1, d), scale, shift, w)


def _finish(y, post_ref, gate_ref, h_ref, o_ref):
    ms = jnp.mean(y * y, axis=-1, keepdims=True)
    yn = y * lax.rsqrt(ms + NORM_EPS) * post_ref[...]
    o_ref[0] = h_ref[0] + gate_ref[0] * yn


def _out_even_kernel(ya_ref, yb_ref, w_ref, post_ref, gate_ref, h_ref, o_ref):
    half = ya_ref.shape[2]
    y = _dot(ya_ref[0], w_ref[:half, :]) + _dot(yb_ref[0], w_ref[half:, :])
    _finish(y, post_ref, gate_ref, h_ref, o_ref)


def _out_odd_kernel(o_in_ref, g_ref, w_ref, post_ref, gate_ref, h_ref, o_ref):
    yin = o_in_ref[0].astype(F32) * _silu(g_ref[0].astype(F32))
    y = _dot(yin.astype(BF16), w_ref[...])
    _finish(y, post_ref, gate_ref, h_ref, o_ref)


def _out_call(kernel, acts, act_specs, w, layer, post, gate, h, tm):
    bsz, seq, d = h.shape
    return pl.pallas_call(
        kernel,
        grid=(bsz, seq // tm),
        in_specs=act_specs + [
            _layer_resident(w, layer),
            _resident((1, d)),
            pl.BlockSpec((1, 1, d), lambda b, t: (b, 0, 0)),
            pl.BlockSpec((1, tm, d), lambda b, t: (b, t, 0))],
        out_specs=pl.BlockSpec((1, tm, d), lambda b, t: (b, t, 0)),
        out_shape=jax.ShapeDtypeStruct((bsz, seq, d), F32),
        compiler_params=_params(("parallel", "parallel")),
        name=kernel.__name__.strip("_"),
    )(*acts, w, post.reshape(1, d), gate, h)


def _rwkv_kernel(r_ref, k_ref, v_ref, g_ref, lo_ref,
                 mu_ref, mulo_ref, w0_ref, w2_ref, a0_ref, a2_ref, kk_ref, ka_ref, rk_ref,
                 lng_ref, lnb_ref, hsum_ref, hexp_ref,
                 o_ref,
                 carry_ref, carrylo_ref, state_ref, y_ref, *bufs, tb, ch):
    width = r_ref.shape[2]
    npair = width // LANES
    nch = tb // ch
    c2 = 2 * ch
    nsq = int(math.log2(ch)) - 1
    step_id = pl.program_id(1)
    set_a, set_b = bufs[:len(bufs) // 2], bufs[len(bufs) // 2:]

    @pl.when(step_id == 0)
    def _():
        carry_ref[...] = jnp.zeros_like(carry_ref)
        carrylo_ref[...] = jnp.zeros_like(carrylo_ref)
        for ref in set_b:
            ref[...] = jnp.zeros_like(ref)

    @pl.when(step_id <= 1)
    def _():
        state_ref[...] = jnp.zeros_like(state_ref)

    def run(wr, rd):
        rx_w, remx_w, repd_w, kd_w, repe_w, ke_w, vv_w, pc_w, bonus_w, sg_w = wr
        rx_r, remx_r, repd_r, kd_r, repe_r, ke_r, vv_r, pc_r, bonus_r, sg_r = rd
        cat = jnp.concatenate

        def headsum(t):
            sums = _dot(t.astype(BF16), hsum_ref[...])
            hi = sums.astype(BF16)
            lo = (sums - hi.astype(F32)).astype(BF16)
            return _dot(cat([hi, lo], axis=1), hexp_ref[...])

        pi = lax.broadcasted_iota(jnp.int32, (c2, c2), 0)
        pj = lax.broadcasted_iota(jnp.int32, (c2, c2), 1)
        blk = (pi // ch) == (pj // ch)
        strict = blk & (pj < pi)
        incl = blk & (pj <= pi)
        m0 = lax.broadcasted_iota(jnp.int32, (ch, LANES), 1) < RWKV_HEAD

        def by_head(t):
            z = jnp.zeros_like(t)
            return cat([jnp.where(m0, t, z), jnp.where(m0, z, t)], axis=0)

        def chunk(c):
            rows = slice(c * ch, (c + 1) * ch)
            pairs = range(npair)
            cols = [slice(p * LANES, (p + 1) * LANES) for p in pairs]
            lhs = [cat([by_head(remx_r[rows, cols[p]]), by_head(rx_r[rows, cols[p]])], axis=0)
                   for p in pairs]
            m1 = [_dot_nt(lhs[p], cat([repd_r[rows, cols[p]]] * 2 + [kd_r[rows, cols[p]]] * 2, axis=0))
                  for p in pairs]
            s_old = [state_ref[p] for p in pairs]
            gs = [_dot_nt(lhs[p], s_old[p].astype(BF16)) for p in pairs]
            v_bd = [by_head(vv_r[rows, cols[p]]) for p in pairs]
            sa = [gs[p][:c2] + _dot(jnp.where(strict, m1[p][:c2, c2:], 0.0).astype(BF16), v_bd[p])
                  for p in pairs]
            pw = [jnp.where(strict, m1[p][:c2, :c2], 0.0) for p in pairs]
            for _i in range(nsq):
                both = [_dot(pw[p].astype(BF16), cat([pw[p], sa[p]], axis=1).astype(BF16)) for p in pairs]
                pw = [both[p][:, :c2] for p in pairs]
                sa = [sa[p] + both[p][:, c2:] for p in pairs]
            sa = [(sa[p] + _dot(pw[p].astype(BF16), sa[p].astype(BF16))).astype(BF16) for p in pairs]
            vals = [cat([sa[p], v_bd[p]], axis=0) for p in pairs]
            for p in pairs:
                arbr = cat([jnp.where(incl, m1[p][c2:, :c2], 0.0), jnp.where(incl, m1[p][c2:, c2:], 0.0)],
                           axis=1).astype(BF16)
                y_bd = gs[p][c2:] + _dot(arbr, vals[p])
                y_ref[rows, cols[p]] = y_bd[:ch] + y_bd[ch:]
            for p in pairs:
                upd = _dot_tn(vals[p], cat([by_head(repe_r[rows, cols[p]]), by_head(ke_r[rows, cols[p]])],
                                           axis=0))
                state_ref[p] = s_old[p] * pc_r[8 * c:8 * c + 1, cols[p]] + upd

        row0 = lax.broadcasted_iota(jnp.int32, (tb, 1), 0) == 0
        env = {}

        def shift(x, cref, slot, mu):
            prev = jnp.where(row0, cref[slot:slot + 1, :], pltpu.roll(x, 1, axis=0))
            cref[slot:slot + 1, :] = x[tb - 1:tb, :]
            return x + (prev - x) * mu

        def stage_shift():
            env["r"] = shift(r_ref[0].astype(F32), carry_ref, 0, mu_ref[0:1, :])
            env["k"] = shift(k_ref[0].astype(F32), carry_ref, 1, mu_ref[1:2, :])
            v = shift(v_ref[0].astype(F32), carry_ref, 2, mu_ref[2:3, :])
            g = shift(g_ref[0].astype(F32), carry_ref, 3, mu_ref[3:4, :])
            lo = shift(lo_ref[0].astype(F32), carrylo_ref, 0, mulo_ref[...])
            env["v"] = v
            vv_w[...] = v.astype(BF16)
            sg_w[...] = _silu(g)
            x = w0_ref[...] + _dot(jnp.tanh(lo[:, :RWKV_LORA]).astype(BF16), w2_ref[...])
            y = -x
            softplus = jnp.maximum(y, 0.0) + jnp.log(1.0 + jnp.exp(-jnp.abs(y)))
            env["ld"] = -jnp.exp(-softplus - 0.5)
            env["a"] = _sigmoid(a0_ref[...] + _dot(lo[:, RWKV_LORA:].astype(BF16), a2_ref[...]))

        def stage_keys():
            r, k, v, a = env["r"], env["k"], env["v"], env["a"]
            kk = k * kk_ref[...]
            k2 = k * (1.0 + (a - 1.0) * ka_ref[...])
            sums = headsum(cat([kk * kk, r * k2 * rk_ref[...]], axis=0))
            kk = kk * lax.rsqrt(jnp.maximum(sums[:tb], 1e-24))
            bonus_w[...] = sums[tb:] * v
            env["kk"], env["k2"], env["rep"] = kk, k2, kk * a

        def stage_decay():
            ld = env["ld"]
            ri = lax.broadcasted_iota(jnp.int32, (tb, tb), 0)
            ci = lax.broadcasted_iota(jnp.int32, (tb, tb), 1)
            lincl = jnp.where(((ri // ch) == (ci // ch)) & (ci <= ri), 1.0, 0.0).astype(BF16)
            hi = ld.astype(BF16)
            r1 = ld - hi.astype(F32)
            mid = r1.astype(BF16)
            low = (r1 - mid.astype(F32)).astype(BF16)
            parts = _dot(lincl, cat([hi, mid, low], axis=1))
            env["cum"] = parts[:, :width] + parts[:, width:2 * width] + parts[:, 2 * width:]

        def stage_operands():
            r, kk, k2, rep, ld, cum = (env[n] for n in ("r", "kk", "k2", "rep", "ld", "cum"))
            pinv = jnp.exp(-cum)
            pend = []
            for s in range(nch):
                pc = jnp.exp(cum[(s + 1) * ch - 1:(s + 1) * ch, :])
                pc_w[8 * s:8 * s + 8, :] = jnp.broadcast_to(pc, (8, width))
                pend.append(pc * pinv[s * ch:(s + 1) * ch, :])
            pend = cat(pend, axis=0)
            rx_w[...] = (r * jnp.exp(cum)).astype(BF16)
            remx_w[...] = (-kk * jnp.exp(cum - ld)).astype(BF16)
            repd_w[...] = (rep * pinv).astype(BF16)
            kd_w[...] = (k2 * pinv).astype(BF16)
            repe_w[...] = (rep * pend).astype(BF16)
            ke_w[...] = (k2 * pend).astype(BF16)

        stages = [stage_shift, stage_keys, stage_decay, stage_operands]
        for c in range(nch):
            chunk(c)
            if c < len(stages):
                stages[c]()
        for stage in stages[nch:]:
            stage()

        yv = y_ref[...]
        inv_n = 1.0 / RWKV_HEAD
        mean = headsum(yv) * inv_n
        dlt = yv - mean
        var = headsum(dlt * dlt) * inv_n
        yn = dlt * lax.rsqrt(var + LNX_EPS) * lng_ref[...] + lnb_ref[...]
        o_ref[0] = ((yn + bonus_r[...]) * sg_r[...]).astype(o_ref.dtype)

    @pl.when(step_id % 2 == 0)
    def _():
        run(set_a, set_b)

    @pl.when(step_id % 2 == 1)
    def _():
        run(set_b, set_a)


def _rwkv_call(p, mu, mulo, w0, w2, a0, a2, k_k, k_a, r_k, lnx_g, lnx_b, tb, ch):
    bsz, seq, _ = p.shape
    width = w0.shape[-1]
    head_of = jnp.arange(width) // RWKV_HEAD
    hsum = (head_of[:, None] == jnp.arange(LANES)[None, :]).astype(BF16)
    hexp = jnp.concatenate([hsum.T, hsum.T], axis=0)
    npair = width // LANES
    nblk = seq // tb
    lo_blk = p.shape[-1] // LANES - 1
    act = lambda j: pl.BlockSpec((1, tb, width), lambda b, s, j=j: (b, jnp.minimum(s, nblk - 1), j))
    row = lambda a: a.reshape(1, width)
    bf = lambda: pltpu.VMEM((tb, width), BF16)
    f32 = lambda: pltpu.VMEM((tb, width), F32)
    buf_set = lambda: [bf(), bf(), bf(), bf(), bf(), bf(), bf(),
                       pltpu.VMEM((8 * (tb // ch), width), F32), f32(), f32()]
    return pl.pallas_call(
        functools.partial(_rwkv_kernel, tb=tb, ch=ch),
        grid=(bsz, nblk + 1),
        in_specs=[act(0), act(1), act(2), act(3),
                  pl.BlockSpec((1, tb, LANES), lambda b, s: (b, jnp.minimum(s, nblk - 1), lo_blk)),
                  _resident((4, width)), _resident((1, LANES)),
                  _resident((1, width)), _resident((RWKV_LORA, width)),
                  _resident((1, width)), _resident((RWKV_LORA, width)),
                  _resident((1, width)), _resident((1, width)), _resident((1, width)),
                  _resident((1, width)), _resident((1, width)),
                  _resident(hsum.shape), _resident(hexp.shape)],
        out_specs=pl.BlockSpec((1, tb, width), lambda b, s: (b, jnp.maximum(s - 1, 0), 0)),
        out_shape=jax.ShapeDtypeStruct((bsz, seq, width), BF16),
        scratch_shapes=[pltpu.VMEM((8, width), F32), pltpu.VMEM((8, LANES), F32),
                        pltpu.VMEM((npair, LANES, LANES), F32), f32()] + buf_set() + buf_set(),
        compiler_params=_params(("parallel", "arbitrary")),
        name="rwkv",
    )(p, p, p, p, p, mu, mulo, row(w0), w2, row(a0), a2, row(k_k), row(k_a), row(r_k),
      row(lnx_g), row(lnx_b), hsum, hexp)


def _s5_kernel(u_ref, gate_ref, bblk_ref, cblk_ref, dnr_ref, dni_ref, dpr_ref, dpi_ref,
               lre_ref, lim_ref, d_ref, gw_ref, gb_ref, o_ref, xr_ref, xi_ref, y_ref, *, ts, cs):
    @pl.when(pl.program_id(1) == 0)
    def _():
        xr_ref[...] = jnp.zeros_like(xr_ref)
        xi_ref[...] = jnp.zeros_like(xi_ref)

    ntile, ucols, scols2 = bblk_ref.shape
    scols = scols2 // 2
    nsub = ts // cs
    u_bf = u_ref[0]
    ri = lax.broadcasted_iota(jnp.int32, (ts, ts), 0)
    ci = lax.broadcasted_iota(jnp.int32, (ts, ts), 1)
    ltri = jnp.where(((ri // cs) == (ci // cs)) & (ci <= ri), 1.0, 0.0).astype(BF16)

    for t in range(ntile):
        sc = slice(t * scols, (t + 1) * scols)
        bu = _dot(u_bf[:, t * ucols:(t + 1) * ucols], bblk_ref[t])
        dnr, dni = dnr_ref[:, sc], dni_ref[:, sc]
        dpr, dpi = dpr_ref[:, sc], dpi_ref[:, sc]
        zr, zi = [], []
        for s in range(nsub):
            br = bu[s * cs:(s + 1) * cs, :scols]
            bi = bu[s * cs:(s + 1) * cs, scols:]
            zr.append(br * dnr - bi * dni)
            zi.append(br * dni + bi * dnr)
        z = jnp.concatenate([jnp.concatenate(zr, axis=0), jnp.concatenate(zi, axis=0)], axis=1)
        csum = _dot(ltri, z.astype(BF16))
        lre, lim = lre_ref[:, sc], lim_ref[:, sc]
        pr, pi_ = xr_ref[0:1, sc], xi_ref[0:1, sc]
        xr, xi = [], []
        for s in range(nsub):
            ar = lre * pr - lim * pi_
            ai = lre * pi_ + lim * pr
            cr = csum[s * cs:(s + 1) * cs, :scols] + ar
            cim = csum[s * cs:(s + 1) * cs, scols:] + ai
            xs_r = cr * dpr - cim * dpi
            xs_i = cr * dpi + cim * dpr
            pr, pi_ = xs_r[cs - 1:cs, :], xs_i[cs - 1:cs, :]
            xr.append(xs_r)
            xi.append(xs_i)
        xr_ref[0:1, sc] = pr
        xi_ref[0:1, sc] = pi_
        xfull = jnp.concatenate([jnp.concatenate(xr, axis=0), jnp.concatenate(xi, axis=0)], axis=1)
        y_ref[:, t * ucols:(t + 1) * ucols] = _dot(xfull.astype(BF16), cblk_ref[t])

    y = y_ref[...] + d_ref[...] * u_bf.astype(F32)
    cdf = 0.5 * (1.0 + jnp.tanh(math.sqrt(2.0 / math.pi) * (y + 0.044715 * (y * y * y))))
    y = y * cdf
    glu = _dot(y.astype(BF16), gw_ref[...]) + gb_ref[...]
    o_ref[0] = (y * _sigmoid(glu) * _silu(gate_ref[0].astype(F32))).astype(o_ref.dtype)


def _s5_tables(lam_re, lam_im, log_dt, b_re, b_im, c_re, c_im, cs):
    g, n = lam_re.shape
    pch = b_re.shape[-1]
    nt = g // S5_GT
    dt = jnp.exp(log_dt)[:, None]
    mag = jnp.exp(lam_re * dt)
    e_re, e_im = mag * jnp.cos(lam_im * dt), mag * jnp.sin(lam_im * dt)
    den = lam_re * lam_re + lam_im * lam_im
    coef_re = ((e_re - 1.0) * lam_re + e_im * lam_im) / den
    coef_im = (e_im * lam_re - (e_re - 1.0) * lam_im) / den
    bb_re = coef_re[..., None] * b_re - coef_im[..., None] * b_im
    bb_im = coef_re[..., None] * b_im + coef_im[..., None] * b_re
    eye = jnp.eye(S5_GT, dtype=F32)

    def btile(bb):
        return jnp.einsum('tgnq,gh->tgqhn', bb.reshape(nt, S5_GT, n, pch), eye).reshape(
            nt, S5_GT * pch, S5_GT * n)

    def ctile(cc):
        return jnp.einsum('tgpn,gh->tgnhp', cc.reshape(nt, S5_GT, pch, n), eye).reshape(
            nt, S5_GT * n, S5_GT * pch)

    bblk = jnp.concatenate([btile(bb_re), btile(bb_im)], axis=-1).astype(BF16)
    cblk = jnp.concatenate([ctile(c_re), -ctile(c_im)], axis=1).astype(BF16)
    j = jnp.arange(cs, dtype=F32)[:, None, None]
    lr, li = (lam_re * dt)[None], (lam_im * dt)[None]
    flat = lambda t: t.reshape(t.shape[0], g * n)
    dpr, dpi = flat(jnp.exp(j * lr) * jnp.cos(j * li)), flat(jnp.exp(j * lr) * jnp.sin(j * li))
    dnr, dni = flat(jnp.exp(-j * lr) * jnp.cos(j * li)), flat(-jnp.exp(-j * lr) * jnp.sin(j * li))
    return bblk, cblk, dnr, dni, dpr, dpi, e_re.reshape(1, g * n), e_im.reshape(1, g * n)


def _s5_call(p, tables, layer, d, glu_w, glu_b, u_blk, gate_blk, ts, cs):
    bsz, seq, _ = p.shape
    width = d.shape[-1]
    bblk, cblk, dnr, dni, dpr, dpi, lre, lim = tables
    nstate = lre.shape[-1]
    res = lambda a: _layer_resident(a, layer)
    return pl.pallas_call(
        functools.partial(_s5_kernel, ts=ts, cs=cs),
        grid=(bsz, seq // ts),
        in_specs=[pl.BlockSpec((1, ts, width), lambda b, t: (b, t, u_blk)),
                  pl.BlockSpec((1, ts, width), lambda b, t: (b, t, gate_blk)),
                  res(bblk), res(cblk), res(dnr), res(dni), res(dpr), res(dpi), res(lre), res(lim),
                  _resident((1, width)), res(glu_w), _resident((1, width))],
        out_specs=pl.BlockSpec((1, ts, width), lambda b, t: (b, t, 0)),
        out_shape=jax.ShapeDtypeStruct((bsz, seq, width), BF16),
        scratch_shapes=[pltpu.VMEM((8, nstate), F32), pltpu.VMEM((8, nstate), F32),
                        pltpu.VMEM((ts, width), F32)],
        compiler_params=_params(("parallel", "arbitrary")),
        name="s5",
    )(p, p, bblk, cblk, dnr, dni, dpr, dpi, lre, lim, d.reshape(1, width), glu_w,
      glu_b.reshape(1, width))


def _rope128(x, cos, sin):
    lane = lax.broadcasted_iota(jnp.int32, x.shape, 1)
    half = QK_ROPE // 2
    partner = jnp.where(lane < half, pltpu.roll(x, LANES - half, axis=1), pltpu.roll(x, half, axis=1))
    return x * cos + partner * sin


def _qkv_kernel(cq_ref, ckv_ref, kpe_ref, cos_ref, sin_ref, qn_ref, kvn_ref, wq_ref, wkv_ref,
                q_ref, k_ref, v_ref, *, scale):
    def latent(ref, gain_ref):
        x = ref[0].astype(F32)
        ms = jnp.mean(x * x, axis=-1, keepdims=True)
        return (x * lax.rsqrt(ms + NORM_EPS) * gain_ref[...]).astype(BF16)

    cq = latent(cq_ref, qn_ref)
    ckv = latent(ckv_ref, kvn_ref)
    cos, sin = cos_ref[0], sin_ref[0]
    kpe = _rope128(kpe_ref[0].astype(F32), cos, sin).astype(BF16)
    hw = 2 * LANES
    for h in range(q_ref.shape[1]):
        qh = _dot(cq, wq_ref[:, h * hw:(h + 1) * hw]) * scale
        q_ref[0, h] = jnp.concatenate(
            [qh[:, :LANES], _rope128(qh[:, LANES:], cos, sin)], axis=1).astype(BF16)
        kvh = _dot(ckv, wkv_ref[:, h * hw:(h + 1) * hw])
        k_ref[0, h] = jnp.concatenate([kvh[:, :LANES].astype(BF16), kpe], axis=1)
        v_ref[0, h] = kvh[:, LANES:].astype(BF16)


def _qkv_call(p, cos, sin, q_norm, kv_norm, wq, wkv, layer, cq_blk, ckv_blk, kpe_blk, tm):
    bsz, seq, _ = p.shape
    nh = MLA_HEADS
    hw = 2 * LANES
    scale = math.log2(math.e) / math.sqrt(QK_NOPE + QK_ROPE)
    head_out = lambda w: pl.BlockSpec((1, nh, tm, w), lambda b, t: (b, 0, t, 0))
    return pl.pallas_call(
        functools.partial(_qkv_kernel, scale=scale),
        grid=(bsz, seq // tm),
        in_specs=[pl.BlockSpec((1, tm, Q_LORA), lambda b, t: (b, t, cq_blk)),
                  pl.BlockSpec((1, tm, KV_LORA), lambda b, t: (b, t, ckv_blk)),
                  pl.BlockSpec((1, tm, LANES), lambda b, t: (b, t, kpe_blk)),
                  pl.BlockSpec((1, tm, LANES), lambda b, t: (b, t, 0)),
                  pl.BlockSpec((1, tm, LANES), lambda b, t: (b, t, 0)),
                  _resident((1, Q_LORA)), _resident((1, KV_LORA)),
                  _layer_resident(wq, layer), _layer_resident(wkv, layer)],
        out_specs=[head_out(hw), head_out(hw), head_out(V_HEAD)],
        out_shape=[jax.ShapeDtypeStruct((bsz, nh, seq, hw), BF16),
                   jax.ShapeDtypeStruct((bsz, nh, seq, hw), BF16),
                   jax.ShapeDtypeStruct((bsz, nh, seq, V_HEAD), BF16)],
        compiler_params=_params(("parallel", "parallel")),
        name="qkv",
    )(p, p, p, cos, sin, q_norm.reshape(1, Q_LORA), kv_norm.reshape(1, KV_LORA), wq, wkv)


def _attn_kernel(q_ref, k_ref, v_ref, o_ref, *, tq, tk):
    qi = pl.program_id(2)
    nsub = tq // tk
    vd = v_ref.shape[3]
    qs = [q_ref[0, 0, i * tk:(i + 1) * tk, :] for i in range(nsub)]
    diag = (lax.broadcasted_iota(jnp.int32, (tk, tk), 0)
            >= lax.broadcasted_iota(jnp.int32, (tk, tk), 1))

    def scores(j, kinds):
        rows = pl.ds(pl.multiple_of(j * tk, tk), tk)
        kb = k_ref[0, 0, rows, :]
        return [None if kinds[i] is None else _dot_nt(qs[i], kb) for i in range(nsub)]

    def update(j, carries, kinds, ss):
        rows = pl.ds(pl.multiple_of(j * tk, tk), tk)
        vb = jnp.concatenate([v_ref[0, 0, rows, :], jnp.ones((tk, vd), BF16)], axis=1)
        out = []
        for i in range(nsub):
            if kinds[i] is None:
                out.append(carries[i])
                continue
            m, acc = carries[i]
            s = jnp.where(diag, ss[i], -1e30) if kinds[i] else ss[i]
            m_new = jnp.maximum(m, jnp.max(s, axis=-1, keepdims=True))
            pexp = jnp.exp2(s - m_new).astype(BF16)
            acc = jnp.exp2(m - m_new) * acc + _dot(pexp, vb)
            out.append((m_new, acc))
        return tuple(out)

    def blocks(js, carries, kinds_list):
        ss = [scores(j, kinds) for j, kinds in zip(js, kinds_list)]
        for j, kinds, s in zip(js, kinds_list, ss):
            carries = update(j, carries, kinds, s)
        return carries

    init = tuple((jnp.full((tk, 1), -1e30, F32), jnp.zeros((tk, 2 * vd), F32)) for _ in range(nsub))
    nfull = qi * nsub
    visible = [[False] * nsub] * nsub
    carries = lax.fori_loop(
        0, qi, lambda jq, c: blocks([jq * nsub + d for d in range(nsub)], c, visible), init)
    tail = [[None if i < d else (i == d) for i in range(nsub)] for d in range(nsub)]
    carries = blocks([nfull + d for d in range(nsub)], carries, tail)
    for i in range(nsub):
        _, acc = carries[i]
        o_ref[0, i * tk:(i + 1) * tk, :] = (acc[:, :vd] / acc[:, vd:]).astype(o_ref.dtype)


def _attn_call(q, k, v, tq, tk):
    bsz, nh, seq, hw = q.shape
    vd = v.shape[3]
    return pl.pallas_call(
        functools.partial(_attn_kernel, tq=tq, tk=tk),
        grid=(bsz, nh, seq // tq),
        in_specs=[pl.BlockSpec((1, 1, tq, hw), lambda b, h, i: (b, h, i, 0)),
                  pl.BlockSpec((1, 1, seq, hw), lambda b, h, i: (b, h, 0, 0)),
                  pl.BlockSpec((1, 1, seq, vd), lambda b, h, i: (b, h, 0, 0))],
        out_specs=pl.BlockSpec((1, tq, vd), lambda b, h, i: (b, i, h)),
        out_shape=jax.ShapeDtypeStruct((bsz, seq, nh * vd), BF16),
        compiler_params=_params(("parallel", "parallel", "arbitrary")),
        name="attn",
    )(q, k, v)


def _rope_tables(positions):
    inv_freq = 1.0 / (ROPE_BASE ** (jnp.arange(0, QK_ROPE, 2, dtype=F32) / QK_ROPE))
    ang = positions.astype(F32)[..., None] * inv_freq
    cos, sin = jnp.cos(ang), jnp.sin(ang)
    zero = jnp.zeros_like(cos)
    return (jnp.concatenate([cos, cos, zero, zero], axis=-1),
            jnp.concatenate([-sin, sin, zero, zero], axis=-1))


def _modulation(ada, j, bsz, d):
    m = ada[j, :bsz]
    part = lambda i: m[:, i * d:(i + 1) * d].reshape(bsz, 1, d)
    return part(0), part(1), part(2)


def kernel(x, c, positions, ev_ada_w, ev_ada_b, ev_norm_pre, ev_norm_post, ev_w_in, ev_mu, ev_w0, ev_w2, ev_a0, ev_a2, ev_k_k, ev_k_a, ev_r_k, ev_lnx_g, ev_lnx_b, ev_lam_re, ev_lam_im, ev_log_dt, ev_b_re, ev_b_im, ev_c_re, ev_c_im, ev_d, ev_glu_w, ev_glu_b, ev_w_out, od_ada_w, od_ada_b, od_norm_pre, od_norm_post, od_w_in, od_q_norm, od_w_q_up, od_kv_norm, od_w_kv_up, od_w_out):
    bsz, seq, d = x.shape
    n_even, n_odd = ev_w_in.shape[0], od_w_in.shape[0]
    depth = n_even + n_odd
    rw = ev_w0.shape[-1]
    sw = ev_d.shape[-1]
    a_in = 4 * rw + 2 * RWKV_LORA
    h = x.astype(F32)

    c8 = jnp.zeros((8, d), F32).at[:bsz].set(c.astype(F32))
    ev_ada = _ada_call(c8, ev_ada_w, ev_ada_b)
    od_ada = _ada_call(c8, od_ada_w, od_ada_b)
    cos, sin = _rope_tables(positions)

    ev_w_in_p = jnp.concatenate([ev_w_in[:, :, :4 * rw], ev_w_in[:, :, a_in:],
                                 ev_w_in[:, :, 4 * rw:a_in]], axis=2).astype(BF16)
    ev_w_out_b, ev_glu_w_b = ev_w_out.astype(BF16), ev_glu_w.astype(BF16)
    s5_tables = jax.vmap(functools.partial(_s5_tables, cs=S5_CHUNK))(
        ev_lam_re, ev_lam_im, ev_log_dt, ev_b_re, ev_b_im, ev_c_re, ev_c_im)
    o2, o3 = Q_LORA + KV_LORA, Q_LORA + KV_LORA + QK_ROPE
    mw = od_w_in.shape[2] - o3
    od_w_in_p = jnp.concatenate([od_w_in[:, :, o3:], od_w_in[:, :, :o3],
                                 jnp.zeros((n_odd, d, LANES - QK_ROPE), od_w_in.dtype)],
                                axis=2).astype(BF16)
    nh = MLA_HEADS
    wq = od_w_q_up.reshape(n_odd, Q_LORA, nh, QK_NOPE + QK_ROPE)
    wq = jnp.pad(wq, ((0, 0), (0, 0), (0, 0), (0, 2 * LANES - QK_NOPE - QK_ROPE)))
    wq = wq.reshape(n_odd, Q_LORA, nh * 2 * LANES).astype(BF16)
    wkv = od_w_kv_up.astype(BF16)
    od_w_out_b = od_w_out.astype(BF16)

    tm_out = min(512, seq)
    for i in range(depth):
        j = i // 2
        if i % 2 == 0:
            shift, scale, gate = _modulation(ev_ada, j, bsz, d)
            p = _norm_proj_call(h, ev_norm_pre[j], scale, shift, ev_w_in_p, j,
                                tm=min(256, seq), tn=PROJ_TN)
            mu = ev_mu[j]
            y_a = _rwkv_call(p, mu[:4 * rw].reshape(4, rw), mu[4 * rw:].reshape(1, 2 * RWKV_LORA),
                             ev_w0[j], ev_w2[j].astype(BF16), ev_a0[j], ev_a2[j].astype(BF16),
                             ev_k_k[j], ev_k_a[j],
                             ev_r_k[j], ev_lnx_g[j], ev_lnx_b[j],
                             tb=min(RWKV_TB, seq), ch=RWKV_CHUNK)
            y_b = _s5_call(p, s5_tables, j, ev_d[j], ev_glu_w_b, ev_glu_b[j],
                           u_blk=(4 * rw) // sw, gate_blk=(4 * rw) // sw + 1,
                           ts=min(S5_TS, seq), cs=S5_CHUNK)
            half = pl.BlockSpec((1, tm_out, rw), lambda b, t_: (b, t_, 0))
            h = _out_call(_out_even_kernel, [y_a, y_b], [half, half],
                          ev_w_out_b, j, ev_norm_post[j], gate, h, tm_out)
        else:
            shift, scale, gate = _modulation(od_ada, j, bsz, d)
            p = _norm_proj_call(h, od_norm_pre[j], scale, shift, od_w_in_p, j,
                                tm=min(512, seq), tn=PROJ_TN)
            q, k, v = _qkv_call(p, cos, sin, od_q_norm[j], od_kv_norm[j], wq, wkv, j,
                                cq_blk=mw // Q_LORA, ckv_blk=mw // KV_LORA + 1,
                                kpe_blk=(mw + o2) // LANES, tm=min(512, seq))
            o = _attn_call(q, k, v, tq=min(2048, seq), tk=min(512, seq))
            full = pl.BlockSpec((1, tm_out, mw), lambda b, t_: (b, t_, 0))
            h = _out_call(_out_odd_kernel, [o, p], [full, full],
                          od_w_out_b, j, od_norm_post[j], gate, h, tm_out)
    return h.astype(x.dtype)
```

```python
import functools
import math

import jax
import jax.numpy as jnp
from jax import lax
from jax.experimental import pallas as pl
from jax.experimental.pallas import tpu as pltpu

F32 = jnp.float32
BF16 = jnp.bfloat16

NORM_EPS = 1e-6
LNX_EPS = 64e-5
ROPE_BASE = 10000.0

RWKV_HEAD = 64
RWKV_LORA = 64
S5_GROUP = 16
S5_STATE = 64
MLA_HEADS = 16
QK_NOPE = 128
QK_ROPE = 64
V_HEAD = 128
Q_LORA = 512
KV_LORA = 512

LANES = 128
V7X_VMEM_LIMIT = 56 * 1024 * 1024

RWKV_CHUNK = 64
RWKV_TB = 256
S5_CHUNK = 64
S5_TS = 256
S5_GT = 16
PROJ_TN = 1024


def _params(sem, vmem=V7X_VMEM_LIMIT):
    return pltpu.CompilerParams(dimension_semantics=sem, vmem_limit_bytes=vmem)


def _sigmoid(x):
    return 1.0 / (1.0 + jnp.exp(-x))


def _silu(x):
    return x * _sigmoid(x)


def _dot(a, b):
    return jnp.dot(a, b, preferred_element_type=F32)


def _dot_nt(a, b):
    return lax.dot_general(a, b, (((1,), (1,)), ((), ())), preferred_element_type=F32)


def _dot_tn(a, b):
    return lax.dot_general(a, b, (((0,), (0,)), ((), ())), preferred_element_type=F32)


def _resident(shape):
    nd = len(shape)
    return pl.BlockSpec(shape, lambda *_: (0,) * nd, pipeline_mode=pl.Buffered(1))


def _layer_resident(stacked, layer):
    nd = stacked.ndim
    return pl.BlockSpec((None,) + stacked.shape[1:], lambda *_: (layer,) + (0,) * (nd - 1),
                        pipeline_mode=pl.Buffered(1))


ADA_PARTS = 3
ADA_ROWS = 8


def _ada_kernel(c_ref, w_ref, b_ref, o_ref):
    s = _silu(c_ref[...]).astype(BF16)
    o_ref[0, 0, :, 0, :] = _dot(s, w_ref[0].astype(BF16)) + b_ref[0]


def _ada_call(c8, w, b):
    nl, d, n3 = w.shape
    tn = d // 2
    per = d // tn
    return pl.pallas_call(
        _ada_kernel,
        grid=(nl, n3 // tn),
        in_specs=[pl.BlockSpec((ADA_ROWS, d), lambda l, n: (0, 0)),
                  pl.BlockSpec((1, d, tn), lambda l, n: (l, 0, n)),
                  pl.BlockSpec((1, 1, tn), lambda l, n: (l, 0, n))],
        out_specs=pl.BlockSpec((1, 1, ADA_ROWS, 1, tn), lambda l, n: (l, n // per, 0, 0, n % per)),
        out_shape=jax.ShapeDtypeStruct((nl, ADA_PARTS, ADA_ROWS, 1, d), F32),
        compiler_params=_params(("parallel", "parallel")),
        name="ada",
    )(c8, w, b.reshape(nl, 1, n3))


def _ada_row(ada, layer, part):
    d = ada.shape[-1]
    return pl.BlockSpec((None, None, None, 1, d), lambda b, *_: (layer, part, b, 0, 0))


def _norm_proj_kernel(h_ref, g_ref, sc_ref, sh_ref, w_ref, o_ref, *, tn):
    x = h_ref[0]
    ms = jnp.mean(x * x, axis=-1, keepdims=True)
    z = (x * lax.rsqrt(ms + NORM_EPS) * g_ref[...]) * (1.0 + sc_ref[...]) + sh_ref[...]
    z = z.astype(BF16)
    n = w_ref.shape[1]
    for n0 in range(0, n, tn):
        n1 = min(n0 + tn, n)
        o_ref[0, :, n0:n1] = _dot(z, w_ref[:, n0:n1]).astype(o_ref.dtype)


def _norm_proj_call(h, g, ada, w, layer, tm, tn):
    bsz, seq, d = h.shape
    n = w.shape[2]
    return pl.pallas_call(
        functools.partial(_norm_proj_kernel, tn=tn),
        grid=(bsz, seq // tm),
        in_specs=[pl.BlockSpec((1, tm, d), lambda b, t: (b, t, 0)),
                  _layer_resident(g, layer),
                  _ada_row(ada, layer, 1),
                  _ada_row(ada, layer, 0),
                  _layer_resident(w, layer)],
        out_specs=pl.BlockSpec((1, tm, n), lambda b, t: (b, t, 0)),
        out_shape=jax.ShapeDtypeStruct((bsz, seq, n), BF16),
        compiler_params=_params(("parallel", "parallel")),
        name="norm_proj",
    )(h, g, ada, ada, w)


def _finish(y, post_ref, gate_ref, h_ref, o_ref):
    ms = jnp.mean(y * y, axis=-1, keepdims=True)
    yn = y * lax.rsqrt(ms + NORM_EPS) * post_ref[...]
    o_ref[0] = h_ref[0] + gate_ref[...] * yn


def _out_even_kernel(ya_ref, yb_ref, w_ref, post_ref, gate_ref, h_ref, o_ref):
    half = ya_ref.shape[2]
    y = _dot(ya_ref[0], w_ref[:half, :]) + _dot(yb_ref[0], w_ref[half:, :])
    _finish(y, post_ref, gate_ref, h_ref, o_ref)


def _out_odd_kernel(o_in_ref, g_ref, w_ref, post_ref, gate_ref, h_ref, o_ref):
    yin = o_in_ref[0].astype(F32) * _silu(g_ref[0].astype(F32))
    y = _dot(yin.astype(BF16), w_ref[...])
    _finish(y, post_ref, gate_ref, h_ref, o_ref)


def _out_call(kernel, acts, act_specs, w, layer, post, ada, h, tm):
    bsz, seq, d = h.shape
    return pl.pallas_call(
        kernel,
        grid=(bsz, seq // tm),
        in_specs=act_specs + [
            _layer_resident(w, layer),
            _layer_resident(post, layer),
            _ada_row(ada, layer, 2),
            pl.BlockSpec((1, tm, d), lambda b, t: (b, t, 0))],
        out_specs=pl.BlockSpec((1, tm, d), lambda b, t: (b, t, 0)),
        out_shape=jax.ShapeDtypeStruct((bsz, seq, d), F32),
        compiler_params=_params(("parallel", "parallel")),
        name=kernel.__name__.strip("_"),
    )(*acts, w, post, ada, h)


def _rwkv_kernel(r_ref, k_ref, v_ref, g_ref, lo_ref,
                 mu_ref, mulo_ref, w0_ref, w2_ref, a0_ref, a2_ref, kk_ref, ka_ref, rk_ref,
                 lng_ref, lnb_ref, hsum_ref, hexp_ref,
                 o_ref,
                 carry_ref, carrylo_ref, state_ref, y_ref, *bufs, tb, ch):
    width = r_ref.shape[2]
    npair = width // LANES
    nch = tb // ch
    c2 = 2 * ch
    nsq = int(math.log2(ch)) - 1
    step_id = pl.program_id(1)
    set_a, set_b = bufs[:len(bufs) // 2], bufs[len(bufs) // 2:]

    @pl.when(step_id == 0)
    def _():
        carry_ref[...] = jnp.zeros_like(carry_ref)
        carrylo_ref[...] = jnp.zeros_like(carrylo_ref)
        state_ref[...] = jnp.zeros_like(state_ref)

    def run(wr, rd):
        rx_w, remx_w, repd_w, kd_w, repe_w, ke_w, vv_w, pc_w, bonus_w, sg_w = wr
        rx_r, remx_r, repd_r, kd_r, repe_r, ke_r, vv_r, pc_r, bonus_r, sg_r = rd or wr
        cat = jnp.concatenate

        def headsum(t):
            sums = _dot(t.astype(BF16), hsum_ref[...])
            hi = sums.astype(BF16)
            lo = (sums - hi.astype(F32)).astype(BF16)
            return _dot(cat([hi, lo], axis=1), hexp_ref[...])

        pi = lax.broadcasted_iota(jnp.int32, (c2, c2), 0)
        pj = lax.broadcasted_iota(jnp.int32, (c2, c2), 1)
        blk = (pi // ch) == (pj // ch)
        strict = blk & (pj < pi)
        incl = blk & (pj <= pi)
        m0 = lax.broadcasted_iota(jnp.int32, (ch, LANES), 1) < RWKV_HEAD

        def by_head(t):
            z = jnp.zeros_like(t)
            return cat([jnp.where(m0, t, z), jnp.where(m0, z, t)], axis=0)

        def chunk(c):
            rows = slice(c * ch, (c + 1) * ch)
            pairs = range(npair)
            cols = [slice(p * LANES, (p + 1) * LANES) for p in pairs]
            lhs = [cat([by_head(remx_r[rows, cols[p]]), by_head(rx_r[rows, cols[p]])], axis=0)
                   for p in pairs]
            m1 = [_dot_nt(lhs[p], cat([repd_r[rows, cols[p]]] * 2 + [kd_r[rows, cols[p]]] * 2, axis=0))
                  for p in pairs]
            s_old = [state_ref[p] for p in pairs]
            gs = [_dot_nt(lhs[p], s_old[p].astype(BF16)) for p in pairs]
            v_bd = [by_head(vv_r[rows, cols[p]]) for p in pairs]
            sa = [gs[p][:c2] + _dot(jnp.where(strict, m1[p][:c2, c2:], 0.0).astype(BF16), v_bd[p])
                  for p in pairs]
            pw = [jnp.where(strict, m1[p][:c2, :c2], 0.0) for p in pairs]
            for _i in range(nsq):
                both = [_dot(pw[p].astype(BF16), cat([pw[p], sa[p]], axis=1).astype(BF16)) for p in pairs]
                pw = [both[p][:, :c2] for p in pairs]
                sa = [sa[p] + both[p][:, c2:] for p in pairs]
            sa = [(sa[p] + _dot(pw[p].astype(BF16), sa[p].astype(BF16))).astype(BF16) for p in pairs]
            vals = [cat([sa[p], v_bd[p]], axis=0) for p in pairs]
            for p in pairs:
                arbr = cat([jnp.where(incl, m1[p][c2:, :c2], 0.0), jnp.where(incl, m1[p][c2:, c2:], 0.0)],
                           axis=1).astype(BF16)
                y_bd = gs[p][c2:] + _dot(arbr, vals[p])
                y_ref[rows, cols[p]] = y_bd[:ch] + y_bd[ch:]
            for p in pairs:
                upd = _dot_tn(vals[p], cat([by_head(repe_r[rows, cols[p]]), by_head(ke_r[rows, cols[p]])],
                                           axis=0))
                state_ref[p] = s_old[p] * pc_r[8 * c:8 * c + 1, cols[p]] + upd

        row0 = lax.broadcasted_iota(jnp.int32, (tb, 1), 0) == 0
        env = {}

        def shift(x, cref, slot, mu):
            prev = jnp.where(row0, cref[slot:slot + 1, :], pltpu.roll(x, 1, axis=0))
            cref[slot:slot + 1, :] = x[tb - 1:tb, :]
            return x + (prev - x) * mu

        def stage_shift():
            env["r"] = shift(r_ref[0].astype(F32), carry_ref, 0, mu_ref[0:1, :])
            env["k"] = shift(k_ref[0].astype(F32), carry_ref, 1, mu_ref[1:2, :])
            v = shift(v_ref[0].astype(F32), carry_ref, 2, mu_ref[2:3, :])
            g = shift(g_ref[0].astype(F32), carry_ref, 3, mu_ref[3:4, :])
            lo = shift(lo_ref[0].astype(F32), carrylo_ref, 0, mulo_ref[...])
            env["v"] = v
            vv_w[...] = v.astype(BF16)
            sg_w[...] = _silu(g)
            x = w0_ref[...] + _dot(jnp.tanh(lo[:, :RWKV_LORA]).astype(BF16), w2_ref[...])
            y = -x
            softplus = jnp.maximum(y, 0.0) + jnp.log(1.0 + jnp.exp(-jnp.abs(y)))
            env["ld"] = -jnp.exp(-softplus - 0.5)
            env["a"] = _sigmoid(a0_ref[...] + _dot(lo[:, RWKV_LORA:].astype(BF16), a2_ref[...]))

        def stage_keys():
            r, k, v, a = env["r"], env["k"], env["v"], env["a"]
            kk = k * kk_ref[...]
            k2 = k * (1.0 + (a - 1.0) * ka_ref[...])
            sums = headsum(cat([kk * kk, r * k2 * rk_ref[...]], axis=0))
            kk = kk * lax.rsqrt(jnp.maximum(sums[:tb], 1e-24))
            bonus_w[...] = sums[tb:] * v
            env["kk"], env["k2"], env["rep"] = kk, k2, kk * a

        def stage_decay():
            ld = env["ld"]
            ri = lax.broadcasted_iota(jnp.int32, (tb, tb), 0)
            ci = lax.broadcasted_iota(jnp.int32, (tb, tb), 1)
            lincl = jnp.where(((ri // ch) == (ci // ch)) & (ci <= ri), 1.0, 0.0).astype(BF16)
            hi = ld.astype(BF16)
            r1 = ld - hi.astype(F32)
            mid = r1.astype(BF16)
            low = (r1 - mid.astype(F32)).astype(BF16)
            parts = _dot(lincl, cat([hi, mid, low], axis=1))
            env["cum"] = parts[:, :width] + parts[:, width:2 * width] + parts[:, 2 * width:]

        def stage_operands():
            r, kk, k2, rep, ld, cum = (env[n] for n in ("r", "kk", "k2", "rep", "ld", "cum"))
            pinv = jnp.exp(-cum)
            pend = []
            for s in range(nch):
                pc = jnp.exp(cum[(s + 1) * ch - 1:(s + 1) * ch, :])
                pc_w[8 * s:8 * s + 8, :] = jnp.broadcast_to(pc, (8, width))
                pend.append(pc * pinv[s * ch:(s + 1) * ch, :])
            pend = cat(pend, axis=0)
            rx_w[...] = (r * jnp.exp(cum)).astype(BF16)
            remx_w[...] = (-kk * jnp.exp(cum - ld)).astype(BF16)
            repd_w[...] = (rep * pinv).astype(BF16)
            kd_w[...] = (k2 * pinv).astype(BF16)
            repe_w[...] = (rep * pend).astype(BF16)
            ke_w[...] = (k2 * pend).astype(BF16)

        stages = [stage_shift, stage_keys, stage_decay, stage_operands]
        if rd is None:
            for stage in stages:
                stage()
            return
        for c in range(nch):
            chunk(c)
            if c < len(stages):
                stages[c]()
        for stage in stages[nch:]:
            stage()

        yv = y_ref[...]
        inv_n = 1.0 / RWKV_HEAD
        mean = headsum(yv) * inv_n
        dlt = yv - mean
        var = headsum(dlt * dlt) * inv_n
        yn = dlt * lax.rsqrt(var + LNX_EPS) * lng_ref[...] + lnb_ref[...]
        o_ref[0] = ((yn + bonus_r[...]) * sg_r[...]).astype(o_ref.dtype)

    @pl.when(step_id == 0)
    def _():
        run(set_a, None)

    @pl.when((step_id > 0) & (step_id % 2 == 0))
    def _():
        run(set_a, set_b)

    @pl.when(step_id % 2 == 1)
    def _():
        run(set_b, set_a)


def _rwkv_call(p, params, layer, tb, ch):
    bsz, seq, _ = p.shape
    width = params[2].shape[-1]
    head_of = jnp.arange(width) // RWKV_HEAD
    hsum = (head_of[:, None] == jnp.arange(LANES)[None, :]).astype(BF16)
    hexp = jnp.concatenate([hsum.T, hsum.T], axis=0)
    npair = width // LANES
    nblk = seq // tb
    lo_blk = p.shape[-1] // LANES - 1
    act = lambda j: pl.BlockSpec((1, tb, width), lambda b, s, j=j: (b, jnp.minimum(s, nblk - 1), j))
    bf = lambda: pltpu.VMEM((tb, width), BF16)
    f32 = lambda: pltpu.VMEM((tb, width), F32)
    buf_set = lambda: [bf(), bf(), bf(), bf(), bf(), bf(), bf(),
                       pltpu.VMEM((8 * (tb // ch), width), F32), f32(), f32()]
    return pl.pallas_call(
        functools.partial(_rwkv_kernel, tb=tb, ch=ch),
        grid=(bsz, nblk + 1),
        in_specs=[act(0), act(1), act(2), act(3),
                  pl.BlockSpec((1, tb, LANES), lambda b, s: (b, jnp.minimum(s, nblk - 1), lo_blk))]
                 + [_layer_resident(a, layer) for a in params]
                 + [_resident(hsum.shape), _resident(hexp.shape)],
        out_specs=pl.BlockSpec((1, tb, width), lambda b, s: (b, jnp.maximum(s - 1, 0), 0)),
        out_shape=jax.ShapeDtypeStruct((bsz, seq, width), BF16),
        scratch_shapes=[pltpu.VMEM((8, width), F32), pltpu.VMEM((8, LANES), F32),
                        pltpu.VMEM((npair, LANES, LANES), F32), f32()] + buf_set() + buf_set(),
        compiler_params=_params(("parallel", "arbitrary")),
        name="rwkv",
    )(p, p, p, p, p, *params, hsum, hexp)


def _s5_kernel(u_ref, gate_ref, bblk_ref, cblk_ref, dnr_ref, dni_ref, dpr_ref, dpi_ref,
               lre_ref, lim_ref, d_ref, gw_ref, gb_ref, o_ref, xr_ref, xi_ref, y_ref, *, ts, cs):
    @pl.when(pl.program_id(1) == 0)
    def _():
        xr_ref[...] = jnp.zeros_like(xr_ref)
        xi_ref[...] = jnp.zeros_like(xi_ref)

    ntile, ucols, scols2 = bblk_ref.shape
    scols = scols2 // 2
    nsub = ts // cs
    u_bf = u_ref[0]
    ri = lax.broadcasted_iota(jnp.int32, (ts, ts), 0)
    ci = lax.broadcasted_iota(jnp.int32, (ts, ts), 1)
    ltri = jnp.where(((ri // cs) == (ci // cs)) & (ci <= ri), 1.0, 0.0).astype(BF16)

    for t in range(ntile):
        sc = slice(t * scols, (t + 1) * scols)
        bu = _dot(u_bf[:, t * ucols:(t + 1) * ucols], bblk_ref[t])
        dnr, dni = dnr_ref[:, sc], dni_ref[:, sc]
        dpr, dpi = dpr_ref[:, sc], dpi_ref[:, sc]
        zr, zi = [], []
        for s in range(nsub):
            br = bu[s * cs:(s + 1) * cs, :scols]
            bi = bu[s * cs:(s + 1) * cs, scols:]
            zr.append(br * dnr - bi * dni)
            zi.append(br * dni + bi * dnr)
        z = jnp.concatenate([jnp.concatenate(zr, axis=0), jnp.concatenate(zi, axis=0)], axis=1)
        csum = _dot(ltri, z.astype(BF16))
        lre, lim = lre_ref[:, sc], lim_ref[:, sc]
        pr, pi_ = xr_ref[0:1, sc], xi_ref[0:1, sc]
        xr, xi = [], []
        for s in range(nsub):
            ar = lre * pr - lim * pi_
            ai = lre * pi_ + lim * pr
            cr = csum[s * cs:(s + 1) * cs, :scols] + ar
            cim = csum[s * cs:(s + 1) * cs, scols:] + ai
            xs_r = cr * dpr - cim * dpi
            xs_i = cr * dpi + cim * dpr
            pr, pi_ = xs_r[cs - 1:cs, :], xs_i[cs - 1:cs, :]
            xr.append(xs_r)
            xi.append(xs_i)
        xr_ref[0:1, sc] = pr
        xi_ref[0:1, sc] = pi_
        xfull = jnp.concatenate([jnp.concatenate(xr, axis=0), jnp.concatenate(xi, axis=0)], axis=1)
        y_ref[:, t * ucols:(t + 1) * ucols] = _dot(xfull.astype(BF16), cblk_ref[t])

    y = y_ref[...] + d_ref[...] * u_bf.astype(F32)
    cdf = 0.5 * (1.0 + jnp.tanh(math.sqrt(2.0 / math.pi) * (y + 0.044715 * (y * y * y))))
    y = y * cdf
    glu = _dot(y.astype(BF16), gw_ref[...]) + gb_ref[...]
    o_ref[0] = (y * _sigmoid(glu) * _silu(gate_ref[0].astype(F32))).astype(o_ref.dtype)


def _s5_tables(lam_re, lam_im, log_dt, b_re, b_im, c_re, c_im, cs):
    g, n = lam_re.shape
    pch = b_re.shape[-1]
    nt = g // S5_GT
    dt = jnp.exp(log_dt)[:, None]
    mag = jnp.exp(lam_re * dt)
    e_re, e_im = mag * jnp.cos(lam_im * dt), mag * jnp.sin(lam_im * dt)
    den = lam_re * lam_re + lam_im * lam_im
    coef_re = ((e_re - 1.0) * lam_re + e_im * lam_im) / den
    coef_im = (e_im * lam_re - (e_re - 1.0) * lam_im) / den
    bb_re = coef_re[..., None] * b_re - coef_im[..., None] * b_im
    bb_im = coef_re[..., None] * b_im + coef_im[..., None] * b_re
    eye = jnp.eye(S5_GT, dtype=F32)

    def btile(bb):
        return jnp.einsum('tgnq,gh->tgqhn', bb.reshape(nt, S5_GT, n, pch), eye).reshape(
            nt, S5_GT * pch, S5_GT * n)

    def ctile(cc):
        return jnp.einsum('tgpn,gh->tgnhp', cc.reshape(nt, S5_GT, pch, n), eye).reshape(
            nt, S5_GT * n, S5_GT * pch)

    bblk = jnp.concatenate([btile(bb_re), btile(bb_im)], axis=-1).astype(BF16)
    cblk = jnp.concatenate([ctile(c_re), -ctile(c_im)], axis=1).astype(BF16)
    j = jnp.arange(cs, dtype=F32)[:, None, None]
    lr, li = (lam_re * dt)[None], (lam_im * dt)[None]
    flat = lambda t: t.reshape(t.shape[0], g * n)
    dpr, dpi = flat(jnp.exp(j * lr) * jnp.cos(j * li)), flat(jnp.exp(j * lr) * jnp.sin(j * li))
    dnr, dni = flat(jnp.exp(-j * lr) * jnp.cos(j * li)), flat(-jnp.exp(-j * lr) * jnp.sin(j * li))
    return bblk, cblk, dnr, dni, dpr, dpi, e_re.reshape(1, g * n), e_im.reshape(1, g * n)


def _s5_call(p, tables, layer, d, glu_w, glu_b, u_blk, gate_blk, ts, cs):
    bsz, seq, _ = p.shape
    width = d.shape[-1]
    bblk, cblk, dnr, dni, dpr, dpi, lre, lim = tables
    nstate = lre.shape[-1]
    res = lambda a: _layer_resident(a, layer)
    return pl.pallas_call(
        functools.partial(_s5_kernel, ts=ts, cs=cs),
        grid=(bsz, seq // ts),
        in_specs=[pl.BlockSpec((1, ts, width), lambda b, t: (b, t, u_blk)),
                  pl.BlockSpec((1, ts, width), lambda b, t: (b, t, gate_blk)),
                  res(bblk), res(cblk), res(dnr), res(dni), res(dpr), res(dpi), res(lre), res(lim),
                  res(d), res(glu_w), res(glu_b)],
        out_specs=pl.BlockSpec((1, ts, width), lambda b, t: (b, t, 0)),
        out_shape=jax.ShapeDtypeStruct((bsz, seq, width), BF16),
        scratch_shapes=[pltpu.VMEM((8, nstate), F32), pltpu.VMEM((8, nstate), F32),
                        pltpu.VMEM((ts, width), F32)],
        compiler_params=_params(("parallel", "arbitrary")),
        name="s5",
    )(p, p, bblk, cblk, dnr, dni, dpr, dpi, lre, lim, d, glu_w, glu_b)


def _rope128(x, cos, sin):
    lane = lax.broadcasted_iota(jnp.int32, x.shape, 1)
    half = QK_ROPE // 2
    partner = jnp.where(lane < half, pltpu.roll(x, LANES - half, axis=1), pltpu.roll(x, half, axis=1))
    return x * cos + partner * sin


def _qkv_kernel(cq_ref, ckv_ref, kpe_ref, cos_ref, sin_ref, qn_ref, kvn_ref, wq_ref, wkv_ref,
                q_ref, k_ref, v_ref, *, scale):
    def latent(ref, gain_ref):
        x = ref[0].astype(F32)
        ms = jnp.mean(x * x, axis=-1, keepdims=True)
        return (x * lax.rsqrt(ms + NORM_EPS) * gain_ref[...]).astype(BF16)

    cq = latent(cq_ref, qn_ref)
    ckv = latent(ckv_ref, kvn_ref)
    cos, sin = cos_ref[0], sin_ref[0]
    kpe = _rope128(kpe_ref[0].astype(F32), cos, sin).astype(BF16)
    hw = 2 * LANES
    for h in range(q_ref.shape[1]):
        qh = _dot(cq, wq_ref[:, h * hw:(h + 1) * hw]) * scale
        q_ref[0, h] = jnp.concatenate(
            [qh[:, :LANES], _rope128(qh[:, LANES:], cos, sin)], axis=1).astype(BF16)
        kvh = _dot(ckv, wkv_ref[:, h * hw:(h + 1) * hw])
        k_ref[0, h] = jnp.concatenate([kvh[:, :LANES].astype(BF16), kpe], axis=1)
        v_ref[0, h] = kvh[:, LANES:].astype(BF16)


def _qkv_call(p, cos, sin, q_norm, kv_norm, wq, wkv, layer, cq_blk, ckv_blk, kpe_blk, tm):
    bsz, seq, _ = p.shape
    nh = MLA_HEADS
    hw = 2 * LANES
    scale = math.log2(math.e) / math.sqrt(QK_NOPE + QK_ROPE)
    head_out = lambda w: pl.BlockSpec((1, nh, tm, w), lambda b, t: (b, 0, t, 0))
    return pl.pallas_call(
        functools.partial(_qkv_kernel, scale=scale),
        grid=(bsz, seq // tm),
        in_specs=[pl.BlockSpec((1, tm, Q_LORA), lambda b, t: (b, t, cq_blk)),
                  pl.BlockSpec((1, tm, KV_LORA), lambda b, t: (b, t, ckv_blk)),
                  pl.BlockSpec((1, tm, LANES), lambda b, t: (b, t, kpe_blk)),
                  pl.BlockSpec((1, tm, LANES), lambda b, t: (b, t, 0)),
                  pl.BlockSpec((1, tm, LANES), lambda b, t: (b, t, 0)),
                  _layer_resident(q_norm, layer), _layer_resident(kv_norm, layer),
                  _layer_resident(wq, layer), _layer_resident(wkv, layer)],
        out_specs=[head_out(hw), head_out(hw), head_out(V_HEAD)],
        out_shape=[jax.ShapeDtypeStruct((bsz, nh, seq, hw), BF16),
                   jax.ShapeDtypeStruct((bsz, nh, seq, hw), BF16),
                   jax.ShapeDtypeStruct((bsz, nh, seq, V_HEAD), BF16)],
        compiler_params=_params(("parallel", "parallel")),
        name="qkv",
    )(p, p, p, cos, sin, q_norm, kv_norm, wq, wkv)


def _attn_kernel(q_ref, k_ref, v_ref, o_ref, *, tq, tk):
    qi = pl.program_id(2)
    nsub = tq // tk
    vd = v_ref.shape[3]
    qs = [q_ref[0, 0, i * tk:(i + 1) * tk, :] for i in range(nsub)]
    diag = (lax.broadcasted_iota(jnp.int32, (tk, tk), 0)
            >= lax.broadcasted_iota(jnp.int32, (tk, tk), 1))

    def scores(j, kinds):
        rows = pl.ds(pl.multiple_of(j * tk, tk), tk)
        kb = k_ref[0, 0, rows, :]
        return [None if kinds[i] is None else _dot_nt(qs[i], kb) for i in range(nsub)]

    def update(j, carries, kinds, ss):
        rows = pl.ds(pl.multiple_of(j * tk, tk), tk)
        vb = jnp.concatenate([v_ref[0, 0, rows, :], jnp.ones((tk, vd), BF16)], axis=1)
        out = []
        for i in range(nsub):
            if kinds[i] is None:
                out.append(carries[i])
                continue
            m, acc = carries[i]
            s = jnp.where(diag, ss[i], -1e30) if kinds[i] else ss[i]
            m_new = jnp.maximum(m, jnp.max(s, axis=-1, keepdims=True))
            pexp = jnp.exp2(s - m_new).astype(BF16)
            acc = jnp.exp2(m - m_new) * acc + _dot(pexp, vb)
            out.append((m_new, acc))
        return tuple(out)

    def blocks(js, carries, kinds_list):
        ss = [scores(j, kinds) for j, kinds in zip(js, kinds_list)]
        for j, kinds, s in zip(js, kinds_list, ss):
            carries = update(j, carries, kinds, s)
        return carries

    init = tuple((jnp.full((tk, 1), -1e30, F32), jnp.zeros((tk, 2 * vd), F32)) for _ in range(nsub))
    nfull = qi * nsub
    visible = [[False] * nsub] * nsub
    carries = lax.fori_loop(
        0, qi, lambda jq, c: blocks([jq * nsub + d for d in range(nsub)], c, visible), init)
    tail = [[None if i < d else (i == d) for i in range(nsub)] for d in range(nsub)]
    carries = blocks([nfull + d for d in range(nsub)], carries, tail)
    for i in range(nsub):
        _, acc = carries[i]
        o_ref[0, i * tk:(i + 1) * tk, :] = (acc[:, :vd] / acc[:, vd:]).astype(o_ref.dtype)


def _attn_call(q, k, v, tq, tk):
    bsz, nh, seq, hw = q.shape
    vd = v.shape[3]
    return pl.pallas_call(
        functools.partial(_attn_kernel, tq=tq, tk=tk),
        grid=(bsz, nh, seq // tq),
        in_specs=[pl.BlockSpec((1, 1, tq, hw), lambda b, h, i: (b, h, i, 0)),
                  pl.BlockSpec((1, 1, seq, hw), lambda b, h, i: (b, h, 0, 0)),
                  pl.BlockSpec((1, 1, seq, vd), lambda b, h, i: (b, h, 0, 0))],
        out_specs=pl.BlockSpec((1, tq, vd), lambda b, h, i: (b, i, h)),
        out_shape=jax.ShapeDtypeStruct((bsz, seq, nh * vd), BF16),
        compiler_params=_params(("parallel", "parallel", "arbitrary")),
        name="attn",
    )(q, k, v)


def _rope_tables(positions):
    inv_freq = 1.0 / (ROPE_BASE ** (jnp.arange(0, QK_ROPE, 2, dtype=F32) / QK_ROPE))
    ang = positions.astype(F32)[..., None] * inv_freq
    cos, sin = jnp.cos(ang), jnp.sin(ang)
    zero = jnp.zeros_like(cos)
    return (jnp.concatenate([cos, cos, zero, zero], axis=-1),
            jnp.concatenate([-sin, sin, zero, zero], axis=-1))


def kernel(x, c, positions, ev_ada_w, ev_ada_b, ev_norm_pre, ev_norm_post, ev_w_in, ev_mu, ev_w0, ev_w2, ev_a0, ev_a2, ev_k_k, ev_k_a, ev_r_k, ev_lnx_g, ev_lnx_b, ev_lam_re, ev_lam_im, ev_log_dt, ev_b_re, ev_b_im, ev_c_re, ev_c_im, ev_d, ev_glu_w, ev_glu_b, ev_w_out, od_ada_w, od_ada_b, od_norm_pre, od_norm_post, od_w_in, od_q_norm, od_w_q_up, od_kv_norm, od_w_kv_up, od_w_out):
    bsz, seq, d = x.shape
    n_even, n_odd = ev_w_in.shape[0], od_w_in.shape[0]
    depth = n_even + n_odd
    rw = ev_w0.shape[-1]
    sw = ev_d.shape[-1]
    a_in = 4 * rw + 2 * RWKV_LORA
    h = x.astype(F32)

    c8 = jnp.zeros((ADA_ROWS, d), F32).at[:bsz].set(c.astype(F32))
    ev_ada = _ada_call(c8, ev_ada_w, ev_ada_b)
    od_ada = _ada_call(c8, od_ada_w, od_ada_b)
    cos, sin = _rope_tables(positions)

    ev_w_in_p = jnp.concatenate([ev_w_in[:, :, :4 * rw], ev_w_in[:, :, a_in:],
                                 ev_w_in[:, :, 4 * rw:a_in]], axis=2).astype(BF16)
    ev_w_out_b, ev_glu_w_b = ev_w_out.astype(BF16), ev_glu_w.astype(BF16)
    s5_tables = jax.vmap(functools.partial(_s5_tables, cs=S5_CHUNK))(
        ev_lam_re, ev_lam_im, ev_log_dt, ev_b_re, ev_b_im, ev_c_re, ev_c_im)
    o2, o3 = Q_LORA + KV_LORA, Q_LORA + KV_LORA + QK_ROPE
    mw = od_w_in.shape[2] - o3
    od_w_in_p = jnp.concatenate([od_w_in[:, :, o3:], od_w_in[:, :, :o3],
                                 jnp.zeros((n_odd, d, LANES - QK_ROPE), od_w_in.dtype)],
                                axis=2).astype(BF16)
    nh = MLA_HEADS
    wq = od_w_q_up.reshape(n_odd, Q_LORA, nh, QK_NOPE + QK_ROPE)
    wq = jnp.pad(wq, ((0, 0), (0, 0), (0, 0), (0, 2 * LANES - QK_NOPE - QK_ROPE)))
    wq = wq.reshape(n_odd, Q_LORA, nh * 2 * LANES).astype(BF16)
    wkv = od_w_kv_up.astype(BF16)
    od_w_out_b = od_w_out.astype(BF16)

    rows = lambda a: a.reshape(a.shape[0], 1, a.shape[-1])
    rwkv_params = [ev_mu[:, :4 * rw].reshape(n_even, 4, rw), rows(ev_mu[:, 4 * rw:]),
                   rows(ev_w0), ev_w2.astype(BF16), rows(ev_a0), ev_a2.astype(BF16),
                   rows(ev_k_k), rows(ev_k_a), rows(ev_r_k), rows(ev_lnx_g), rows(ev_lnx_b)]
    ev_pre, ev_post, od_pre, od_post = (rows(a) for a in (ev_norm_pre, ev_norm_post,
                                                          od_norm_pre, od_norm_post))
    ev_d_r, ev_glu_b_r, od_qn, od_kvn = (rows(a) for a in (ev_d, ev_glu_b, od_q_norm, od_kv_norm))

    tm_out = min(512, seq)
    for i in range(depth):
        j = i // 2
        if i % 2 == 0:
            p = _norm_proj_call(h, ev_pre, ev_ada, ev_w_in_p, j, tm=min(256, seq), tn=PROJ_TN)
            y_a = _rwkv_call(p, rwkv_params, j, tb=min(RWKV_TB, seq), ch=RWKV_CHUNK)
            y_b = _s5_call(p, s5_tables, j, ev_d_r, ev_glu_w_b, ev_glu_b_r,
                           u_blk=(4 * rw) // sw, gate_blk=(4 * rw) // sw + 1,
                           ts=min(S5_TS, seq), cs=S5_CHUNK)
            half = pl.BlockSpec((1, tm_out, rw), lambda b, t_: (b, t_, 0))
            h = _out_call(_out_even_kernel, [y_a, y_b], [half, half],
                          ev_w_out_b, j, ev_post, ev_ada, h, tm_out)
        else:
            p = _norm_proj_call(h, od_pre, od_ada, od_w_in_p, j, tm=min(512, seq), tn=PROJ_TN)
            q, k, v = _qkv_call(p, cos, sin, od_qn, od_kvn, wq, wkv, j,
                                cq_blk=mw // Q_LORA, ckv_blk=mw // KV_LORA + 1,
                                kpe_blk=(mw + o2) // LANES, tm=min(512, seq))
            o = _attn_call(q, k, v, tq=min(2048, seq), tk=min(512, seq))
            full = pl.BlockSpec((1, tm_out, mw), lambda b, t_: (b, t_, 0))
            h = _out_call(_out_odd_kernel, [o, p], [full, full],
                          od_w_out_b, j, od_post, od_ada, h, tm_out)
    return h.astype(x.dtype)
```

```python
import functools
import math

import jax
import jax.numpy as jnp
from jax import lax
from jax.experimental import pallas as pl
from jax.experimental.pallas import tpu as pltpu

F32 = jnp.float32
BF16 = jnp.bfloat16

NORM_EPS = 1e-6
LNX_EPS = 64e-5
ROPE_BASE = 10000.0

RWKV_HEAD = 64
RWKV_LORA = 64
S5_GROUP = 16
S5_STATE = 64
MLA_HEADS = 16
QK_NOPE = 128
QK_ROPE = 64
V_HEAD = 128
Q_LORA = 512
KV_LORA = 512

LANES = 128
V7X_VMEM_LIMIT = 56 * 1024 * 1024

RWKV_CHUNK = 64
RWKV_TB = 256
S5_SUB = 16
S5_TS = 512
PROJ_TN = 1024


def _params(sem, vmem=V7X_VMEM_LIMIT):
    return pltpu.CompilerParams(dimension_semantics=sem, vmem_limit_bytes=vmem)


def _sigmoid(x):
    return 1.0 / (1.0 + jnp.exp(-x))


def _silu(x):
    return x * _sigmoid(x)


def _dot(a, b):
    return jnp.dot(a, b, preferred_element_type=F32)


def _dot_nt(a, b):
    return lax.dot_general(a, b, (((1,), (1,)), ((), ())), preferred_element_type=F32)


def _dot_tn(a, b):
    return lax.dot_general(a, b, (((0,), (0,)), ((), ())), preferred_element_type=F32)


def _resident(shape):
    nd = len(shape)
    return pl.BlockSpec(shape, lambda *_: (0,) * nd, pipeline_mode=pl.Buffered(1))


def _layer_resident(stacked, layer):
    nd = stacked.ndim
    return pl.BlockSpec((None,) + stacked.shape[1:], lambda *_: (layer,) + (0,) * (nd - 1),
                        pipeline_mode=pl.Buffered(1))


ADA_PARTS = 3
ADA_ROWS = 8


def _ada_kernel(c_ref, w_ref, b_ref, o_ref):
    s = _silu(c_ref[...]).astype(BF16)
    o_ref[0, 0, :, 0, :] = _dot(s, w_ref[0].astype(BF16)) + b_ref[0]


def _ada_call(c8, w, b):
    nl, d, n3 = w.shape
    tn = d // 2
    per = d // tn
    return pl.pallas_call(
        _ada_kernel,
        grid=(nl, n3 // tn),
        in_specs=[pl.BlockSpec((ADA_ROWS, d), lambda l, n: (0, 0)),
                  pl.BlockSpec((1, d, tn), lambda l, n: (l, 0, n)),
                  pl.BlockSpec((1, 1, tn), lambda l, n: (l, 0, n))],
        out_specs=pl.BlockSpec((1, 1, ADA_ROWS, 1, tn), lambda l, n: (l, n // per, 0, 0, n % per)),
        out_shape=jax.ShapeDtypeStruct((nl, ADA_PARTS, ADA_ROWS, 1, d), F32),
        compiler_params=_params(("parallel", "parallel")),
        name="ada",
    )(c8, w, b.reshape(nl, 1, n3))


def _ada_row(ada, layer, part):
    d = ada.shape[-1]
    return pl.BlockSpec((None, None, None, 1, d), lambda b, *_: (layer, part, b, 0, 0))


def _norm_proj_kernel(h_ref, g_ref, sc_ref, sh_ref, w_ref, o_ref, *, tn, segments):
    x = h_ref[0]
    ms = jnp.mean(x * x, axis=-1, keepdims=True)
    z = (x * lax.rsqrt(ms + NORM_EPS) * g_ref[...]) * (1.0 + sc_ref[...]) + sh_ref[...]
    z = z.astype(BF16)
    for out0, w0, width in segments:
        for c0 in range(0, width, tn):
            cw = min(tn, width - c0)
            o_ref[0, :, out0 + c0:out0 + c0 + cw] = _dot(
                z, w_ref[:, w0 + c0:w0 + c0 + cw]).astype(o_ref.dtype)


def _norm_proj_call(h, g, ada, w, layer, segments, tm, tn):
    bsz, seq, d = h.shape
    n = w.shape[2]
    assert sum(width for _, _, width in segments) == n
    return pl.pallas_call(
        functools.partial(_norm_proj_kernel, tn=tn, segments=segments),
        grid=(bsz, seq // tm),
        in_specs=[pl.BlockSpec((1, tm, d), lambda b, t: (b, t, 0)),
                  _layer_resident(g, layer),
                  _ada_row(ada, layer, 1),
                  _ada_row(ada, layer, 0),
                  _layer_resident(w, layer)],
        out_specs=pl.BlockSpec((1, tm, n), lambda b, t: (b, t, 0)),
        out_shape=jax.ShapeDtypeStruct((bsz, seq, n), BF16),
        compiler_params=_params(("parallel", "parallel")),
        name="norm_proj",
    )(h, g, ada, ada, w)


def _finish(y, post_ref, gate_ref, h_ref, o_ref):
    ms = jnp.mean(y * y, axis=-1, keepdims=True)
    yn = y * lax.rsqrt(ms + NORM_EPS) * post_ref[...]
    o_ref[0] = h_ref[0] + gate_ref[...] * yn


def _out_even_kernel(ya_ref, yb_ref, w_ref, post_ref, gate_ref, h_ref, o_ref):
    half = ya_ref.shape[2]
    y = _dot(ya_ref[0], w_ref[:half, :]) + _dot(yb_ref[0], w_ref[half:, :])
    _finish(y, post_ref, gate_ref, h_ref, o_ref)


def _out_odd_kernel(o_in_ref, g_ref, w_ref, post_ref, gate_ref, h_ref, o_ref):
    yin = o_in_ref[0].astype(F32) * _silu(g_ref[0].astype(F32))
    y = _dot(yin.astype(BF16), w_ref[...])
    _finish(y, post_ref, gate_ref, h_ref, o_ref)


def _out_call(kernel, acts, act_specs, w, layer, post, ada, h, tm):
    bsz, seq, d = h.shape
    return pl.pallas_call(
        kernel,
        grid=(bsz, seq // tm),
        in_specs=act_specs + [
            _layer_resident(w, layer),
            _layer_resident(post, layer),
            _ada_row(ada, layer, 2),
            pl.BlockSpec((1, tm, d), lambda b, t: (b, t, 0))],
        out_specs=pl.BlockSpec((1, tm, d), lambda b, t: (b, t, 0)),
        out_shape=jax.ShapeDtypeStruct((bsz, seq, d), F32),
        compiler_params=_params(("parallel", "parallel")),
        name=kernel.__name__.strip("_"),
    )(*acts, w, post, ada, h)


def _rwkv_kernel(r_ref, k_ref, v_ref, g_ref, lo_ref,
                 mu_ref, mulo_ref, w0_ref, w2_ref, a0_ref, a2_ref, kk_ref, ka_ref, rk_ref,
                 lng_ref, lnb_ref, hsum_ref, hexp_ref,
                 o_ref,
                 carry_ref, carrylo_ref, state_ref, y_ref, *bufs, tb, ch):
    width = r_ref.shape[2]
    npair = width // LANES
    nch = tb // ch
    c2 = 2 * ch
    nsq = int(math.log2(ch)) - 1
    step_id = pl.program_id(1)
    set_a, set_b = bufs[:len(bufs) // 2], bufs[len(bufs) // 2:]

    @pl.when(step_id == 0)
    def _():
        carry_ref[...] = jnp.zeros_like(carry_ref)
        carrylo_ref[...] = jnp.zeros_like(carrylo_ref)
        state_ref[...] = jnp.zeros_like(state_ref)

    def run(wr, rd):
        rx_w, remx_w, repd_w, kd_w, repe_w, ke_w, vv_w, pc_w, bonus_w, sg_w = wr
        rx_r, remx_r, repd_r, kd_r, repe_r, ke_r, vv_r, pc_r, bonus_r, sg_r = rd or wr
        cat = jnp.concatenate

        def headsum(t):
            sums = _dot(t.astype(BF16), hsum_ref[...])
            hi = sums.astype(BF16)
            lo = (sums - hi.astype(F32)).astype(BF16)
            return _dot(cat([hi, lo], axis=1), hexp_ref[...])

        pi = lax.broadcasted_iota(jnp.int32, (c2, c2), 0)
        pj = lax.broadcasted_iota(jnp.int32, (c2, c2), 1)
        blk = (pi // ch) == (pj // ch)
        strict = blk & (pj < pi)
        incl = blk & (pj <= pi)
        m0 = lax.broadcasted_iota(jnp.int32, (ch, LANES), 1) < RWKV_HEAD

        def by_head(t):
            z = jnp.zeros_like(t)
            return cat([jnp.where(m0, t, z), jnp.where(m0, z, t)], axis=0)

        def chunk(c):
            rows = slice(c * ch, (c + 1) * ch)
            pairs = range(npair)
            cols = [slice(p * LANES, (p + 1) * LANES) for p in pairs]
            lhs = [cat([by_head(remx_r[rows, cols[p]]), by_head(rx_r[rows, cols[p]])], axis=0)
                   for p in pairs]
            m1 = [_dot_nt(lhs[p], cat([repd_r[rows, cols[p]]] * 2 + [kd_r[rows, cols[p]]] * 2, axis=0))
                  for p in pairs]
            s_old = [state_ref[p] for p in pairs]
            gs = [_dot_nt(lhs[p], s_old[p].astype(BF16)) for p in pairs]
            v_bd = [by_head(vv_r[rows, cols[p]]) for p in pairs]
            sa = [gs[p][:c2] + _dot(jnp.where(strict, m1[p][:c2, c2:], 0.0).astype(BF16), v_bd[p])
                  for p in pairs]
            pw = [jnp.where(strict, m1[p][:c2, :c2], 0.0) for p in pairs]
            for _i in range(nsq):
                both = [_dot(pw[p].astype(BF16), cat([pw[p], sa[p]], axis=1).astype(BF16)) for p in pairs]
                pw = [both[p][:, :c2] for p in pairs]
                sa = [sa[p] + both[p][:, c2:] for p in pairs]
            sa = [(sa[p] + _dot(pw[p].astype(BF16), sa[p].astype(BF16))).astype(BF16) for p in pairs]
            vals = [cat([sa[p], v_bd[p]], axis=0) for p in pairs]
            for p in pairs:
                arbr = cat([jnp.where(incl, m1[p][c2:, :c2], 0.0), jnp.where(incl, m1[p][c2:, c2:], 0.0)],
                           axis=1).astype(BF16)
                y_bd = gs[p][c2:] + _dot(arbr, vals[p])
                y_ref[rows, cols[p]] = y_bd[:ch] + y_bd[ch:]
            for p in pairs:
                upd = _dot_tn(vals[p], cat([by_head(repe_r[rows, cols[p]]), by_head(ke_r[rows, cols[p]])],
                                           axis=0))
                state_ref[p] = s_old[p] * pc_r[8 * c:8 * c + 1, cols[p]] + upd

        row0 = lax.broadcasted_iota(jnp.int32, (tb, 1), 0) == 0
        env = {}

        def shift(x, cref, slot, mu):
            prev = jnp.where(row0, cref[slot:slot + 1, :], pltpu.roll(x, 1, axis=0))
            cref[slot:slot + 1, :] = x[tb - 1:tb, :]
            return x + (prev - x) * mu

        def stage_shift():
            env["r"] = shift(r_ref[0].astype(F32), carry_ref, 0, mu_ref[0:1, :])
            env["k"] = shift(k_ref[0].astype(F32), carry_ref, 1, mu_ref[1:2, :])
            v = shift(v_ref[0].astype(F32), carry_ref, 2, mu_ref[2:3, :])
            g = shift(g_ref[0].astype(F32), carry_ref, 3, mu_ref[3:4, :])
            lo = shift(lo_ref[0].astype(F32), carrylo_ref, 0, mulo_ref[...])
            env["v"] = v
            vv_w[...] = v.astype(BF16)
            sg_w[...] = _silu(g)
            x = w0_ref[...] + _dot(jnp.tanh(lo[:, :RWKV_LORA]).astype(BF16), w2_ref[...])
            y = -x
            softplus = jnp.maximum(y, 0.0) + jnp.log(1.0 + jnp.exp(-jnp.abs(y)))
            env["ld"] = -jnp.exp(-softplus - 0.5)
            env["a"] = _sigmoid(a0_ref[...] + _dot(lo[:, RWKV_LORA:].astype(BF16), a2_ref[...]))

        def stage_keys():
            r, k, v, a = env["r"], env["k"], env["v"], env["a"]
            kk = k * kk_ref[...]
            k2 = k * (1.0 + (a - 1.0) * ka_ref[...])
            sums = headsum(cat([kk * kk, r * k2 * rk_ref[...]], axis=0))
            kk = kk * lax.rsqrt(jnp.maximum(sums[:tb], 1e-24))
            bonus_w[...] = sums[tb:] * v
            env["kk"], env["k2"], env["rep"] = kk, k2, kk * a

        def stage_decay():
            ld = env["ld"]
            ri = lax.broadcasted_iota(jnp.int32, (tb, tb), 0)
            ci = lax.broadcasted_iota(jnp.int32, (tb, tb), 1)
            lincl = jnp.where(((ri // ch) == (ci // ch)) & (ci <= ri), 1.0, 0.0).astype(BF16)
            hi = ld.astype(BF16)
            r1 = ld - hi.astype(F32)
            mid = r1.astype(BF16)
            low = (r1 - mid.astype(F32)).astype(BF16)
            parts = _dot(lincl, cat([hi, mid, low], axis=1))
            env["cum"] = parts[:, :width] + parts[:, width:2 * width] + parts[:, 2 * width:]

        def stage_operands():
            r, kk, k2, rep, ld, cum = (env[n] for n in ("r", "kk", "k2", "rep", "ld", "cum"))
            pinv = jnp.exp(-cum)
            pend = []
            for s in range(nch):
                pc = jnp.exp(cum[(s + 1) * ch - 1:(s + 1) * ch, :])
                pc_w[8 * s:8 * s + 8, :] = jnp.broadcast_to(pc, (8, width))
                pend.append(pc * pinv[s * ch:(s + 1) * ch, :])
            pend = cat(pend, axis=0)
            rx_w[...] = (r * jnp.exp(cum)).astype(BF16)
            remx_w[...] = (-kk * jnp.exp(cum - ld)).astype(BF16)
            repd_w[...] = (rep * pinv).astype(BF16)
            kd_w[...] = (k2 * pinv).astype(BF16)
            repe_w[...] = (rep * pend).astype(BF16)
            ke_w[...] = (k2 * pend).astype(BF16)

        stages = [stage_shift, stage_keys, stage_decay, stage_operands]
        if rd is None:
            for stage in stages:
                stage()
            return
        for c in range(nch):
            chunk(c)
            if c < len(stages):
                stages[c]()
        for stage in stages[nch:]:
            stage()

        yv = y_ref[...]
        inv_n = 1.0 / RWKV_HEAD
        mean = headsum(yv) * inv_n
        dlt = yv - mean
        var = headsum(dlt * dlt) * inv_n
        yn = dlt * lax.rsqrt(var + LNX_EPS) * lng_ref[...] + lnb_ref[...]
        o_ref[0] = ((yn + bonus_r[...]) * sg_r[...]).astype(o_ref.dtype)

    @pl.when(step_id == 0)
    def _():
        run(set_a, None)

    @pl.when((step_id > 0) & (step_id % 2 == 0))
    def _():
        run(set_a, set_b)

    @pl.when(step_id % 2 == 1)
    def _():
        run(set_b, set_a)


def _rwkv_call(p, params, layer, tb, ch):
    bsz, seq, _ = p.shape
    width = params[2].shape[-1]
    head_of = jnp.arange(width) // RWKV_HEAD
    hsum = (head_of[:, None] == jnp.arange(LANES)[None, :]).astype(BF16)
    hexp = jnp.concatenate([hsum.T, hsum.T], axis=0)
    npair = width // LANES
    nblk = seq // tb
    lo_blk = p.shape[-1] // LANES - 1
    act = lambda j: pl.BlockSpec((1, tb, width), lambda b, s, j=j: (b, jnp.minimum(s, nblk - 1), j))
    bf = lambda: pltpu.VMEM((tb, width), BF16)
    f32 = lambda: pltpu.VMEM((tb, width), F32)
    buf_set = lambda: [bf(), bf(), bf(), bf(), bf(), bf(), bf(),
                       pltpu.VMEM((8 * (tb // ch), width), F32), f32(), f32()]
    return pl.pallas_call(
        functools.partial(_rwkv_kernel, tb=tb, ch=ch),
        grid=(bsz, nblk + 1),
        in_specs=[act(0), act(1), act(2), act(3),
                  pl.BlockSpec((1, tb, LANES), lambda b, s: (b, jnp.minimum(s, nblk - 1), lo_blk))]
                 + [_layer_resident(a, layer) for a in params]
                 + [_resident(hsum.shape), _resident(hexp.shape)],
        out_specs=pl.BlockSpec((1, tb, width), lambda b, s: (b, jnp.maximum(s - 1, 0), 0)),
        out_shape=jax.ShapeDtypeStruct((bsz, seq, width), BF16),
        scratch_shapes=[pltpu.VMEM((8, width), F32), pltpu.VMEM((8, LANES), F32),
                        pltpu.VMEM((npair, LANES, LANES), F32), f32()] + buf_set() + buf_set(),
        compiler_params=_params(("parallel", "arbitrary")),
        name="rwkv",
    )(p, p, p, p, p, *params, hsum, hexp)


def _s5_scan_kernel(u_ref, t_ref, win_ref, wout_ref, mul_ref, y_ref, *, nck):
    u = u_ref[0]
    nstate = win_ref.shape[2] // 2
    x = _dot(u, win_ref[0])
    chunk = lax.broadcasted_iota(jnp.int32, (x.shape[0], 1), 0) % nck
    a, b = mul_ref[0, 0:1, :], mul_ref[0, 1:2, :]
    shift = 1
    while shift < nck:
        prev = jnp.where(chunk >= shift, pltpu.roll(x, shift, axis=0), 0.0)
        x = x + prev * a + pltpu.roll(prev, nstate, axis=1) * b
        a, b = a * a - b * b, 2.0 * a * b
        shift *= 2
    s_in = jnp.where(chunk >= 1, pltpu.roll(x, 1, axis=0), 0.0)
    y = _dot(u, t_ref[0]) + _dot(s_in.astype(BF16), wout_ref[0])
    y_ref[0] = y.astype(y_ref.dtype)


def _s5_scan_call(u3, tables, layer, nck):
    ngroup, rows, width = u3.shape
    toe, win, wout, mul = tables
    per_group = lambda a: pl.BlockSpec((None, 1) + a.shape[2:],
                                       lambda g: (layer, g) + (0,) * (a.ndim - 2))
    return pl.pallas_call(
        functools.partial(_s5_scan_kernel, nck=nck),
        grid=(ngroup,),
        in_specs=[pl.BlockSpec((1, rows, width), lambda g: (g, 0, 0)),
                  per_group(toe), per_group(win), per_group(wout), per_group(mul)],
        out_specs=pl.BlockSpec((1, rows, width), lambda g: (g, 0, 0)),
        out_shape=jax.ShapeDtypeStruct((ngroup, rows, width), BF16),
        compiler_params=_params(("parallel",)),
        name="s5_scan",
    )(u3, toe, win, wout, mul)


def _s5_mix_kernel(y_ref, u_ref, gate_ref, d_ref, gw_ref, gb_ref, o_ref):
    y = y_ref[0].astype(F32) + d_ref[...] * u_ref[0].astype(F32)
    cdf = 0.5 * (1.0 + jnp.tanh(math.sqrt(2.0 / math.pi) * (y + 0.044715 * (y * y * y))))
    y = y * cdf
    glu = _dot(y.astype(BF16), gw_ref[...]) + gb_ref[...]
    o_ref[0] = (y * _sigmoid(glu) * _silu(gate_ref[0].astype(F32))).astype(o_ref.dtype)


def _s5_mix_call(y, p, layer, d, glu_w, glu_b, u_blk, gate_blk, ts):
    bsz, seq, width = y.shape
    return pl.pallas_call(
        _s5_mix_kernel,
        grid=(bsz, seq // ts),
        in_specs=[pl.BlockSpec((1, ts, width), lambda b, t: (b, t, 0)),
                  pl.BlockSpec((1, ts, width), lambda b, t: (b, t, u_blk)),
                  pl.BlockSpec((1, ts, width), lambda b, t: (b, t, gate_blk)),
                  _layer_resident(d, layer), _layer_resident(glu_w, layer),
                  _layer_resident(glu_b, layer)],
        out_specs=pl.BlockSpec((1, ts, width), lambda b, t: (b, t, 0)),
        out_shape=jax.ShapeDtypeStruct((bsz, seq, width), BF16),
        compiler_params=_params(("parallel", "parallel")),
        name="s5_mix",
    )(y, p, p, d, glu_w, glu_b)


def _s5_tables(lam_re, lam_im, log_dt, b_re, b_im, c_re, c_im, sub):
    g, n = lam_re.shape
    pch = b_re.shape[-1]
    dt = jnp.exp(log_dt)[:, None]
    mag = jnp.exp(lam_re * dt)
    e_re, e_im = mag * jnp.cos(lam_im * dt), mag * jnp.sin(lam_im * dt)
    den = lam_re * lam_re + lam_im * lam_im
    coef_re = ((e_re - 1.0) * lam_re + e_im * lam_im) / den
    coef_im = (e_im * lam_re - (e_re - 1.0) * lam_im) / den
    bb_re = coef_re[..., None] * b_re - coef_im[..., None] * b_im
    bb_im = coef_re[..., None] * b_im + coef_im[..., None] * b_re
    d = jnp.arange(sub + 1, dtype=F32)[:, None, None]
    lr, li = (lam_re * dt)[None], (lam_im * dt)[None]
    pw_re, pw_im = jnp.exp(d * lr) * jnp.cos(d * li), jnp.exp(d * lr) * jnp.sin(d * li)

    def c_times(p_re, p_im):
        return (c_re[None] * p_re[:, :, None, :] - c_im[None] * p_im[:, :, None, :],
                c_re[None] * p_im[:, :, None, :] + c_im[None] * p_re[:, :, None, :])

    a_re, a_im = c_times(pw_re[:sub], pw_im[:sub])
    kern = (jnp.einsum('dgpn,gnq->dgpq', a_re, bb_re) - jnp.einsum('dgpn,gnq->dgpq', a_im, bb_im))
    lag = jnp.arange(sub)[None, :] - jnp.arange(sub)[:, None]
    toe = jnp.where((lag >= 0)[:, :, None, None, None], kern[jnp.clip(lag, 0, sub - 1)], 0.0)
    toe = toe.transpose(2, 0, 4, 1, 3).reshape(g, sub * pch, sub * pch)
    rev = sub - 1 - jnp.arange(sub)
    pin_re, pin_im = pw_re[rev][..., None], pw_im[rev][..., None]
    win = jnp.concatenate([pin_re * bb_re[None] - pin_im * bb_im[None],
                           pin_re * bb_im[None] + pin_im * bb_re[None]], axis=2)
    win = win.transpose(1, 0, 3, 2).reshape(g, sub * pch, 2 * n)
    o_re, o_im = c_times(pw_re[1:], pw_im[1:])
    wout = jnp.concatenate([o_re, -o_im], axis=3).transpose(1, 3, 0, 2).reshape(g, 2 * n, sub * pch)
    mul = jnp.stack([jnp.concatenate([pw_re[sub], pw_re[sub]], axis=-1),
                     jnp.concatenate([-pw_im[sub], pw_im[sub]], axis=-1)], axis=1)
    return toe.astype(BF16), win.astype(BF16), wout.astype(BF16), mul


def _s5_call(p, tables, layer, d, glu_w, glu_b, u_blk, gate_blk, ts, sub):
    bsz, seq, _ = p.shape
    width = d.shape[-1]
    ngroup = width // S5_GROUP
    nck = seq // sub
    u = p[:, :, u_blk * width:(u_blk + 1) * width]
    u3 = u.reshape(bsz, nck, sub, ngroup, S5_GROUP).transpose(3, 0, 1, 2, 4).reshape(
        ngroup, bsz * nck, sub * S5_GROUP)
    y3 = _s5_scan_call(u3, tables, layer, nck)
    y = y3.reshape(ngroup, bsz, nck, sub, S5_GROUP).transpose(1, 2, 3, 0, 4).reshape(bsz, seq, width)
    return _s5_mix_call(y, p, layer, d, glu_w, glu_b, u_blk, gate_blk, ts)


def _rope128(x, cos, sin):
    lane = lax.broadcasted_iota(jnp.int32, x.shape, 1)
    half = QK_ROPE // 2
    partner = jnp.where(lane < half, pltpu.roll(x, LANES - half, axis=1), pltpu.roll(x, half, axis=1))
    return x * cos + partner * sin


def _qkv_kernel(cq_ref, ckv_ref, kpe_ref, cos_ref, sin_ref, qn_ref, kvn_ref, wq_ref, wkv_ref,
                q_ref, k_ref, v_ref, *, scale):
    def latent(ref, gain_ref):
        x = ref[0].astype(F32)
        ms = jnp.mean(x * x, axis=-1, keepdims=True)
        return (x * lax.rsqrt(ms + NORM_EPS) * gain_ref[...]).astype(BF16)

    cq = latent(cq_ref, qn_ref)
    ckv = latent(ckv_ref, kvn_ref)
    cos, sin = cos_ref[0], sin_ref[0]
    kpe = _rope128(kpe_ref[0].astype(F32), cos, sin).astype(BF16)
    hw = 2 * LANES
    for h in range(q_ref.shape[1]):
        qh = _dot(cq, wq_ref[:, h * hw:(h + 1) * hw]) * scale
        q_ref[0, h] = jnp.concatenate(
            [qh[:, :LANES], _rope128(qh[:, LANES:], cos, sin)], axis=1).astype(BF16)
        kvh = _dot(ckv, wkv_ref[:, h * hw:(h + 1) * hw])
        k_ref[0, h] = jnp.concatenate([kvh[:, :LANES].astype(BF16), kpe], axis=1)
        v_ref[0, h] = kvh[:, LANES:].astype(BF16)


def _qkv_call(p, cos, sin, q_norm, kv_norm, wq, wkv, layer, cq_blk, ckv_blk, kpe_blk, tm):
    bsz, seq, _ = p.shape
    nh = MLA_HEADS
    hw = 2 * LANES
    scale = math.log2(math.e) / math.sqrt(QK_NOPE + QK_ROPE)
    head_out = lambda w: pl.BlockSpec((1, nh, tm, w), lambda b, t: (b, 0, t, 0))
    return pl.pallas_call(
        functools.partial(_qkv_kernel, scale=scale),
        grid=(bsz, seq // tm),
        in_specs=[pl.BlockSpec((1, tm, Q_LORA), lambda b, t: (b, t, cq_blk)),
                  pl.BlockSpec((1, tm, KV_LORA), lambda b, t: (b, t, ckv_blk)),
                  pl.BlockSpec((1, tm, LANES), lambda b, t: (b, t, kpe_blk)),
                  pl.BlockSpec((1, tm, LANES), lambda b, t: (b, t, 0)),
                  pl.BlockSpec((1, tm, LANES), lambda b, t: (b, t, 0)),
                  _layer_resident(q_norm, layer), _layer_resident(kv_norm, layer),
                  _layer_resident(wq, layer), _layer_resident(wkv, layer)],
        out_specs=[head_out(hw), head_out(hw), head_out(V_HEAD)],
        out_shape=[jax.ShapeDtypeStruct((bsz, nh, seq, hw), BF16),
                   jax.ShapeDtypeStruct((bsz, nh, seq, hw), BF16),
                   jax.ShapeDtypeStruct((bsz, nh, seq, V_HEAD), BF16)],
        compiler_params=_params(("parallel", "parallel")),
        name="qkv",
    )(p, p, p, cos, sin, q_norm, kv_norm, wq, wkv)


def _attn_kernel(q_ref, k_ref, v_ref, o_ref, *, tq, tk):
    qi = pl.program_id(2)
    nsub = tq // tk
    vd = v_ref.shape[3]
    qs = [q_ref[0, 0, i * tk:(i + 1) * tk, :] for i in range(nsub)]
    diag = (lax.broadcasted_iota(jnp.int32, (tk, tk), 0)
            >= lax.broadcasted_iota(jnp.int32, (tk, tk), 1))

    def scores(j, kinds):
        rows = pl.ds(pl.multiple_of(j * tk, tk), tk)
        kb = k_ref[0, 0, rows, :]
        return [None if kinds[i] is None else _dot_nt(qs[i], kb) for i in range(nsub)]

    def update(j, carries, kinds, ss):
        rows = pl.ds(pl.multiple_of(j * tk, tk), tk)
        vb = jnp.concatenate([v_ref[0, 0, rows, :], jnp.ones((tk, vd), BF16)], axis=1)
        out = []
        for i in range(nsub):
            if kinds[i] is None:
                out.append(carries[i])
                continue
            m, acc = carries[i]
            s = jnp.where(diag, ss[i], -1e30) if kinds[i] else ss[i]
            m_new = jnp.maximum(m, jnp.max(s, axis=-1, keepdims=True))
            pexp = jnp.exp2(s - m_new).astype(BF16)
            acc = jnp.exp2(m - m_new) * acc + _dot(pexp, vb)
            out.append((m_new, acc))
        return tuple(out)

    def blocks(js, carries, kinds_list):
        ss = [scores(j, kinds) for j, kinds in zip(js, kinds_list)]
        for j, kinds, s in zip(js, kinds_list, ss):
            carries = update(j, carries, kinds, s)
        return carries

    init = tuple((jnp.full((tk, 1), -1e30, F32), jnp.zeros((tk, 2 * vd), F32)) for _ in range(nsub))
    nfull = qi * nsub
    visible = [[False] * nsub] * nsub
    carries = lax.fori_loop(
        0, qi, lambda jq, c: blocks([jq * nsub + d for d in range(nsub)], c, visible), init)
    tail = [[None if i < d else (i == d) for i in range(nsub)] for d in range(nsub)]
    carries = blocks([nfull + d for d in range(nsub)], carries, tail)
    for i in range(nsub):
        _, acc = carries[i]
        o_ref[0, i * tk:(i + 1) * tk, :] = (acc[:, :vd] / acc[:, vd:]).astype(o_ref.dtype)


def _attn_call(q, k, v, tq, tk):
    bsz, nh, seq, hw = q.shape
    vd = v.shape[3]
    return pl.pallas_call(
        functools.partial(_attn_kernel, tq=tq, tk=tk),
        grid=(bsz, nh, seq // tq),
        in_specs=[pl.BlockSpec((1, 1, tq, hw), lambda b, h, i: (b, h, i, 0)),
                  pl.BlockSpec((1, 1, seq, hw), lambda b, h, i: (b, h, 0, 0)),
                  pl.BlockSpec((1, 1, seq, vd), lambda b, h, i: (b, h, 0, 0))],
        out_specs=pl.BlockSpec((1, tq, vd), lambda b, h, i: (b, i, h)),
        out_shape=jax.ShapeDtypeStruct((bsz, seq, nh * vd), BF16),
        compiler_params=_params(("parallel", "parallel", "arbitrary")),
        name="attn",
    )(q, k, v)


def _rope_tables(positions):
    inv_freq = 1.0 / (ROPE_BASE ** (jnp.arange(0, QK_ROPE, 2, dtype=F32) / QK_ROPE))
    ang = positions.astype(F32)[..., None] * inv_freq
    cos, sin = jnp.cos(ang), jnp.sin(ang)
    zero = jnp.zeros_like(cos)
    return (jnp.concatenate([cos, cos, zero, zero], axis=-1),
            jnp.concatenate([-sin, sin, zero, zero], axis=-1))


def kernel(x, c, positions, ev_ada_w, ev_ada_b, ev_norm_pre, ev_norm_post, ev_w_in, ev_mu, ev_w0, ev_w2, ev_a0, ev_a2, ev_k_k, ev_k_a, ev_r_k, ev_lnx_g, ev_lnx_b, ev_lam_re, ev_lam_im, ev_log_dt, ev_b_re, ev_b_im, ev_c_re, ev_c_im, ev_d, ev_glu_w, ev_glu_b, ev_w_out, od_ada_w, od_ada_b, od_norm_pre, od_norm_post, od_w_in, od_q_norm, od_w_q_up, od_kv_norm, od_w_kv_up, od_w_out):
    bsz, seq, d = x.shape
    n_even, n_odd = ev_w_in.shape[0], od_w_in.shape[0]
    depth = n_even + n_odd
    rw = ev_w0.shape[-1]
    sw = ev_d.shape[-1]
    a_in = 4 * rw + 2 * RWKV_LORA
    h = x.astype(F32)

    c8 = jnp.zeros((ADA_ROWS, d), F32).at[:bsz].set(c.astype(F32))
    ev_ada = _ada_call(c8, ev_ada_w, ev_ada_b)
    od_ada = _ada_call(c8, od_ada_w, od_ada_b)
    cos, sin = _rope_tables(positions)

    ev_w_in_b = ev_w_in.astype(BF16)
    ev_in = ev_w_in.shape[2]
    ev_segments = ((0, 0, 4 * rw), (4 * rw, a_in, ev_in - a_in), (4 * rw + ev_in - a_in, 4 * rw, a_in - 4 * rw))
    ev_w_out_b, ev_glu_w_b = ev_w_out.astype(BF16), ev_glu_w.astype(BF16)
    s5_tables = jax.vmap(functools.partial(_s5_tables, sub=S5_SUB))(
        ev_lam_re, ev_lam_im, ev_log_dt, ev_b_re, ev_b_im, ev_c_re, ev_c_im)
    o2, o3 = Q_LORA + KV_LORA, Q_LORA + KV_LORA + QK_ROPE
    mw = od_w_in.shape[2] - o3
    od_w_in_p = jnp.concatenate([od_w_in[:, :, o3:], od_w_in[:, :, :o3],
                                 jnp.zeros((n_odd, d, LANES - QK_ROPE), od_w_in.dtype)],
                                axis=2).astype(BF16)
    nh = MLA_HEADS
    wq = od_w_q_up.reshape(n_odd, Q_LORA, nh, QK_NOPE + QK_ROPE)
    wq = jnp.pad(wq, ((0, 0), (0, 0), (0, 0), (0, 2 * LANES - QK_NOPE - QK_ROPE)))
    wq = wq.reshape(n_odd, Q_LORA, nh * 2 * LANES).astype(BF16)
    wkv = od_w_kv_up.astype(BF16)
    od_w_out_b = od_w_out.astype(BF16)

    rows = lambda a: a.reshape(a.shape[0], 1, a.shape[-1])
    rwkv_params = [ev_mu[:, :4 * rw].reshape(n_even, 4, rw), rows(ev_mu[:, 4 * rw:]),
                   rows(ev_w0), ev_w2.astype(BF16), rows(ev_a0), ev_a2.astype(BF16),
                   rows(ev_k_k), rows(ev_k_a), rows(ev_r_k), rows(ev_lnx_g), rows(ev_lnx_b)]
    ev_pre, ev_post, od_pre, od_post = (rows(a) for a in (ev_norm_pre, ev_norm_post,
                                                          od_norm_pre, od_norm_post))
    ev_d_r, ev_glu_b_r, od_qn, od_kvn = (rows(a) for a in (ev_d, ev_glu_b, od_q_norm, od_kv_norm))

    tm_out = min(512, seq)
    for i in range(depth):
        j = i // 2
        if i % 2 == 0:
            p = _norm_proj_call(h, ev_pre, ev_ada, ev_w_in_b, j, ev_segments,
                                tm=min(256, seq), tn=PROJ_TN)
            y_a = _rwkv_call(p, rwkv_params, j, tb=min(RWKV_TB, seq), ch=RWKV_CHUNK)
            y_b = _s5_call(p, s5_tables, j, ev_d_r, ev_glu_w_b, ev_glu_b_r,
                           u_blk=(4 * rw) // sw, gate_blk=(4 * rw) // sw + 1,
                           ts=min(S5_TS, seq), sub=S5_SUB)
            half = pl.BlockSpec((1, tm_out, rw), lambda b, t_: (b, t_, 0))
            h = _out_call(_out_even_kernel, [y_a, y_b], [half, half],
                          ev_w_out_b, j, ev_post, ev_ada, h, tm_out)
        else:
            p = _norm_proj_call(h, od_pre, od_ada, od_w_in_p, j, ((0, 0, od_w_in_p.shape[2]),),
                                tm=min(512, seq), tn=PROJ_TN)
            q, k, v = _qkv_call(p, cos, sin, od_qn, od_kvn, wq, wkv, j,
                                cq_blk=mw // Q_LORA, ckv_blk=mw // KV_LORA + 1,
                                kpe_blk=(mw + o2) // LANES, tm=min(512, seq))
            o = _attn_call(q, k, v, tq=min(2048, seq), tk=min(512, seq))
            full = pl.BlockSpec((1, tm_out, mw), lambda b, t_: (b, t_, 0))
            h = _out_call(_out_odd_kernel, [o, p], [full, full],
                          od_w_out_b, j, od_post, od_ada, h, tm_out)
    return h.astype(x.dtype)
```

```python
import functools
import math

import jax
import jax.numpy as jnp
from jax import lax
from jax.experimental import pallas as pl
from jax.experimental.pallas import tpu as pltpu

F32 = jnp.float32
BF16 = jnp.bfloat16

NORM_EPS = 1e-6
LNX_EPS = 64e-5
ROPE_BASE = 10000.0

RWKV_HEAD = 64
RWKV_LORA = 64
S5_GROUP = 16
S5_STATE = 64
MLA_HEADS = 16
QK_NOPE = 128
QK_ROPE = 64
V_HEAD = 128
Q_LORA = 512
KV_LORA = 512

LANES = 128
V7X_VMEM_LIMIT = 56 * 1024 * 1024

RWKV_CHUNK = 64
RWKV_TB = 256
S5_CHUNK = 64
S5_TS = 256
S5_GT = 16
PROJ_TN = 1024


def _params(sem, vmem=V7X_VMEM_LIMIT):
    return pltpu.CompilerParams(dimension_semantics=sem, vmem_limit_bytes=vmem)


def _sigmoid(x):
    return 1.0 / (1.0 + jnp.exp(-x))


def _silu(x):
    return x * _sigmoid(x)


def _dot(a, b):
    return jnp.dot(a, b, preferred_element_type=F32)


def _dot_nt(a, b):
    return lax.dot_general(a, b, (((1,), (1,)), ((), ())), preferred_element_type=F32)


def _dot_tn(a, b):
    return lax.dot_general(a, b, (((0,), (0,)), ((), ())), preferred_element_type=F32)


def _resident(shape):
    nd = len(shape)
    return pl.BlockSpec(shape, lambda *_: (0,) * nd, pipeline_mode=pl.Buffered(1))


def _layer_resident(stacked, layer):
    nd = stacked.ndim
    return pl.BlockSpec((None,) + stacked.shape[1:], lambda *_: (layer,) + (0,) * (nd - 1),
                        pipeline_mode=pl.Buffered(1))


ADA_PARTS = 3
ADA_ROWS = 8


def _ada_kernel(c_ref, w_ref, b_ref, o_ref):
    s = _silu(c_ref[...]).astype(BF16)
    o_ref[0, 0, :, 0, :] = _dot(s, w_ref[0].astype(BF16)) + b_ref[0]


def _ada_call(c8, w, b):
    nl, d, n3 = w.shape
    tn = d // 2
    per = d // tn
    return pl.pallas_call(
        _ada_kernel,
        grid=(nl, n3 // tn),
        in_specs=[pl.BlockSpec((ADA_ROWS, d), lambda l, n: (0, 0)),
                  pl.BlockSpec((1, d, tn), lambda l, n: (l, 0, n)),
                  pl.BlockSpec((1, 1, tn), lambda l, n: (l, 0, n))],
        out_specs=pl.BlockSpec((1, 1, ADA_ROWS, 1, tn), lambda l, n: (l, n // per, 0, 0, n % per)),
        out_shape=jax.ShapeDtypeStruct((nl, ADA_PARTS, ADA_ROWS, 1, d), F32),
        compiler_params=_params(("parallel", "parallel")),
        name="ada",
    )(c8, w, b.reshape(nl, 1, n3))


def _ada_row(ada, layer, part):
    d = ada.shape[-1]
    return pl.BlockSpec((None, None, None, 1, d), lambda b, *_: (layer, part, b, 0, 0))


def _norm_proj_kernel(h_ref, g_ref, sc_ref, sh_ref, w_ref, o_ref, *, tn, segments):
    x = h_ref[0]
    ms = jnp.mean(x * x, axis=-1, keepdims=True)
    z = (x * lax.rsqrt(ms + NORM_EPS) * g_ref[...]) * (1.0 + sc_ref[...]) + sh_ref[...]
    z = z.astype(BF16)
    for out0, w0, width in segments:
        for c0 in range(0, width, tn):
            cw = min(tn, width - c0)
            o_ref[0, :, out0 + c0:out0 + c0 + cw] = _dot(
                z, w_ref[:, w0 + c0:w0 + c0 + cw]).astype(o_ref.dtype)


def _norm_proj_call(h, g, ada, w, layer, segments, tm, tn):
    bsz, seq, d = h.shape
    n = w.shape[2]
    assert sum(width for _, _, width in segments) == n
    return pl.pallas_call(
        functools.partial(_norm_proj_kernel, tn=tn, segments=segments),
        grid=(bsz, seq // tm),
        in_specs=[pl.BlockSpec((1, tm, d), lambda b, t: (b, t, 0)),
                  _layer_resident(g, layer),
                  _ada_row(ada, layer, 1),
                  _ada_row(ada, layer, 0),
                  _layer_resident(w, layer)],
        out_specs=pl.BlockSpec((1, tm, n), lambda b, t: (b, t, 0)),
        out_shape=jax.ShapeDtypeStruct((bsz, seq, n), BF16),
        compiler_params=_params(("parallel", "parallel")),
        name="norm_proj",
    )(h, g, ada, ada, w)


def _finish(y, post_ref, gate_ref, h_ref, o_ref):
    ms = jnp.mean(y * y, axis=-1, keepdims=True)
    yn = y * lax.rsqrt(ms + NORM_EPS) * post_ref[...]
    o_ref[0] = h_ref[0] + gate_ref[...] * yn


def _out_even_kernel(ya_ref, yb_ref, w_ref, post_ref, gate_ref, h_ref, o_ref):
    half = ya_ref.shape[2]
    y = _dot(ya_ref[0], w_ref[:half, :]) + _dot(yb_ref[0], w_ref[half:, :])
    _finish(y, post_ref, gate_ref, h_ref, o_ref)


def _out_odd_kernel(o_in_ref, g_ref, w_ref, post_ref, gate_ref, h_ref, o_ref):
    yin = o_in_ref[0].astype(F32) * _silu(g_ref[0].astype(F32))
    y = _dot(yin.astype(BF16), w_ref[...])
    _finish(y, post_ref, gate_ref, h_ref, o_ref)


def _out_call(kernel, acts, act_specs, w, layer, post, ada, h, tm):
    bsz, seq, d = h.shape
    return pl.pallas_call(
        kernel,
        grid=(bsz, seq // tm),
        in_specs=act_specs + [
            _layer_resident(w, layer),
            _layer_resident(post, layer),
            _ada_row(ada, layer, 2),
            pl.BlockSpec((1, tm, d), lambda b, t: (b, t, 0))],
        out_specs=pl.BlockSpec((1, tm, d), lambda b, t: (b, t, 0)),
        out_shape=jax.ShapeDtypeStruct((bsz, seq, d), F32),
        compiler_params=_params(("parallel", "parallel")),
        name=kernel.__name__.strip("_"),
    )(*acts, w, post, ada, h)


def _rwkv_kernel(r_ref, k_ref, v_ref, g_ref, lo_ref,
                 mu_ref, mulo_ref, w0_ref, w2_ref, a0_ref, a2_ref, kk_ref, ka_ref, rk_ref,
                 lng_ref, lnb_ref, hsum_ref, hexp_ref,
                 o_ref,
                 carry_ref, carrylo_ref, state_ref, y_ref, *bufs, tb, ch):
    width = r_ref.shape[2]
    npair = width // LANES
    nch = tb // ch
    c2 = 2 * ch
    nsq = int(math.log2(ch)) - 1
    step_id = pl.program_id(1)
    set_a, set_b = bufs[:len(bufs) // 2], bufs[len(bufs) // 2:]

    @pl.when(step_id == 0)
    def _():
        carry_ref[...] = jnp.zeros_like(carry_ref)
        carrylo_ref[...] = jnp.zeros_like(carrylo_ref)
        state_ref[...] = jnp.zeros_like(state_ref)

    def run(wr, rd):
        rx_w, remx_w, repd_w, kd_w, repe_w, ke_w, vv_w, pc_w, bonus_w, sg_w = wr
        rx_r, remx_r, repd_r, kd_r, repe_r, ke_r, vv_r, pc_r, bonus_r, sg_r = rd or wr
        cat = jnp.concatenate

        def headsum(t):
            sums = _dot(t.astype(BF16), hsum_ref[...])
            hi = sums.astype(BF16)
            lo = (sums - hi.astype(F32)).astype(BF16)
            return _dot(cat([hi, lo], axis=1), hexp_ref[...])

        pi = lax.broadcasted_iota(jnp.int32, (c2, c2), 0)
        pj = lax.broadcasted_iota(jnp.int32, (c2, c2), 1)
        blk = (pi // ch) == (pj // ch)
        strict = blk & (pj < pi)
        incl = blk & (pj <= pi)
        m0 = lax.broadcasted_iota(jnp.int32, (ch, LANES), 1) < RWKV_HEAD

        def by_head(t):
            z = jnp.zeros_like(t)
            return cat([jnp.where(m0, t, z), jnp.where(m0, z, t)], axis=0)

        def chunk(c):
            rows = slice(c * ch, (c + 1) * ch)
            pairs = range(npair)
            cols = [slice(p * LANES, (p + 1) * LANES) for p in pairs]
            lhs = [cat([by_head(remx_r[rows, cols[p]]), by_head(rx_r[rows, cols[p]])], axis=0)
                   for p in pairs]
            m1 = [_dot_nt(lhs[p], cat([repd_r[rows, cols[p]]] * 2 + [kd_r[rows, cols[p]]] * 2, axis=0))
                  for p in pairs]
            s_old = [state_ref[p] for p in pairs]
            gs = [_dot_nt(lhs[p], s_old[p].astype(BF16)) for p in pairs]
            v_bd = [by_head(vv_r[rows, cols[p]]) for p in pairs]
            sa = [gs[p][:c2] + _dot(jnp.where(strict, m1[p][:c2, c2:], 0.0).astype(BF16), v_bd[p])
                  for p in pairs]
            pw = [jnp.where(strict, m1[p][:c2, :c2], 0.0) for p in pairs]
            for _i in range(nsq):
                both = [_dot(pw[p].astype(BF16), cat([pw[p], sa[p]], axis=1).astype(BF16)) for p in pairs]
                pw = [both[p][:, :c2] for p in pairs]
                sa = [sa[p] + both[p][:, c2:] for p in pairs]
            sa = [(sa[p] + _dot(pw[p].astype(BF16), sa[p].astype(BF16))).astype(BF16) for p in pairs]
            vals = [cat([sa[p], v_bd[p]], axis=0) for p in pairs]
            for p in pairs:
                arbr = cat([jnp.where(incl, m1[p][c2:, :c2], 0.0), jnp.where(incl, m1[p][c2:, c2:], 0.0)],
                           axis=1).astype(BF16)
                y_bd = gs[p][c2:] + _dot(arbr, vals[p])
                y_ref[rows, cols[p]] = y_bd[:ch] + y_bd[ch:]
            for p in pairs:
                upd = _dot_tn(vals[p], cat([by_head(repe_r[rows, cols[p]]), by_head(ke_r[rows, cols[p]])],
                                           axis=0))
                state_ref[p] = s_old[p] * pc_r[8 * c:8 * c + 1, cols[p]] + upd

        row0 = lax.broadcasted_iota(jnp.int32, (tb, 1), 0) == 0
        env = {}

        def shift(x, cref, slot, mu):
            prev = jnp.where(row0, cref[slot:slot + 1, :], pltpu.roll(x, 1, axis=0))
            cref[slot:slot + 1, :] = x[tb - 1:tb, :]
            return x + (prev - x) * mu

        def stage_shift():
            env["r"] = shift(r_ref[0].astype(F32), carry_ref, 0, mu_ref[0:1, :])
            env["k"] = shift(k_ref[0].astype(F32), carry_ref, 1, mu_ref[1:2, :])
            v = shift(v_ref[0].astype(F32), carry_ref, 2, mu_ref[2:3, :])
            g = shift(g_ref[0].astype(F32), carry_ref, 3, mu_ref[3:4, :])
            lo = shift(lo_ref[0].astype(F32), carrylo_ref, 0, mulo_ref[...])
            env["v"] = v
            vv_w[...] = v.astype(BF16)
            sg_w[...] = _silu(g)
            x = w0_ref[...] + _dot(jnp.tanh(lo[:, :RWKV_LORA]).astype(BF16), w2_ref[...])
            y = -x
            softplus = jnp.maximum(y, 0.0) + jnp.log(1.0 + jnp.exp(-jnp.abs(y)))
            env["ld"] = -jnp.exp(-softplus - 0.5)
            env["a"] = _sigmoid(a0_ref[...] + _dot(lo[:, RWKV_LORA:].astype(BF16), a2_ref[...]))

        def stage_keys():
            r, k, v, a = env["r"], env["k"], env["v"], env["a"]
            kk = k * kk_ref[...]
            k2 = k * (1.0 + (a - 1.0) * ka_ref[...])
            sums = headsum(cat([kk * kk, r * k2 * rk_ref[...]], axis=0))
            kk = kk * lax.rsqrt(jnp.maximum(sums[:tb], 1e-24))
            bonus_w[...] = sums[tb:] * v
            env["kk"], env["k2"], env["rep"] = kk, k2, kk * a

        def stage_decay():
            ld = env["ld"]
            ri = lax.broadcasted_iota(jnp.int32, (tb, tb), 0)
            ci = lax.broadcasted_iota(jnp.int32, (tb, tb), 1)
            lincl = jnp.where(((ri // ch) == (ci // ch)) & (ci <= ri), 1.0, 0.0).astype(BF16)
            hi = ld.astype(BF16)
            r1 = ld - hi.astype(F32)
            mid = r1.astype(BF16)
            low = (r1 - mid.astype(F32)).astype(BF16)
            parts = _dot(lincl, cat([hi, mid, low], axis=1))
            env["cum"] = parts[:, :width] + parts[:, width:2 * width] + parts[:, 2 * width:]

        def stage_operands():
            r, kk, k2, rep, ld, cum = (env[n] for n in ("r", "kk", "k2", "rep", "ld", "cum"))
            pinv = jnp.exp(-cum)
            pend = []
            for s in range(nch):
                pc = jnp.exp(cum[(s + 1) * ch - 1:(s + 1) * ch, :])
                pc_w[8 * s:8 * s + 8, :] = jnp.broadcast_to(pc, (8, width))
                pend.append(pc * pinv[s * ch:(s + 1) * ch, :])
            pend = cat(pend, axis=0)
            rx_w[...] = (r * jnp.exp(cum)).astype(BF16)
            remx_w[...] = (-kk * jnp.exp(cum - ld)).astype(BF16)
            repd_w[...] = (rep * pinv).astype(BF16)
            kd_w[...] = (k2 * pinv).astype(BF16)
            repe_w[...] = (rep * pend).astype(BF16)
            ke_w[...] = (k2 * pend).astype(BF16)

        stages = [stage_shift, stage_keys, stage_decay, stage_operands]
        if rd is None:
            for stage in stages:
                stage()
            return
        for c in range(nch):
            chunk(c)
            if c < len(stages):
                stages[c]()
        for stage in stages[nch:]:
            stage()

        yv = y_ref[...]
        inv_n = 1.0 / RWKV_HEAD
        mean = headsum(yv) * inv_n
        dlt = yv - mean
        var = headsum(dlt * dlt) * inv_n
        yn = dlt * lax.rsqrt(var + LNX_EPS) * lng_ref[...] + lnb_ref[...]
        o_ref[0] = ((yn + bonus_r[...]) * sg_r[...]).astype(o_ref.dtype)

    @pl.when(step_id == 0)
    def _():
        run(set_a, None)

    @pl.when((step_id > 0) & (step_id % 2 == 0))
    def _():
        run(set_a, set_b)

    @pl.when(step_id % 2 == 1)
    def _():
        run(set_b, set_a)


def _rwkv_call(p, params, layer, tb, ch):
    bsz, seq, _ = p.shape
    width = params[2].shape[-1]
    head_of = jnp.arange(width) // RWKV_HEAD
    hsum = (head_of[:, None] == jnp.arange(LANES)[None, :]).astype(BF16)
    hexp = jnp.concatenate([hsum.T, hsum.T], axis=0)
    npair = width // LANES
    nblk = seq // tb
    lo_blk = p.shape[-1] // LANES - 1
    act = lambda j: pl.BlockSpec((1, tb, width), lambda b, s, j=j: (b, jnp.minimum(s, nblk - 1), j))
    bf = lambda: pltpu.VMEM((tb, width), BF16)
    f32 = lambda: pltpu.VMEM((tb, width), F32)
    buf_set = lambda: [bf(), bf(), bf(), bf(), bf(), bf(), bf(),
                       pltpu.VMEM((8 * (tb // ch), width), F32), f32(), f32()]
    return pl.pallas_call(
        functools.partial(_rwkv_kernel, tb=tb, ch=ch),
        grid=(bsz, nblk + 1),
        in_specs=[act(0), act(1), act(2), act(3),
                  pl.BlockSpec((1, tb, LANES), lambda b, s: (b, jnp.minimum(s, nblk - 1), lo_blk))]
                 + [_layer_resident(a, layer) for a in params]
                 + [_resident(hsum.shape), _resident(hexp.shape)],
        out_specs=pl.BlockSpec((1, tb, width), lambda b, s: (b, jnp.maximum(s - 1, 0), 0)),
        out_shape=jax.ShapeDtypeStruct((bsz, seq, width), BF16),
        scratch_shapes=[pltpu.VMEM((8, width), F32), pltpu.VMEM((8, LANES), F32),
                        pltpu.VMEM((npair, LANES, LANES), F32), f32()] + buf_set() + buf_set(),
        compiler_params=_params(("parallel", "arbitrary")),
        name="rwkv",
    )(p, p, p, p, p, *params, hsum, hexp)


def _s5_kernel(u_ref, gate_ref, bblk_ref, cblk_ref, dnr_ref, dni_ref, dpr_ref, dpi_ref,
               lre_ref, lim_ref, d_ref, gw_ref, gb_ref, o_ref, xr_ref, xi_ref, y_ref, *, ts, cs):
    @pl.when(pl.program_id(1) == 0)
    def _():
        xr_ref[...] = jnp.zeros_like(xr_ref)
        xi_ref[...] = jnp.zeros_like(xi_ref)

    ntile, ucols, scols2 = bblk_ref.shape
    scols = scols2 // 2
    nsub = ts // cs
    u_bf = u_ref[0]
    ri = lax.broadcasted_iota(jnp.int32, (ts, ts), 0)
    ci = lax.broadcasted_iota(jnp.int32, (ts, ts), 1)
    ltri = jnp.where(((ri // cs) == (ci // cs)) & (ci <= ri), 1.0, 0.0).astype(BF16)

    tiles = range(ntile)
    cols = [slice(t * scols, (t + 1) * scols) for t in tiles]
    cat = jnp.concatenate

    bu = [_dot(u_bf[:, t * ucols:(t + 1) * ucols], bblk_ref[t]) for t in tiles]

    def scaled_inputs(t):
        dnr, dni = dnr_ref[:, cols[t]], dni_ref[:, cols[t]]
        zr, zi = [], []
        for s in range(nsub):
            br = bu[t][s * cs:(s + 1) * cs, :scols]
            bi = bu[t][s * cs:(s + 1) * cs, scols:]
            zr.append(br * dnr - bi * dni)
            zi.append(br * dni + bi * dnr)
        return cat([cat(zr, axis=0), cat(zi, axis=0)], axis=1).astype(BF16)

    csum = [_dot(ltri, scaled_inputs(t)) for t in tiles]

    def states(t):
        sc = cols[t]
        dpr, dpi = dpr_ref[:, sc], dpi_ref[:, sc]
        lre, lim = lre_ref[:, sc], lim_ref[:, sc]
        pr, pi_ = xr_ref[0:1, sc], xi_ref[0:1, sc]
        xr, xi = [], []
        for s in range(nsub):
            ar = lre * pr - lim * pi_
            ai = lre * pi_ + lim * pr
            cr = csum[t][s * cs:(s + 1) * cs, :scols] + ar
            cim = csum[t][s * cs:(s + 1) * cs, scols:] + ai
            xs_r = cr * dpr - cim * dpi
            xs_i = cr * dpi + cim * dpr
            pr, pi_ = xs_r[cs - 1:cs, :], xs_i[cs - 1:cs, :]
            xr.append(xs_r)
            xi.append(xs_i)
        xr_ref[0:1, sc] = pr
        xi_ref[0:1, sc] = pi_
        return cat([cat(xr, axis=0), cat(xi, axis=0)], axis=1).astype(BF16)

    for t in tiles:
        y_ref[:, t * ucols:(t + 1) * ucols] = _dot(states(t), cblk_ref[t])

    y = y_ref[...] + d_ref[...] * u_bf.astype(F32)
    cdf = 0.5 * (1.0 + jnp.tanh(math.sqrt(2.0 / math.pi) * (y + 0.044715 * (y * y * y))))
    y = y * cdf
    glu = _dot(y.astype(BF16), gw_ref[...]) + gb_ref[...]
    o_ref[0] = (y * _sigmoid(glu) * _silu(gate_ref[0].astype(F32))).astype(o_ref.dtype)


def _s5_tables(lam_re, lam_im, log_dt, b_re, b_im, c_re, c_im, cs):
    g, n = lam_re.shape
    pch = b_re.shape[-1]
    nt = g // S5_GT
    dt = jnp.exp(log_dt)[:, None]
    mag = jnp.exp(lam_re * dt)
    e_re, e_im = mag * jnp.cos(lam_im * dt), mag * jnp.sin(lam_im * dt)
    den = lam_re * lam_re + lam_im * lam_im
    coef_re = ((e_re - 1.0) * lam_re + e_im * lam_im) / den
    coef_im = (e_im * lam_re - (e_re - 1.0) * lam_im) / den
    bb_re = coef_re[..., None] * b_re - coef_im[..., None] * b_im
    bb_im = coef_re[..., None] * b_im + coef_im[..., None] * b_re
    eye = jnp.eye(S5_GT, dtype=F32)

    def btile(bb):
        return jnp.einsum('tgnq,gh->tgqhn', bb.reshape(nt, S5_GT, n, pch), eye).reshape(
            nt, S5_GT * pch, S5_GT * n)

    def ctile(cc):
        return jnp.einsum('tgpn,gh->tgnhp', cc.reshape(nt, S5_GT, pch, n), eye).reshape(
            nt, S5_GT * n, S5_GT * pch)

    bblk = jnp.concatenate([btile(bb_re), btile(bb_im)], axis=-1).astype(BF16)
    cblk = jnp.concatenate([ctile(c_re), -ctile(c_im)], axis=1).astype(BF16)
    j = jnp.arange(cs, dtype=F32)[:, None, None]
    lr, li = (lam_re * dt)[None], (lam_im * dt)[None]
    flat = lambda t: t.reshape(t.shape[0], g * n)
    dpr, dpi = flat(jnp.exp(j * lr) * jnp.cos(j * li)), flat(jnp.exp(j * lr) * jnp.sin(j * li))
    dnr, dni = flat(jnp.exp(-j * lr) * jnp.cos(j * li)), flat(-jnp.exp(-j * lr) * jnp.sin(j * li))
    return bblk, cblk, dnr, dni, dpr, dpi, e_re.reshape(1, g * n), e_im.reshape(1, g * n)


def _s5_call(p, tables, layer, d, glu_w, glu_b, u_blk, gate_blk, ts, cs):
    bsz, seq, _ = p.shape
    width = d.shape[-1]
    bblk, cblk, dnr, dni, dpr, dpi, lre, lim = tables
    nstate = lre.shape[-1]
    res = lambda a: _layer_resident(a, layer)
    return pl.pallas_call(
        functools.partial(_s5_kernel, ts=ts, cs=cs),
        grid=(bsz, seq // ts),
        in_specs=[pl.BlockSpec((1, ts, width), lambda b, t: (b, t, u_blk)),
                  pl.BlockSpec((1, ts, width), lambda b, t: (b, t, gate_blk)),
                  res(bblk), res(cblk), res(dnr), res(dni), res(dpr), res(dpi), res(lre), res(lim),
                  res(d), res(glu_w), res(glu_b)],
        out_specs=pl.BlockSpec((1, ts, width), lambda b, t: (b, t, 0)),
        out_shape=jax.ShapeDtypeStruct((bsz, seq, width), BF16),
        scratch_shapes=[pltpu.VMEM((8, nstate), F32), pltpu.VMEM((8, nstate), F32),
                        pltpu.VMEM((ts, width), F32)],
        compiler_params=_params(("parallel", "arbitrary")),
        name="s5",
    )(p, p, bblk, cblk, dnr, dni, dpr, dpi, lre, lim, d, glu_w, glu_b)


def _rope128(x, cos, sin):
    lane = lax.broadcasted_iota(jnp.int32, x.shape, 1)
    half = QK_ROPE // 2
    partner = jnp.where(lane < half, pltpu.roll(x, LANES - half, axis=1), pltpu.roll(x, half, axis=1))
    return x * cos + partner * sin


def _qkv_kernel(cq_ref, ckv_ref, kpe_ref, cos_ref, sin_ref, qn_ref, kvn_ref, wq_ref, wkv_ref,
                q_ref, k_ref, v_ref, *, scale):
    def latent(ref, gain_ref):
        x = ref[0].astype(F32)
        ms = jnp.mean(x * x, axis=-1, keepdims=True)
        return (x * lax.rsqrt(ms + NORM_EPS) * gain_ref[...]).astype(BF16)

    cq = latent(cq_ref, qn_ref)
    ckv = latent(ckv_ref, kvn_ref)
    cos, sin = cos_ref[0], sin_ref[0]
    kpe = _rope128(kpe_ref[0].astype(F32), cos, sin).astype(BF16)
    hw = 2 * LANES
    for h in range(q_ref.shape[1]):
        qh = _dot(cq, wq_ref[:, h * hw:(h + 1) * hw]) * scale
        q_ref[0, h] = jnp.concatenate(
            [qh[:, :LANES], _rope128(qh[:, LANES:], cos, sin)], axis=1).astype(BF16)
        kvh = _dot(ckv, wkv_ref[:, h * hw:(h + 1) * hw])
        k_ref[0, h] = jnp.concatenate([kvh[:, :LANES].astype(BF16), kpe], axis=1)
        v_ref[0, h] = kvh[:, LANES:].astype(BF16)


def _qkv_call(p, cos, sin, q_norm, kv_norm, wq, wkv, layer, cq_blk, ckv_blk, kpe_blk, tm):
    bsz, seq, _ = p.shape
    nh = MLA_HEADS
    hw = 2 * LANES
    scale = math.log2(math.e) / math.sqrt(QK_NOPE + QK_ROPE)
    head_out = lambda w: pl.BlockSpec((1, nh, tm, w), lambda b, t: (b, 0, t, 0))
    return pl.pallas_call(
        functools.partial(_qkv_kernel, scale=scale),
        grid=(bsz, seq // tm),
        in_specs=[pl.BlockSpec((1, tm, Q_LORA), lambda b, t: (b, t, cq_blk)),
                  pl.BlockSpec((1, tm, KV_LORA), lambda b, t: (b, t, ckv_blk)),
                  pl.BlockSpec((1, tm, LANES), lambda b, t: (b, t, kpe_blk)),
                  pl.BlockSpec((1, tm, LANES), lambda b, t: (b, t, 0)),
                  pl.BlockSpec((1, tm, LANES), lambda b, t: (b, t, 0)),
                  _layer_resident(q_norm, layer), _layer_resident(kv_norm, layer),
                  _layer_resident(wq, layer), _layer_resident(wkv, layer)],
        out_specs=[head_out(hw), head_out(hw), head_out(V_HEAD)],
        out_shape=[jax.ShapeDtypeStruct((bsz, nh, seq, hw), BF16),
                   jax.ShapeDtypeStruct((bsz, nh, seq, hw), BF16),
                   jax.ShapeDtypeStruct((bsz, nh, seq, V_HEAD), BF16)],
        compiler_params=_params(("parallel", "parallel")),
        name="qkv",
    )(p, p, p, cos, sin, q_norm, kv_norm, wq, wkv)


def _attn_kernel(q_ref, k_ref, v_ref, o_ref, *, tq, tk):
    qi = pl.program_id(2)
    nsub = tq // tk
    vd = v_ref.shape[3]
    qs = [q_ref[0, 0, i * tk:(i + 1) * tk, :] for i in range(nsub)]
    diag = (lax.broadcasted_iota(jnp.int32, (tk, tk), 0)
            >= lax.broadcasted_iota(jnp.int32, (tk, tk), 1))

    def scores(j, kinds):
        rows = pl.ds(pl.multiple_of(j * tk, tk), tk)
        kb = k_ref[0, 0, rows, :]
        return [None if kinds[i] is None else _dot_nt(qs[i], kb) for i in range(nsub)]

    def update(j, carries, kinds, ss):
        rows = pl.ds(pl.multiple_of(j * tk, tk), tk)
        vb = jnp.concatenate([v_ref[0, 0, rows, :], jnp.ones((tk, vd), BF16)], axis=1)
        out = []
        for i in range(nsub):
            if kinds[i] is None:
                out.append(carries[i])
                continue
            m, acc = carries[i]
            s = jnp.where(diag, ss[i], -1e30) if kinds[i] else ss[i]
            m_new = jnp.maximum(m, jnp.max(s, axis=-1, keepdims=True))
            pexp = jnp.exp2(s - m_new).astype(BF16)
            acc = jnp.exp2(m - m_new) * acc + _dot(pexp, vb)
            out.append((m_new, acc))
        return tuple(out)

    def blocks(js, carries, kinds_list):
        ss = [scores(j, kinds) for j, kinds in zip(js, kinds_list)]
        for j, kinds, s in zip(js, kinds_list, ss):
            carries = update(j, carries, kinds, s)
        return carries

    init = tuple((jnp.full((tk, 1), -1e30, F32), jnp.zeros((tk, 2 * vd), F32)) for _ in range(nsub))
    nfull = qi * nsub
    visible = [[False] * nsub] * nsub
    carries = lax.fori_loop(
        0, qi, lambda jq, c: blocks([jq * nsub + d for d in range(nsub)], c, visible), init)
    tail = [[None if i < d else (i == d) for i in range(nsub)] for d in range(nsub)]
    carries = blocks([nfull + d for d in range(nsub)], carries, tail)
    for i in range(nsub):
        _, acc = carries[i]
        o_ref[0, i * tk:(i + 1) * tk, :] = (acc[:, :vd] / acc[:, vd:]).astype(o_ref.dtype)


def _attn_call(q, k, v, tq, tk):
    bsz, nh, seq, hw = q.shape
    vd = v.shape[3]
    return pl.pallas_call(
        functools.partial(_attn_kernel, tq=tq, tk=tk),
        grid=(bsz, nh, seq // tq),
        in_specs=[pl.BlockSpec((1, 1, tq, hw), lambda b, h, i: (b, h, i, 0)),
                  pl.BlockSpec((1, 1, seq, hw), lambda b, h, i: (b, h, 0, 0)),
                  pl.BlockSpec((1, 1, seq, vd), lambda b, h, i: (b, h, 0, 0))],
        out_specs=pl.BlockSpec((1, tq, vd), lambda b, h, i: (b, i, h)),
        out_shape=jax.ShapeDtypeStruct((bsz, seq, nh * vd), BF16),
        compiler_params=_params(("parallel", "parallel", "arbitrary")),
        name="attn",
    )(q, k, v)


def _rope_tables(positions):
    inv_freq = 1.0 / (ROPE_BASE ** (jnp.arange(0, QK_ROPE, 2, dtype=F32) / QK_ROPE))
    ang = positions.astype(F32)[..., None] * inv_freq
    cos, sin = jnp.cos(ang), jnp.sin(ang)
    zero = jnp.zeros_like(cos)
    return (jnp.concatenate([cos, cos, zero, zero], axis=-1),
            jnp.concatenate([-sin, sin, zero, zero], axis=-1))


def kernel(x, c, positions, ev_ada_w, ev_ada_b, ev_norm_pre, ev_norm_post, ev_w_in, ev_mu, ev_w0, ev_w2, ev_a0, ev_a2, ev_k_k, ev_k_a, ev_r_k, ev_lnx_g, ev_lnx_b, ev_lam_re, ev_lam_im, ev_log_dt, ev_b_re, ev_b_im, ev_c_re, ev_c_im, ev_d, ev_glu_w, ev_glu_b, ev_w_out, od_ada_w, od_ada_b, od_norm_pre, od_norm_post, od_w_in, od_q_norm, od_w_q_up, od_kv_norm, od_w_kv_up, od_w_out):
    bsz, seq, d = x.shape
    n_even, n_odd = ev_w_in.shape[0], od_w_in.shape[0]
    depth = n_even + n_odd
    rw = ev_w0.shape[-1]
    sw = ev_d.shape[-1]
    a_in = 4 * rw + 2 * RWKV_LORA
    h = x.astype(F32)

    c8 = jnp.zeros((ADA_ROWS, d), F32).at[:bsz].set(c.astype(F32))
    ev_ada = _ada_call(c8, ev_ada_w, ev_ada_b)
    od_ada = _ada_call(c8, od_ada_w, od_ada_b)
    cos, sin = _rope_tables(positions)

    ev_w_in_b = ev_w_in.astype(BF16)
    ev_in = ev_w_in.shape[2]
    ev_segments = ((0, 0, 4 * rw), (4 * rw, a_in, ev_in - a_in), (4 * rw + ev_in - a_in, 4 * rw, a_in - 4 * rw))
    ev_w_out_b, ev_glu_w_b = ev_w_out.astype(BF16), ev_glu_w.astype(BF16)
    s5_tables = jax.vmap(functools.partial(_s5_tables, cs=S5_CHUNK))(
        ev_lam_re, ev_lam_im, ev_log_dt, ev_b_re, ev_b_im, ev_c_re, ev_c_im)
    o2, o3 = Q_LORA + KV_LORA, Q_LORA + KV_LORA + QK_ROPE
    mw = od_w_in.shape[2] - o3
    od_w_in_p = jnp.concatenate([od_w_in[:, :, o3:], od_w_in[:, :, :o3],
                                 jnp.zeros((n_odd, d, LANES - QK_ROPE), od_w_in.dtype)],
                                axis=2).astype(BF16)
    nh = MLA_HEADS
    wq = od_w_q_up.reshape(n_odd, Q_LORA, nh, QK_NOPE + QK_ROPE)
    wq = jnp.pad(wq, ((0, 0), (0, 0), (0, 0), (0, 2 * LANES - QK_NOPE - QK_ROPE)))
    wq = wq.reshape(n_odd, Q_LORA, nh * 2 * LANES).astype(BF16)
    wkv = od_w_kv_up.astype(BF16)
    od_w_out_b = od_w_out.astype(BF16)

    rows = lambda a: a.reshape(a.shape[0], 1, a.shape[-1])
    rwkv_params = [ev_mu[:, :4 * rw].reshape(n_even, 4, rw), rows(ev_mu[:, 4 * rw:]),
                   rows(ev_w0), ev_w2.astype(BF16), rows(ev_a0), ev_a2.astype(BF16),
                   rows(ev_k_k), rows(ev_k_a), rows(ev_r_k), rows(ev_lnx_g), rows(ev_lnx_b)]
    ev_pre, ev_post, od_pre, od_post = (rows(a) for a in (ev_norm_pre, ev_norm_post,
                                                          od_norm_pre, od_norm_post))
    ev_d_r, ev_glu_b_r, od_qn, od_kvn = (rows(a) for a in (ev_d, ev_glu_b, od_q_norm, od_kv_norm))

    tm_out = min(512, seq)
    for i in range(depth):
        j = i // 2
        if i % 2 == 0:
            p = _norm_proj_call(h, ev_pre, ev_ada, ev_w_in_b, j, ev_segments,
                                tm=min(256, seq), tn=PROJ_TN)
            y_a = _rwkv_call(p, rwkv_params, j, tb=min(RWKV_TB, seq), ch=RWKV_CHUNK)
            y_b = _s5_call(p, s5_tables, j, ev_d_r, ev_glu_w_b, ev_glu_b_r,
                           u_blk=(4 * rw) // sw, gate_blk=(4 * rw) // sw + 1,
                           ts=min(S5_TS, seq), cs=S5_CHUNK)
            half = pl.BlockSpec((1, tm_out, rw), lambda b, t_: (b, t_, 0))
            h = _out_call(_out_even_kernel, [y_a, y_b], [half, half],
                          ev_w_out_b, j, ev_post, ev_ada, h, tm_out)
        else:
            p = _norm_proj_call(h, od_pre, od_ada, od_w_in_p, j, ((0, 0, od_w_in_p.shape[2]),),
                                tm=min(512, seq), tn=PROJ_TN)
            q, k, v = _qkv_call(p, cos, sin, od_qn, od_kvn, wq, wkv, j,
                                cq_blk=mw // Q_LORA, ckv_blk=mw // KV_LORA + 1,
                                kpe_blk=(mw + o2) // LANES, tm=min(512, seq))
            o = _attn_call(q, k, v, tq=min(2048, seq), tk=min(512, seq))
            full = pl.BlockSpec((1, tm_out, mw), lambda b, t_: (b, t_, 0))
            h = _out_call(_out_odd_kernel, [o, p], [full, full],
                          od_w_out_b, j, od_post, od_ada, h, tm_out)
    return h.astype(x.dtype)
```

```python
import functools
import math

import jax
import jax.numpy as jnp
from jax import lax
from jax.experimental import pallas as pl
from jax.experimental.pallas import tpu as pltpu

F32 = jnp.float32
BF16 = jnp.bfloat16

NORM_EPS = 1e-6
LNX_EPS = 64e-5
ROPE_BASE = 10000.0

RWKV_HEAD = 64
RWKV_LORA = 64
S5_GROUP = 16
S5_STATE = 64
MLA_HEADS = 16
QK_NOPE = 128
QK_ROPE = 64
V_HEAD = 128
Q_LORA = 512
KV_LORA = 512

LANES = 128
V7X_VMEM_LIMIT = 56 * 1024 * 1024

RWKV_CHUNK = 64
RWKV_TB = 256
S5_CHUNK = 16
S5_TS = 256
S5_GT = 16
PROJ_TN = 1024


def _params(sem, vmem=V7X_VMEM_LIMIT):
    return pltpu.CompilerParams(dimension_semantics=sem, vmem_limit_bytes=vmem)


def _sigmoid(x):
    return 1.0 / (1.0 + jnp.exp(-x))


def _silu(x):
    return x * _sigmoid(x)


def _dot(a, b):
    return jnp.dot(a, b, preferred_element_type=F32)


def _dot_nt(a, b):
    return lax.dot_general(a, b, (((1,), (1,)), ((), ())), preferred_element_type=F32)


def _dot_tn(a, b):
    return lax.dot_general(a, b, (((0,), (0,)), ((), ())), preferred_element_type=F32)


def _resident(shape):
    nd = len(shape)
    return pl.BlockSpec(shape, lambda *_: (0,) * nd, pipeline_mode=pl.Buffered(1))


def _layer_resident(stacked, layer):
    nd = stacked.ndim
    return pl.BlockSpec((None,) + stacked.shape[1:], lambda *_: (layer,) + (0,) * (nd - 1),
                        pipeline_mode=pl.Buffered(1))


ADA_PARTS = 3
ADA_ROWS = 8


def _ada_kernel(c_ref, w_ref, b_ref, o_ref):
    s = _silu(c_ref[...]).astype(BF16)
    o_ref[0, 0, :, 0, :] = _dot(s, w_ref[0].astype(BF16)) + b_ref[0]


def _ada_call(c8, w, b):
    nl, d, n3 = w.shape
    tn = d // 2
    per = d // tn
    return pl.pallas_call(
        _ada_kernel,
        grid=(nl, n3 // tn),
        in_specs=[pl.BlockSpec((ADA_ROWS, d), lambda l, n: (0, 0)),
                  pl.BlockSpec((1, d, tn), lambda l, n: (l, 0, n)),
                  pl.BlockSpec((1, 1, tn), lambda l, n: (l, 0, n))],
        out_specs=pl.BlockSpec((1, 1, ADA_ROWS, 1, tn), lambda l, n: (l, n // per, 0, 0, n % per)),
        out_shape=jax.ShapeDtypeStruct((nl, ADA_PARTS, ADA_ROWS, 1, d), F32),
        compiler_params=_params(("parallel", "parallel")),
        name="ada",
    )(c8, w, b.reshape(nl, 1, n3))


def _ada_row(ada, layer, part):
    d = ada.shape[-1]
    return pl.BlockSpec((None, None, None, 1, d), lambda b, *_: (layer, part, b, 0, 0))


def _norm_proj_kernel(h_ref, g_ref, sc_ref, sh_ref, w_ref, o_ref, *, tn, segments):
    x = h_ref[0]
    ms = jnp.mean(x * x, axis=-1, keepdims=True)
    z = (x * lax.rsqrt(ms + NORM_EPS) * g_ref[...]) * (1.0 + sc_ref[...]) + sh_ref[...]
    z = z.astype(BF16)
    for out0, w0, width in segments:
        for c0 in range(0, width, tn):
            cw = min(tn, width - c0)
            o_ref[0, :, out0 + c0:out0 + c0 + cw] = _dot(
                z, w_ref[:, w0 + c0:w0 + c0 + cw]).astype(o_ref.dtype)


def _norm_proj_call(h, g, ada, w, layer, segments, tm, tn):
    bsz, seq, d = h.shape
    n = w.shape[2]
    assert sum(width for _, _, width in segments) == n
    return pl.pallas_call(
        functools.partial(_norm_proj_kernel, tn=tn, segments=segments),
        grid=(bsz, seq // tm),
        in_specs=[pl.BlockSpec((1, tm, d), lambda b, t: (b, t, 0)),
                  _layer_resident(g, layer),
                  _ada_row(ada, layer, 1),
                  _ada_row(ada, layer, 0),
                  _layer_resident(w, layer)],
        out_specs=pl.BlockSpec((1, tm, n), lambda b, t: (b, t, 0)),
        out_shape=jax.ShapeDtypeStruct((bsz, seq, n), BF16),
        compiler_params=_params(("parallel", "parallel")),
        name="norm_proj",
    )(h, g, ada, ada, w)


def _finish(y, post_ref, gate_ref, h_ref, o_ref):
    ms = jnp.mean(y * y, axis=-1, keepdims=True)
    yn = y * lax.rsqrt(ms + NORM_EPS) * post_ref[...]
    o_ref[0] = h_ref[0] + gate_ref[...] * yn


def _out_even_kernel(ya_ref, yb_ref, w_ref, post_ref, gate_ref, h_ref, o_ref):
    half = ya_ref.shape[2]
    y = _dot(ya_ref[0], w_ref[:half, :]) + _dot(yb_ref[0], w_ref[half:, :])
    _finish(y, post_ref, gate_ref, h_ref, o_ref)


def _out_odd_kernel(o_in_ref, g_ref, w_ref, post_ref, gate_ref, h_ref, o_ref):
    yin = o_in_ref[0].astype(F32) * _silu(g_ref[0].astype(F32))
    y = _dot(yin.astype(BF16), w_ref[...])
    _finish(y, post_ref, gate_ref, h_ref, o_ref)


def _out_call(kernel, acts, act_specs, w, layer, post, ada, h, tm):
    bsz, seq, d = h.shape
    return pl.pallas_call(
        kernel,
        grid=(bsz, seq // tm),
        in_specs=act_specs + [
            _layer_resident(w, layer),
            _layer_resident(post, layer),
            _ada_row(ada, layer, 2),
            pl.BlockSpec((1, tm, d), lambda b, t: (b, t, 0))],
        out_specs=pl.BlockSpec((1, tm, d), lambda b, t: (b, t, 0)),
        out_shape=jax.ShapeDtypeStruct((bsz, seq, d), F32),
        compiler_params=_params(("parallel", "parallel")),
        name=kernel.__name__.strip("_"),
    )(*acts, w, post, ada, h)


def _rwkv_kernel(r_ref, k_ref, v_ref, g_ref, lo_ref,
                 mu_ref, mulo_ref, w0_ref, w2_ref, a0_ref, a2_ref, kk_ref, ka_ref, rk_ref,
                 lng_ref, lnb_ref, hsum_ref, hexp_ref,
                 o_ref,
                 carry_ref, carrylo_ref, state_ref, y_ref, *bufs, tb, ch):
    width = r_ref.shape[2]
    npair = width // LANES
    nch = tb // ch
    c2 = 2 * ch
    nsq = int(math.log2(ch)) - 1
    step_id = pl.program_id(1)
    set_a, set_b = bufs[:len(bufs) // 2], bufs[len(bufs) // 2:]

    @pl.when(step_id == 0)
    def _():
        carry_ref[...] = jnp.zeros_like(carry_ref)
        carrylo_ref[...] = jnp.zeros_like(carrylo_ref)
        state_ref[...] = jnp.zeros_like(state_ref)

    def run(wr, rd):
        rx_w, remx_w, repd_w, kd_w, repe_w, ke_w, vv_w, pc_w, bonus_w, sg_w = wr
        rx_r, remx_r, repd_r, kd_r, repe_r, ke_r, vv_r, pc_r, bonus_r, sg_r = rd or wr
        cat = jnp.concatenate

        def headsum(t):
            sums = _dot(t.astype(BF16), hsum_ref[...])
            hi = sums.astype(BF16)
            lo = (sums - hi.astype(F32)).astype(BF16)
            return _dot(cat([hi, lo], axis=1), hexp_ref[...])

        pi = lax.broadcasted_iota(jnp.int32, (c2, c2), 0)
        pj = lax.broadcasted_iota(jnp.int32, (c2, c2), 1)
        blk = (pi // ch) == (pj // ch)
        strict = blk & (pj < pi)
        incl = blk & (pj <= pi)
        m0 = lax.broadcasted_iota(jnp.int32, (ch, LANES), 1) < RWKV_HEAD

        def by_head(t):
            z = jnp.zeros_like(t)
            return cat([jnp.where(m0, t, z), jnp.where(m0, z, t)], axis=0)

        def chunk(c):
            rows = slice(c * ch, (c + 1) * ch)
            pairs = range(npair)
            cols = [slice(p * LANES, (p + 1) * LANES) for p in pairs]
            lhs = [cat([by_head(remx_r[rows, cols[p]]), by_head(rx_r[rows, cols[p]])], axis=0)
                   for p in pairs]
            m1 = [_dot_nt(lhs[p], cat([repd_r[rows, cols[p]]] * 2 + [kd_r[rows, cols[p]]] * 2, axis=0))
                  for p in pairs]
            s_old = [state_ref[p] for p in pairs]
            gs = [_dot_nt(lhs[p], s_old[p].astype(BF16)) for p in pairs]
            v_bd = [by_head(vv_r[rows, cols[p]]) for p in pairs]
            sa = [gs[p][:c2] + _dot(jnp.where(strict, m1[p][:c2, c2:], 0.0).astype(BF16), v_bd[p])
                  for p in pairs]
            pw = [jnp.where(strict, m1[p][:c2, :c2], 0.0) for p in pairs]
            for _i in range(nsq):
                both = [_dot(pw[p].astype(BF16), cat([pw[p], sa[p]], axis=1).astype(BF16)) for p in pairs]
                pw = [both[p][:, :c2] for p in pairs]
                sa = [sa[p] + both[p][:, c2:] for p in pairs]
            sa = [(sa[p] + _dot(pw[p].astype(BF16), sa[p].astype(BF16))).astype(BF16) for p in pairs]
            vals = [cat([sa[p], v_bd[p]], axis=0) for p in pairs]
            for p in pairs:
                arbr = cat([jnp.where(incl, m1[p][c2:, :c2], 0.0), jnp.where(incl, m1[p][c2:, c2:], 0.0)],
                           axis=1).astype(BF16)
                y_bd = gs[p][c2:] + _dot(arbr, vals[p])
                y_ref[rows, cols[p]] = y_bd[:ch] + y_bd[ch:]
            for p in pairs:
                upd = _dot_tn(vals[p], cat([by_head(repe_r[rows, cols[p]]), by_head(ke_r[rows, cols[p]])],
                                           axis=0))
                state_ref[p] = s_old[p] * pc_r[8 * c:8 * c + 1, cols[p]] + upd

        row0 = lax.broadcasted_iota(jnp.int32, (tb, 1), 0) == 0
        env = {}

        def shift(x, cref, slot, mu):
            prev = jnp.where(row0, cref[slot:slot + 1, :], pltpu.roll(x, 1, axis=0))
            cref[slot:slot + 1, :] = x[tb - 1:tb, :]
            return x + (prev - x) * mu

        def stage_shift():
            env["r"] = shift(r_ref[0].astype(F32), carry_ref, 0, mu_ref[0:1, :])
            env["k"] = shift(k_ref[0].astype(F32), carry_ref, 1, mu_ref[1:2, :])
            v = shift(v_ref[0].astype(F32), carry_ref, 2, mu_ref[2:3, :])
            g = shift(g_ref[0].astype(F32), carry_ref, 3, mu_ref[3:4, :])
            lo = shift(lo_ref[0].astype(F32), carrylo_ref, 0, mulo_ref[...])
            env["v"] = v
            vv_w[...] = v.astype(BF16)
            sg_w[...] = _silu(g)
            x = w0_ref[...] + _dot(jnp.tanh(lo[:, :RWKV_LORA]).astype(BF16), w2_ref[...])
            y = -x
            softplus = jnp.maximum(y, 0.0) + jnp.log(1.0 + jnp.exp(-jnp.abs(y)))
            env["ld"] = -jnp.exp(-softplus - 0.5)
            env["a"] = _sigmoid(a0_ref[...] + _dot(lo[:, RWKV_LORA:].astype(BF16), a2_ref[...]))

        def stage_keys():
            r, k, v, a = env["r"], env["k"], env["v"], env["a"]
            kk = k * kk_ref[...]
            k2 = k * (1.0 + (a - 1.0) * ka_ref[...])
            sums = headsum(cat([kk * kk, r * k2 * rk_ref[...]], axis=0))
            kk = kk * lax.rsqrt(jnp.maximum(sums[:tb], 1e-24))
            bonus_w[...] = sums[tb:] * v
            env["kk"], env["k2"], env["rep"] = kk, k2, kk * a

        def stage_decay():
            ld = env["ld"]
            ri = lax.broadcasted_iota(jnp.int32, (tb, tb), 0)
            ci = lax.broadcasted_iota(jnp.int32, (tb, tb), 1)
            lincl = jnp.where(((ri // ch) == (ci // ch)) & (ci <= ri), 1.0, 0.0).astype(BF16)
            hi = ld.astype(BF16)
            r1 = ld - hi.astype(F32)
            mid = r1.astype(BF16)
            low = (r1 - mid.astype(F32)).astype(BF16)
            parts = _dot(lincl, cat([hi, mid, low], axis=1))
            env["cum"] = parts[:, :width] + parts[:, width:2 * width] + parts[:, 2 * width:]

        def stage_operands():
            r, kk, k2, rep, ld, cum = (env[n] for n in ("r", "kk", "k2", "rep", "ld", "cum"))
            pinv = jnp.exp(-cum)
            pend = []
            for s in range(nch):
                pc = jnp.exp(cum[(s + 1) * ch - 1:(s + 1) * ch, :])
                pc_w[8 * s:8 * s + 8, :] = jnp.broadcast_to(pc, (8, width))
                pend.append(pc * pinv[s * ch:(s + 1) * ch, :])
            pend = cat(pend, axis=0)
            rx_w[...] = (r * jnp.exp(cum)).astype(BF16)
            remx_w[...] = (-kk * jnp.exp(cum - ld)).astype(BF16)
            repd_w[...] = (rep * pinv).astype(BF16)
            kd_w[...] = (k2 * pinv).astype(BF16)
            repe_w[...] = (rep * pend).astype(BF16)
            ke_w[...] = (k2 * pend).astype(BF16)

        stages = [stage_shift, stage_keys, stage_decay, stage_operands]
        if rd is None:
            for stage in stages:
                stage()
            return
        for c in range(nch):
            chunk(c)
            if c < len(stages):
                stages[c]()
        for stage in stages[nch:]:
            stage()

        yv = y_ref[...]
        inv_n = 1.0 / RWKV_HEAD
        mean = headsum(yv) * inv_n
        dlt = yv - mean
        var = headsum(dlt * dlt) * inv_n
        yn = dlt * lax.rsqrt(var + LNX_EPS) * lng_ref[...] + lnb_ref[...]
        o_ref[0] = ((yn + bonus_r[...]) * sg_r[...]).astype(o_ref.dtype)

    @pl.when(step_id == 0)
    def _():
        run(set_a, None)

    @pl.when((step_id > 0) & (step_id % 2 == 0))
    def _():
        run(set_a, set_b)

    @pl.when(step_id % 2 == 1)
    def _():
        run(set_b, set_a)


def _rwkv_call(p, params, layer, tb, ch):
    bsz, seq, _ = p.shape
    width = params[2].shape[-1]
    head_of = jnp.arange(width) // RWKV_HEAD
    hsum = (head_of[:, None] == jnp.arange(LANES)[None, :]).astype(BF16)
    hexp = jnp.concatenate([hsum.T, hsum.T], axis=0)
    npair = width // LANES
    nblk = seq // tb
    lo_blk = p.shape[-1] // LANES - 1
    act = lambda j: pl.BlockSpec((1, tb, width), lambda b, s, j=j: (b, jnp.minimum(s, nblk - 1), j))
    bf = lambda: pltpu.VMEM((tb, width), BF16)
    f32 = lambda: pltpu.VMEM((tb, width), F32)
    buf_set = lambda: [bf(), bf(), bf(), bf(), bf(), bf(), bf(),
                       pltpu.VMEM((8 * (tb // ch), width), F32), f32(), f32()]
    return pl.pallas_call(
        functools.partial(_rwkv_kernel, tb=tb, ch=ch),
        grid=(bsz, nblk + 1),
        in_specs=[act(0), act(1), act(2), act(3),
                  pl.BlockSpec((1, tb, LANES), lambda b, s: (b, jnp.minimum(s, nblk - 1), lo_blk))]
                 + [_layer_resident(a, layer) for a in params]
                 + [_resident(hsum.shape), _resident(hexp.shape)],
        out_specs=pl.BlockSpec((1, tb, width), lambda b, s: (b, jnp.maximum(s - 1, 0), 0)),
        out_shape=jax.ShapeDtypeStruct((bsz, seq, width), BF16),
        scratch_shapes=[pltpu.VMEM((8, width), F32), pltpu.VMEM((8, LANES), F32),
                        pltpu.VMEM((npair, LANES, LANES), F32), f32()] + buf_set() + buf_set(),
        compiler_params=_params(("parallel", "arbitrary")),
        name="rwkv",
    )(p, p, p, p, p, *params, hsum, hexp)


def _s5_kernel(u_ref, gate_ref, bblk_ref, cblk_ref, dnr_ref, dni_ref, dpr_ref, dpi_ref,
               lre_ref, lim_ref, d_ref, gw_ref, gb_ref, o_ref, xr_ref, xi_ref, y_ref, *, ts, cs):
    @pl.when(pl.program_id(1) == 0)
    def _():
        xr_ref[...] = jnp.zeros_like(xr_ref)
        xi_ref[...] = jnp.zeros_like(xi_ref)

    ntile, ucols, scols2 = bblk_ref.shape
    scols = scols2 // 2
    nsub = ts // cs
    u_bf = u_ref[0]
    ri = lax.broadcasted_iota(jnp.int32, (ts, ts), 0)
    ci = lax.broadcasted_iota(jnp.int32, (ts, ts), 1)
    ltri = jnp.where(((ri // cs) == (ci // cs)) & (ci <= ri), 1.0, 0.0).astype(BF16)

    tiles = range(ntile)
    cols = [slice(t * scols, (t + 1) * scols) for t in tiles]
    cat = jnp.concatenate

    bu = [_dot(u_bf[:, t * ucols:(t + 1) * ucols], bblk_ref[t]) for t in tiles]

    def scaled_inputs(t):
        dnr, dni = dnr_ref[:, cols[t]], dni_ref[:, cols[t]]
        zr, zi = [], []
        for s in range(nsub):
            br = bu[t][s * cs:(s + 1) * cs, :scols]
            bi = bu[t][s * cs:(s + 1) * cs, scols:]
            zr.append(br * dnr - bi * dni)
            zi.append(br * dni + bi * dnr)
        return cat([cat(zr, axis=0), cat(zi, axis=0)], axis=1).astype(BF16)

    csum = [_dot(ltri, scaled_inputs(t)) for t in tiles]

    def states(t):
        sc = cols[t]
        dpr, dpi = dpr_ref[:, sc], dpi_ref[:, sc]
        lre, lim = lre_ref[:, sc], lim_ref[:, sc]
        pr, pi_ = xr_ref[0:1, sc], xi_ref[0:1, sc]
        xr, xi = [], []
        for s in range(nsub):
            ar = lre * pr - lim * pi_
            ai = lre * pi_ + lim * pr
            cr = csum[t][s * cs:(s + 1) * cs, :scols] + ar
            cim = csum[t][s * cs:(s + 1) * cs, scols:] + ai
            xs_r = cr * dpr - cim * dpi
            xs_i = cr * dpi + cim * dpr
            pr, pi_ = xs_r[cs - 1:cs, :], xs_i[cs - 1:cs, :]
            xr.append(xs_r)
            xi.append(xs_i)
        xr_ref[0:1, sc] = pr
        xi_ref[0:1, sc] = pi_
        return cat([cat(xr, axis=0), cat(xi, axis=0)], axis=1).astype(BF16)

    for t in tiles:
        y_ref[:, t * ucols:(t + 1) * ucols] = _dot(states(t), cblk_ref[t])

    y = y_ref[...] + d_ref[...] * u_bf.astype(F32)
    cdf = 0.5 * (1.0 + jnp.tanh(math.sqrt(2.0 / math.pi) * (y + 0.044715 * (y * y * y))))
    y = y * cdf
    glu = _dot(y.astype(BF16), gw_ref[...]) + gb_ref[...]
    o_ref[0] = (y * _sigmoid(glu) * _silu(gate_ref[0].astype(F32))).astype(o_ref.dtype)


def _s5_tables(lam_re, lam_im, log_dt, b_re, b_im, c_re, c_im, cs):
    g, n = lam_re.shape
    pch = b_re.shape[-1]
    nt = g // S5_GT
    dt = jnp.exp(log_dt)[:, None]
    mag = jnp.exp(lam_re * dt)
    e_re, e_im = mag * jnp.cos(lam_im * dt), mag * jnp.sin(lam_im * dt)
    den = lam_re * lam_re + lam_im * lam_im
    coef_re = ((e_re - 1.0) * lam_re + e_im * lam_im) / den
    coef_im = (e_im * lam_re - (e_re - 1.0) * lam_im) / den
    bb_re = coef_re[..., None] * b_re - coef_im[..., None] * b_im
    bb_im = coef_re[..., None] * b_im + coef_im[..., None] * b_re
    eye = jnp.eye(S5_GT, dtype=F32)

    def btile(bb):
        return jnp.einsum('tgnq,gh->tgqhn', bb.reshape(nt, S5_GT, n, pch), eye).reshape(
            nt, S5_GT * pch, S5_GT * n)

    def ctile(cc):
        return jnp.einsum('tgpn,gh->tgnhp', cc.reshape(nt, S5_GT, pch, n), eye).reshape(
            nt, S5_GT * n, S5_GT * pch)

    bblk = jnp.concatenate([btile(bb_re), btile(bb_im)], axis=-1).astype(BF16)
    cblk = jnp.concatenate([ctile(c_re), -ctile(c_im)], axis=1).astype(BF16)
    j = jnp.arange(cs, dtype=F32)[:, None, None]
    lr, li = (lam_re * dt)[None], (lam_im * dt)[None]
    flat = lambda t: t.reshape(t.shape[0], g * n)
    dpr, dpi = flat(jnp.exp(j * lr) * jnp.cos(j * li)), flat(jnp.exp(j * lr) * jnp.sin(j * li))
    dnr, dni = flat(jnp.exp(-j * lr) * jnp.cos(j * li)), flat(-jnp.exp(-j * lr) * jnp.sin(j * li))
    return bblk, cblk, dnr, dni, dpr, dpi, e_re.reshape(1, g * n), e_im.reshape(1, g * n)


def _s5_call(p, tables, layer, d, glu_w, glu_b, u_blk, gate_blk, ts, cs):
    bsz, seq, _ = p.shape
    width = d.shape[-1]
    bblk, cblk, dnr, dni, dpr, dpi, lre, lim = tables
    nstate = lre.shape[-1]
    res = lambda a: _layer_resident(a, layer)
    return pl.pallas_call(
        functools.partial(_s5_kernel, ts=ts, cs=cs),
        grid=(bsz, seq // ts),
        in_specs=[pl.BlockSpec((1, ts, width), lambda b, t: (b, t, u_blk)),
                  pl.BlockSpec((1, ts, width), lambda b, t: (b, t, gate_blk)),
                  res(bblk), res(cblk), res(dnr), res(dni), res(dpr), res(dpi), res(lre), res(lim),
                  res(d), res(glu_w), res(glu_b)],
        out_specs=pl.BlockSpec((1, ts, width), lambda b, t: (b, t, 0)),
        out_shape=jax.ShapeDtypeStruct((bsz, seq, width), BF16),
        scratch_shapes=[pltpu.VMEM((8, nstate), F32), pltpu.VMEM((8, nstate), F32),
                        pltpu.VMEM((ts, width), F32)],
        compiler_params=_params(("parallel", "arbitrary")),
        name="s5",
    )(p, p, bblk, cblk, dnr, dni, dpr, dpi, lre, lim, d, glu_w, glu_b)


def _rope128(x, cos, sin):
    lane = lax.broadcasted_iota(jnp.int32, x.shape, 1)
    half = QK_ROPE // 2
    partner = jnp.where(lane < half, pltpu.roll(x, LANES - half, axis=1), pltpu.roll(x, half, axis=1))
    return x * cos + partner * sin


def _qkv_kernel(cq_ref, ckv_ref, kpe_ref, cos_ref, sin_ref, qn_ref, kvn_ref, wq_ref, wkv_ref,
                q_ref, k_ref, v_ref, *, scale):
    def latent(ref, gain_ref):
        x = ref[0].astype(F32)
        ms = jnp.mean(x * x, axis=-1, keepdims=True)
        return (x * lax.rsqrt(ms + NORM_EPS) * gain_ref[...]).astype(BF16)

    cq = latent(cq_ref, qn_ref)
    ckv = latent(ckv_ref, kvn_ref)
    cos, sin = cos_ref[0], sin_ref[0]
    kpe = _rope128(kpe_ref[0].astype(F32), cos, sin).astype(BF16)
    hw = 2 * LANES
    for h in range(q_ref.shape[1]):
        qh = _dot(cq, wq_ref[:, h * hw:(h + 1) * hw]) * scale
        q_ref[0, h] = jnp.concatenate(
            [qh[:, :LANES], _rope128(qh[:, LANES:], cos, sin)], axis=1).astype(BF16)
        kvh = _dot(ckv, wkv_ref[:, h * hw:(h + 1) * hw])
        k_ref[0, h] = jnp.concatenate([kvh[:, :LANES].astype(BF16), kpe], axis=1)
        v_ref[0, h] = kvh[:, LANES:].astype(BF16)


def _qkv_call(p, cos, sin, q_norm, kv_norm, wq, wkv, layer, cq_blk, ckv_blk, kpe_blk, tm):
    bsz, seq, _ = p.shape
    nh = MLA_HEADS
    hw = 2 * LANES
    scale = math.log2(math.e) / math.sqrt(QK_NOPE + QK_ROPE)
    head_out = lambda w: pl.BlockSpec((1, nh, tm, w), lambda b, t: (b, 0, t, 0))
    return pl.pallas_call(
        functools.partial(_qkv_kernel, scale=scale),
        grid=(bsz, seq // tm),
        in_specs=[pl.BlockSpec((1, tm, Q_LORA), lambda b, t: (b, t, cq_blk)),
                  pl.BlockSpec((1, tm, KV_LORA), lambda b, t: (b, t, ckv_blk)),
                  pl.BlockSpec((1, tm, LANES), lambda b, t: (b, t, kpe_blk)),
                  pl.BlockSpec((1, tm, LANES), lambda b, t: (b, t, 0)),
                  pl.BlockSpec((1, tm, LANES), lambda b, t: (b, t, 0)),
                  _layer_resident(q_norm, layer), _layer_resident(kv_norm, layer),
                  _layer_resident(wq, layer), _layer_resident(wkv, layer)],
        out_specs=[head_out(hw), head_out(hw), head_out(V_HEAD)],
        out_shape=[jax.ShapeDtypeStruct((bsz, nh, seq, hw), BF16),
                   jax.ShapeDtypeStruct((bsz, nh, seq, hw), BF16),
                   jax.ShapeDtypeStruct((bsz, nh, seq, V_HEAD), BF16)],
        compiler_params=_params(("parallel", "parallel")),
        name="qkv",
    )(p, p, p, cos, sin, q_norm, kv_norm, wq, wkv)


def _attn_kernel(q_ref, k_ref, v_ref, o_ref, *, tq, tk):
    qi = pl.program_id(2)
    nsub = tq // tk
    vd = v_ref.shape[3]
    qs = [q_ref[0, 0, i * tk:(i + 1) * tk, :] for i in range(nsub)]
    diag = (lax.broadcasted_iota(jnp.int32, (tk, tk), 0)
            >= lax.broadcasted_iota(jnp.int32, (tk, tk), 1))

    def scores(j, kinds):
        rows = pl.ds(pl.multiple_of(j * tk, tk), tk)
        kb = k_ref[0, 0, rows, :]
        return [None if kinds[i] is None else _dot_nt(qs[i], kb) for i in range(nsub)]

    def update(j, carries, kinds, ss):
        rows = pl.ds(pl.multiple_of(j * tk, tk), tk)
        vb = jnp.concatenate([v_ref[0, 0, rows, :], jnp.ones((tk, vd), BF16)], axis=1)
        out = []
        for i in range(nsub):
            if kinds[i] is None:
                out.append(carries[i])
                continue
            m, acc = carries[i]
            s = jnp.where(diag, ss[i], -1e30) if kinds[i] else ss[i]
            m_new = jnp.maximum(m, jnp.max(s, axis=-1, keepdims=True))
            pexp = jnp.exp2(s - m_new).astype(BF16)
            acc = jnp.exp2(m - m_new) * acc + _dot(pexp, vb)
            out.append((m_new, acc))
        return tuple(out)

    def blocks(js, carries, kinds_list):
        ss = [scores(j, kinds) for j, kinds in zip(js, kinds_list)]
        for j, kinds, s in zip(js, kinds_list, ss):
            carries = update(j, carries, kinds, s)
        return carries

    init = tuple((jnp.full((tk, 1), -1e30, F32), jnp.zeros((tk, 2 * vd), F32)) for _ in range(nsub))
    nfull = qi * nsub
    visible = [[False] * nsub] * nsub
    carries = lax.fori_loop(
        0, qi, lambda jq, c: blocks([jq * nsub + d for d in range(nsub)], c, visible), init)
    tail = [[None if i < d else (i == d) for i in range(nsub)] for d in range(nsub)]
    carries = blocks([nfull + d for d in range(nsub)], carries, tail)
    for i in range(nsub):
        _, acc = carries[i]
        o_ref[0, i * tk:(i + 1) * tk, :] = (acc[:, :vd] / acc[:, vd:]).astype(o_ref.dtype)


def _attn_call(q, k, v, tq, tk):
    bsz, nh, seq, hw = q.shape
    vd = v.shape[3]
    return pl.pallas_call(
        functools.partial(_attn_kernel, tq=tq, tk=tk),
        grid=(bsz, nh, seq // tq),
        in_specs=[pl.BlockSpec((1, 1, tq, hw), lambda b, h, i: (b, h, i, 0)),
                  pl.BlockSpec((1, 1, seq, hw), lambda b, h, i: (b, h, 0, 0)),
                  pl.BlockSpec((1, 1, seq, vd), lambda b, h, i: (b, h, 0, 0))],
        out_specs=pl.BlockSpec((1, tq, vd), lambda b, h, i: (b, i, h)),
        out_shape=jax.ShapeDtypeStruct((bsz, seq, nh * vd), BF16),
        compiler_params=_params(("parallel", "parallel", "arbitrary")),
        name="attn",
    )(q, k, v)


def _rope_tables(positions):
    inv_freq = 1.0 / (ROPE_BASE ** (jnp.arange(0, QK_ROPE, 2, dtype=F32) / QK_ROPE))
    ang = positions.astype(F32)[..., None] * inv_freq
    cos, sin = jnp.cos(ang), jnp.sin(ang)
    zero = jnp.zeros_like(cos)
    return (jnp.concatenate([cos, cos, zero, zero], axis=-1),
            jnp.concatenate([-sin, sin, zero, zero], axis=-1))


def kernel(x, c, positions, ev_ada_w, ev_ada_b, ev_norm_pre, ev_norm_post, ev_w_in, ev_mu, ev_w0, ev_w2, ev_a0, ev_a2, ev_k_k, ev_k_a, ev_r_k, ev_lnx_g, ev_lnx_b, ev_lam_re, ev_lam_im, ev_log_dt, ev_b_re, ev_b_im, ev_c_re, ev_c_im, ev_d, ev_glu_w, ev_glu_b, ev_w_out, od_ada_w, od_ada_b, od_norm_pre, od_norm_post, od_w_in, od_q_norm, od_w_q_up, od_kv_norm, od_w_kv_up, od_w_out):
    bsz, seq, d = x.shape
    n_even, n_odd = ev_w_in.shape[0], od_w_in.shape[0]
    depth = n_even + n_odd
    rw = ev_w0.shape[-1]
    sw = ev_d.shape[-1]
    a_in = 4 * rw + 2 * RWKV_LORA
    h = x.astype(F32)

    c8 = jnp.zeros((ADA_ROWS, d), F32).at[:bsz].set(c.astype(F32))
    ev_ada = _ada_call(c8, ev_ada_w, ev_ada_b)
    od_ada = _ada_call(c8, od_ada_w, od_ada_b)
    cos, sin = _rope_tables(positions)

    ev_w_in_b = ev_w_in.astype(BF16)
    ev_in = ev_w_in.shape[2]
    ev_segments = ((0, 0, 4 * rw), (4 * rw, a_in, ev_in - a_in), (4 * rw + ev_in - a_in, 4 * rw, a_in - 4 * rw))
    ev_w_out_b, ev_glu_w_b = ev_w_out.astype(BF16), ev_glu_w.astype(BF16)
    s5_tables = jax.vmap(functools.partial(_s5_tables, cs=S5_CHUNK))(
        ev_lam_re, ev_lam_im, ev_log_dt, ev_b_re, ev_b_im, ev_c_re, ev_c_im)
    o2, o3 = Q_LORA + KV_LORA, Q_LORA + KV_LORA + QK_ROPE
    mw = od_w_in.shape[2] - o3
    od_w_in_p = jnp.concatenate([od_w_in[:, :, o3:], od_w_in[:, :, :o3],
                                 jnp.zeros((n_odd, d, LANES - QK_ROPE), od_w_in.dtype)],
                                axis=2).astype(BF16)
    nh = MLA_HEADS
    wq = od_w_q_up.reshape(n_odd, Q_LORA, nh, QK_NOPE + QK_ROPE)
    wq = jnp.pad(wq, ((0, 0), (0, 0), (0, 0), (0, 2 * LANES - QK_NOPE - QK_ROPE)))
    wq = wq.reshape(n_odd, Q_LORA, nh * 2 * LANES).astype(BF16)
    wkv = od_w_kv_up.astype(BF16)
    od_w_out_b = od_w_out.astype(BF16)

    rows = lambda a: a.reshape(a.shape[0], 1, a.shape[-1])
    rwkv_params = [ev_mu[:, :4 * rw].reshape(n_even, 4, rw), rows(ev_mu[:, 4 * rw:]),
                   rows(ev_w0), ev_w2.astype(BF16), rows(ev_a0), ev_a2.astype(BF16),
                   rows(ev_k_k), rows(ev_k_a), rows(ev_r_k), rows(ev_lnx_g), rows(ev_lnx_b)]
    ev_pre, ev_post, od_pre, od_post = (rows(a) for a in (ev_norm_pre, ev_norm_post,
                                                          od_norm_pre, od_norm_post))
    ev_d_r, ev_glu_b_r, od_qn, od_kvn = (rows(a) for a in (ev_d, ev_glu_b, od_q_norm, od_kv_norm))

    tm_out = min(512, seq)
    for i in range(depth):
        j = i // 2
        if i % 2 == 0:
            p = _norm_proj_call(h, ev_pre, ev_ada, ev_w_in_b, j, ev_segments,
                                tm=min(256, seq), tn=PROJ_TN)
            y_a = _rwkv_call(p, rwkv_params, j, tb=min(RWKV_TB, seq), ch=RWKV_CHUNK)
            y_b = _s5_call(p, s5_tables, j, ev_d_r, ev_glu_w_b, ev_glu_b_r,
                           u_blk=(4 * rw) // sw, gate_blk=(4 * rw) // sw + 1,
                           ts=min(S5_TS, seq), cs=S5_CHUNK)
            half = pl.BlockSpec((1, tm_out, rw), lambda b, t_: (b, t_, 0))
            h = _out_call(_out_even_kernel, [y_a, y_b], [half, half],
                          ev_w_out_b, j, ev_post, ev_ada, h, tm_out)
        else:
            p = _norm_proj_call(h, od_pre, od_ada, od_w_in_p, j, ((0, 0, od_w_in_p.shape[2]),),
                                tm=min(512, seq), tn=PROJ_TN)
            q, k, v = _qkv_call(p, cos, sin, od_qn, od_kvn, wq, wkv, j,
                                cq_blk=mw // Q_LORA, ckv_blk=mw // KV_LORA + 1,
                                kpe_blk=(mw + o2) // LANES, tm=min(512, seq))
            o = _attn_call(q, k, v, tq=min(2048, seq), tk=min(512, seq))
            full = pl.BlockSpec((1, tm_out, mw), lambda b, t_: (b, t_, 0))
            h = _out_call(_out_odd_kernel, [o, p], [full, full],
                          od_w_out_b, j, od_post, od_ada, h, tm_out)
    return h.astype(x.dtype)
```

```python
import functools
import math

import jax
import jax.numpy as jnp
from jax import lax
from jax.experimental import pallas as pl
from jax.experimental.pallas import tpu as pltpu

F32 = jnp.float32
BF16 = jnp.bfloat16

NORM_EPS = 1e-6
LNX_EPS = 64e-5
ROPE_BASE = 10000.0

RWKV_HEAD = 64
RWKV_LORA = 64
S5_GROUP = 16
S5_STATE = 64
MLA_HEADS = 16
QK_NOPE = 128
QK_ROPE = 64
V_HEAD = 128
Q_LORA = 512
KV_LORA = 512

LANES = 128
V7X_VMEM_LIMIT = 56 * 1024 * 1024

RWKV_CHUNK = 64
RWKV_TB = 256
S5_CHUNK = 16
S5_TS = 256
S5_GT = 16
PROJ_TN = 1024


def _params(sem, vmem=V7X_VMEM_LIMIT):
    return pltpu.CompilerParams(dimension_semantics=sem, vmem_limit_bytes=vmem)


def _sigmoid(x):
    return 1.0 / (1.0 + jnp.exp(-x))


def _silu(x):
    return x * _sigmoid(x)


def _dot(a, b):
    return jnp.dot(a, b, preferred_element_type=F32)


def _dot_nt(a, b):
    return lax.dot_general(a, b, (((1,), (1,)), ((), ())), preferred_element_type=F32)


def _dot_tn(a, b):
    return lax.dot_general(a, b, (((0,), (0,)), ((), ())), preferred_element_type=F32)


def _resident(shape):
    nd = len(shape)
    return pl.BlockSpec(shape, lambda *_: (0,) * nd, pipeline_mode=pl.Buffered(1))


def _layer_resident(stacked, layer):
    nd = stacked.ndim
    return pl.BlockSpec((None,) + stacked.shape[1:], lambda *_: (layer,) + (0,) * (nd - 1),
                        pipeline_mode=pl.Buffered(1))


ADA_PARTS = 3
ADA_ROWS = 8


def _ada_kernel(c_ref, w_ref, b_ref, o_ref):
    s = _silu(c_ref[...]).astype(BF16)
    o_ref[0, 0, :, 0, :] = _dot(s, w_ref[0].astype(BF16)) + b_ref[0]


def _ada_call(c8, w, b):
    nl, d, n3 = w.shape
    tn = d // 2
    per = d // tn
    return pl.pallas_call(
        _ada_kernel,
        grid=(nl, n3 // tn),
        in_specs=[pl.BlockSpec((ADA_ROWS, d), lambda l, n: (0, 0)),
                  pl.BlockSpec((1, d, tn), lambda l, n: (l, 0, n)),
                  pl.BlockSpec((1, 1, tn), lambda l, n: (l, 0, n))],
        out_specs=pl.BlockSpec((1, 1, ADA_ROWS, 1, tn), lambda l, n: (l, n // per, 0, 0, n % per)),
        out_shape=jax.ShapeDtypeStruct((nl, ADA_PARTS, ADA_ROWS, 1, d), F32),
        compiler_params=_params(("parallel", "parallel")),
        name="ada",
    )(c8, w, b.reshape(nl, 1, n3))


def _ada_row(ada, layer, part):
    d = ada.shape[-1]
    return pl.BlockSpec((None, None, None, 1, d), lambda b, *_: (layer, part, b, 0, 0))


def _norm_proj_kernel(h_ref, g_ref, sc_ref, sh_ref, w_ref, o_ref, *, tn, segments):
    x = h_ref[0]
    ms = jnp.mean(x * x, axis=-1, keepdims=True)
    z = (x * lax.rsqrt(ms + NORM_EPS) * g_ref[...]) * (1.0 + sc_ref[...]) + sh_ref[...]
    z = z.astype(BF16)
    for out0, w0, width in segments:
        for c0 in range(0, width, tn):
            cw = min(tn, width - c0)
            o_ref[0, :, out0 + c0:out0 + c0 + cw] = _dot(
                z, w_ref[:, w0 + c0:w0 + c0 + cw]).astype(o_ref.dtype)


def _norm_proj_call(h, g, ada, w, layer, segments, tm, tn):
    bsz, seq, d = h.shape
    n = w.shape[2]
    assert sum(width for _, _, width in segments) == n
    return pl.pallas_call(
        functools.partial(_norm_proj_kernel, tn=tn, segments=segments),
        grid=(bsz, seq // tm),
        in_specs=[pl.BlockSpec((1, tm, d), lambda b, t: (b, t, 0)),
                  _layer_resident(g, layer),
                  _ada_row(ada, layer, 1),
                  _ada_row(ada, layer, 0),
                  _layer_resident(w, layer)],
        out_specs=pl.BlockSpec((1, tm, n), lambda b, t: (b, t, 0)),
        out_shape=jax.ShapeDtypeStruct((bsz, seq, n), BF16),
        compiler_params=_params(("parallel", "parallel")),
        name="norm_proj",
    )(h, g, ada, ada, w)


def _finish(y, post_ref, gate_ref, h_ref, o_ref):
    ms = jnp.mean(y * y, axis=-1, keepdims=True)
    yn = y * lax.rsqrt(ms + NORM_EPS) * post_ref[...]
    o_ref[0] = h_ref[0] + gate_ref[...] * yn


def _out_even_kernel(ya_ref, yb_ref, w_ref, post_ref, gate_ref, h_ref, o_ref):
    half = ya_ref.shape[2]
    y = _dot(ya_ref[0], w_ref[:half, :]) + _dot(yb_ref[0], w_ref[half:, :])
    _finish(y, post_ref, gate_ref, h_ref, o_ref)


def _out_odd_kernel(o_in_ref, g_ref, w_ref, post_ref, gate_ref, h_ref, o_ref):
    yin = o_in_ref[0].astype(F32) * _silu(g_ref[0].astype(F32))
    y = _dot(yin.astype(BF16), w_ref[...])
    _finish(y, post_ref, gate_ref, h_ref, o_ref)


def _out_call(kernel, acts, act_specs, w, layer, post, ada, h, tm):
    bsz, seq, d = h.shape
    return pl.pallas_call(
        kernel,
        grid=(bsz, seq // tm),
        in_specs=act_specs + [
            _layer_resident(w, layer),
            _layer_resident(post, layer),
            _ada_row(ada, layer, 2),
            pl.BlockSpec((1, tm, d), lambda b, t: (b, t, 0))],
        out_specs=pl.BlockSpec((1, tm, d), lambda b, t: (b, t, 0)),
        out_shape=jax.ShapeDtypeStruct((bsz, seq, d), F32),
        compiler_params=_params(("parallel", "parallel")),
        name=kernel.__name__.strip("_"),
    )(*acts, w, post, ada, h)


def _rwkv_kernel(r_ref, k_ref, v_ref, g_ref, lo_ref,
                 mu_ref, mulo_ref, w0_ref, w2_ref, a0_ref, a2_ref, kk_ref, ka_ref, rk_ref,
                 lng_ref, lnb_ref, hsum_ref, hexp_ref,
                 o_ref,
                 carry_ref, carrylo_ref, state_ref, y_ref, *bufs, tb, ch):
    width = r_ref.shape[2]
    npair = width // LANES
    nch = tb // ch
    c2 = 2 * ch
    nsq = int(math.log2(ch)) - 1
    step_id = pl.program_id(1)
    set_a, set_b = bufs[:len(bufs) // 2], bufs[len(bufs) // 2:]

    @pl.when(step_id == 0)
    def _():
        carry_ref[...] = jnp.zeros_like(carry_ref)
        carrylo_ref[...] = jnp.zeros_like(carrylo_ref)
        state_ref[...] = jnp.zeros_like(state_ref)

    def run(wr, rd):
        rx_w, remx_w, repd_w, kd_w, repe_w, ke_w, vv_w, pc_w, bonus_w, sg_w = wr
        rx_r, remx_r, repd_r, kd_r, repe_r, ke_r, vv_r, pc_r, bonus_r, sg_r = rd or wr
        cat = jnp.concatenate

        def headsum(t):
            sums = _dot(t.astype(BF16), hsum_ref[...])
            hi = sums.astype(BF16)
            lo = (sums - hi.astype(F32)).astype(BF16)
            return _dot(cat([hi, lo], axis=1), hexp_ref[...])

        pi = lax.broadcasted_iota(jnp.int32, (c2, c2), 0)
        pj = lax.broadcasted_iota(jnp.int32, (c2, c2), 1)
        blk = (pi // ch) == (pj // ch)
        strict = blk & (pj < pi)
        incl = blk & (pj <= pi)
        m0 = lax.broadcasted_iota(jnp.int32, (ch, LANES), 1) < RWKV_HEAD

        def by_head(t):
            z = jnp.zeros_like(t)
            return cat([jnp.where(m0, t, z), jnp.where(m0, z, t)], axis=0)

        def chunk(c):
            rows = slice(c * ch, (c + 1) * ch)
            pairs = range(npair)
            cols = [slice(p * LANES, (p + 1) * LANES) for p in pairs]
            lhs = [cat([by_head(remx_r[rows, cols[p]]), by_head(rx_r[rows, cols[p]])], axis=0)
                   for p in pairs]
            m1 = [_dot_nt(lhs[p], cat([repd_r[rows, cols[p]]] * 2 + [kd_r[rows, cols[p]]] * 2, axis=0))
                  for p in pairs]
            s_old = [state_ref[p] for p in pairs]
            gs = [_dot_nt(lhs[p], s_old[p].astype(BF16)) for p in pairs]
            v_bd = [by_head(vv_r[rows, cols[p]]) for p in pairs]
            sa = [gs[p][:c2] + _dot(jnp.where(strict, m1[p][:c2, c2:], 0.0).astype(BF16), v_bd[p])
                  for p in pairs]
            pw = [jnp.where(strict, m1[p][:c2, :c2], 0.0) for p in pairs]
            for _i in range(nsq):
                both = [_dot(pw[p].astype(BF16), cat([pw[p], sa[p]], axis=1).astype(BF16)) for p in pairs]
                pw = [both[p][:, :c2] for p in pairs]
                sa = [sa[p] + both[p][:, c2:] for p in pairs]
            sa = [(sa[p] + _dot(pw[p].astype(BF16), sa[p].astype(BF16))).astype(BF16) for p in pairs]
            vals = [cat([sa[p], v_bd[p]], axis=0) for p in pairs]
            for p in pairs:
                arbr = cat([jnp.where(incl, m1[p][c2:, :c2], 0.0), jnp.where(incl, m1[p][c2:, c2:], 0.0)],
                           axis=1).astype(BF16)
                y_bd = gs[p][c2:] + _dot(arbr, vals[p])
                y_ref[rows, cols[p]] = y_bd[:ch] + y_bd[ch:]
            for p in pairs:
                upd = _dot_tn(vals[p], cat([by_head(repe_r[rows, cols[p]]), by_head(ke_r[rows, cols[p]])],
                                           axis=0))
                state_ref[p] = s_old[p] * pc_r[8 * c:8 * c + 1, cols[p]] + upd

        row0 = lax.broadcasted_iota(jnp.int32, (tb, 1), 0) == 0
        env = {}

        def shift(x, cref, slot, mu):
            prev = jnp.where(row0, cref[slot:slot + 1, :], pltpu.roll(x, 1, axis=0))
            cref[slot:slot + 1, :] = x[tb - 1:tb, :]
            return x + (prev - x) * mu

        def stage_shift():
            env["r"] = shift(r_ref[0].astype(F32), carry_ref, 0, mu_ref[0:1, :])
            env["k"] = shift(k_ref[0].astype(F32), carry_ref, 1, mu_ref[1:2, :])
            v = shift(v_ref[0].astype(F32), carry_ref, 2, mu_ref[2:3, :])
            g = shift(g_ref[0].astype(F32), carry_ref, 3, mu_ref[3:4, :])
            lo = shift(lo_ref[0].astype(F32), carrylo_ref, 0, mulo_ref[...])
            env["v"] = v
            vv_w[...] = v.astype(BF16)
            sg_w[...] = _silu(g)
            x = w0_ref[...] + _dot(jnp.tanh(lo[:, :RWKV_LORA]).astype(BF16), w2_ref[...])
            y = -x
            softplus = jnp.maximum(y, 0.0) + jnp.log(1.0 + jnp.exp(-jnp.abs(y)))
            env["ld"] = -jnp.exp(-softplus - 0.5)
            env["a"] = _sigmoid(a0_ref[...] + _dot(lo[:, RWKV_LORA:].astype(BF16), a2_ref[...]))

        def stage_keys():
            r, k, v, a = env["r"], env["k"], env["v"], env["a"]
            kk = k * kk_ref[...]
            k2 = k * (1.0 + (a - 1.0) * ka_ref[...])
            sums = headsum(cat([kk * kk, r * k2 * rk_ref[...]], axis=0))
            kk = kk * lax.rsqrt(jnp.maximum(sums[:tb], 1e-24))
            bonus_w[...] = sums[tb:] * v
            env["kk"], env["k2"], env["rep"] = kk, k2, kk * a

        def stage_decay():
            ld = env["ld"]
            ri = lax.broadcasted_iota(jnp.int32, (tb, tb), 0)
            ci = lax.broadcasted_iota(jnp.int32, (tb, tb), 1)
            lincl = jnp.where(((ri // ch) == (ci // ch)) & (ci <= ri), 1.0, 0.0).astype(BF16)
            hi = ld.astype(BF16)
            r1 = ld - hi.astype(F32)
            mid = r1.astype(BF16)
            low = (r1 - mid.astype(F32)).astype(BF16)
            parts = _dot(lincl, cat([hi, mid, low], axis=1))
            env["cum"] = parts[:, :width] + parts[:, width:2 * width] + parts[:, 2 * width:]

        def stage_operands():
            r, kk, k2, rep, ld, cum = (env[n] for n in ("r", "kk", "k2", "rep", "ld", "cum"))
            pinv = jnp.exp(-cum)
            pend = []
            for s in range(nch):
                pc = jnp.exp(cum[(s + 1) * ch - 1:(s + 1) * ch, :])
                pc_w[8 * s:8 * s + 8, :] = jnp.broadcast_to(pc, (8, width))
                pend.append(pc * pinv[s * ch:(s + 1) * ch, :])
            pend = cat(pend, axis=0)
            rx_w[...] = (r * jnp.exp(cum)).astype(BF16)
            remx_w[...] = (-kk * jnp.exp(cum - ld)).astype(BF16)
            repd_w[...] = (rep * pinv).astype(BF16)
            kd_w[...] = (k2 * pinv).astype(BF16)
            repe_w[...] = (rep * pend).astype(BF16)
            ke_w[...] = (k2 * pend).astype(BF16)

        stages = [stage_shift, stage_keys, stage_decay, stage_operands]
        if rd is None:
            for stage in stages:
                stage()
            return
        for c in range(nch):
            chunk(c)
            if c < len(stages):
                stages[c]()
        for stage in stages[nch:]:
            stage()

        yv = y_ref[...]
        inv_n = 1.0 / RWKV_HEAD
        mean = headsum(yv) * inv_n
        dlt = yv - mean
        var = headsum(dlt * dlt) * inv_n
        yn = dlt * lax.rsqrt(var + LNX_EPS) * lng_ref[...] + lnb_ref[...]
        o_ref[0] = ((yn + bonus_r[...]) * sg_r[...]).astype(o_ref.dtype)

    @pl.when(step_id == 0)
    def _():
        run(set_a, None)

    @pl.when((step_id > 0) & (step_id % 2 == 0))
    def _():
        run(set_a, set_b)

    @pl.when(step_id % 2 == 1)
    def _():
        run(set_b, set_a)


def _rwkv_call(p, params, layer, tb, ch):
    bsz, seq, _ = p.shape
    width = params[2].shape[-1]
    head_of = jnp.arange(width) // RWKV_HEAD
    hsum = (head_of[:, None] == jnp.arange(LANES)[None, :]).astype(BF16)
    hexp = jnp.concatenate([hsum.T, hsum.T], axis=0)
    npair = width // LANES
    nblk = seq // tb
    lo_blk = p.shape[-1] // LANES - 1
    act = lambda j: pl.BlockSpec((1, tb, width), lambda b, s, j=j: (b, jnp.minimum(s, nblk - 1), j))
    bf = lambda: pltpu.VMEM((tb, width), BF16)
    f32 = lambda: pltpu.VMEM((tb, width), F32)
    buf_set = lambda: [bf(), bf(), bf(), bf(), bf(), bf(), bf(),
                       pltpu.VMEM((8 * (tb // ch), width), F32), f32(), f32()]
    return pl.pallas_call(
        functools.partial(_rwkv_kernel, tb=tb, ch=ch),
        grid=(bsz, nblk + 1),
        in_specs=[act(0), act(1), act(2), act(3),
                  pl.BlockSpec((1, tb, LANES), lambda b, s: (b, jnp.minimum(s, nblk - 1), lo_blk))]
                 + [_layer_resident(a, layer) for a in params]
                 + [_resident(hsum.shape), _resident(hexp.shape)],
        out_specs=pl.BlockSpec((1, tb, width), lambda b, s: (b, jnp.maximum(s - 1, 0), 0)),
        out_shape=jax.ShapeDtypeStruct((bsz, seq, width), BF16),
        scratch_shapes=[pltpu.VMEM((8, width), F32), pltpu.VMEM((8, LANES), F32),
                        pltpu.VMEM((npair, LANES, LANES), F32), f32()] + buf_set() + buf_set(),
        compiler_params=_params(("parallel", "arbitrary")),
        name="rwkv",
    )(p, p, p, p, p, *params, hsum, hexp)


def _s5_kernel(u_ref, gate_ref, bblk_ref, cblk_ref, dnr_ref, dni_ref, dpr_ref, dpi_ref,
               lre_ref, lim_ref, d_ref, gw_ref, gb_ref, o_ref, xr_ref, xi_ref, y_ref, *, ts, cs):
    @pl.when(pl.program_id(1) == 0)
    def _():
        xr_ref[...] = jnp.zeros_like(xr_ref)
        xi_ref[...] = jnp.zeros_like(xi_ref)

    ntile, ucols, scols2 = bblk_ref.shape
    scols = scols2 // 2
    nsub = ts // cs
    u_bf = u_ref[0]
    ri = lax.broadcasted_iota(jnp.int32, (ts, ts), 0)
    ci = lax.broadcasted_iota(jnp.int32, (ts, ts), 1)
    ltri = jnp.where(((ri // cs) == (ci // cs)) & (ci <= ri), 1.0, 0.0).astype(BF16)

    tiles = range(ntile)
    cols = [slice(t * scols, (t + 1) * scols) for t in tiles]
    cat = jnp.concatenate

    bu = [_dot(u_bf[:, t * ucols:(t + 1) * ucols], bblk_ref[t]) for t in tiles]

    def scaled_inputs(t):
        dnr, dni = dnr_ref[:, cols[t]], dni_ref[:, cols[t]]
        zr, zi = [], []
        for s in range(nsub):
            br = bu[t][s * cs:(s + 1) * cs, :scols]
            bi = bu[t][s * cs:(s + 1) * cs, scols:]
            zr.append(br * dnr - bi * dni)
            zi.append(br * dni + bi * dnr)
        return cat([cat(zr, axis=0), cat(zi, axis=0)], axis=1).astype(BF16)

    csum = [_dot(ltri, scaled_inputs(t)) for t in tiles]

    def states(t):
        sc = cols[t]
        dpr, dpi = dpr_ref[:, sc], dpi_ref[:, sc]
        lre, lim = lre_ref[:, sc], lim_ref[:, sc]
        pr, pi_ = xr_ref[0:1, sc], xi_ref[0:1, sc]
        xr, xi = [], []
        for s in range(nsub):
            ar = lre * pr - lim * pi_
            ai = lre * pi_ + lim * pr
            cr = csum[t][s * cs:(s + 1) * cs, :scols] + ar
            cim = csum[t][s * cs:(s + 1) * cs, scols:] + ai
            xs_r = cr * dpr - cim * dpi
            xs_i = cr * dpi + cim * dpr
            pr, pi_ = xs_r[cs - 1:cs, :], xs_i[cs - 1:cs, :]
            xr.append(xs_r)
            xi.append(xs_i)
        xr_ref[0:1, sc] = pr
        xi_ref[0:1, sc] = pi_
        return cat([cat(xr, axis=0), cat(xi, axis=0)], axis=1).astype(BF16)

    for t in tiles:
        y_ref[:, t * ucols:(t + 1) * ucols] = _dot(states(t), cblk_ref[t])

    y = y_ref[...] + d_ref[...] * u_bf.astype(F32)
    cdf = 0.5 * (1.0 + jnp.tanh(math.sqrt(2.0 / math.pi) * (y + 0.044715 * (y * y * y))))
    y = y * cdf
    glu = _dot(y.astype(BF16), gw_ref[...]) + gb_ref[...]
    o_ref[0] = (y * _sigmoid(glu) * _silu(gate_ref[0].astype(F32))).astype(o_ref.dtype)


def _s5_tables(lam_re, lam_im, log_dt, b_re, b_im, c_re, c_im, cs):
    g, n = lam_re.shape
    pch = b_re.shape[-1]
    nt = g // S5_GT
    dt = jnp.exp(log_dt)[:, None]
    mag = jnp.exp(lam_re * dt)
    e_re, e_im = mag * jnp.cos(lam_im * dt), mag * jnp.sin(lam_im * dt)
    den = lam_re * lam_re + lam_im * lam_im
    coef_re = ((e_re - 1.0) * lam_re + e_im * lam_im) / den
    coef_im = (e_im * lam_re - (e_re - 1.0) * lam_im) / den
    bb_re = coef_re[..., None] * b_re - coef_im[..., None] * b_im
    bb_im = coef_re[..., None] * b_im + coef_im[..., None] * b_re
    eye = jnp.eye(S5_GT, dtype=F32)

    def btile(bb):
        return jnp.einsum('tgnq,gh->tgqhn', bb.reshape(nt, S5_GT, n, pch), eye).reshape(
            nt, S5_GT * pch, S5_GT * n)

    def ctile(cc):
        return jnp.einsum('tgpn,gh->tgnhp', cc.reshape(nt, S5_GT, pch, n), eye).reshape(
            nt, S5_GT * n, S5_GT * pch)

    bblk = jnp.concatenate([btile(bb_re), btile(bb_im)], axis=-1).astype(BF16)
    cblk = jnp.concatenate([ctile(c_re), -ctile(c_im)], axis=1).astype(BF16)
    j = jnp.arange(cs, dtype=F32)[:, None, None]
    lr, li = (lam_re * dt)[None], (lam_im * dt)[None]
    flat = lambda t: t.reshape(t.shape[0], g * n)
    dpr, dpi = flat(jnp.exp(j * lr) * jnp.cos(j * li)), flat(jnp.exp(j * lr) * jnp.sin(j * li))
    dnr, dni = flat(jnp.exp(-j * lr) * jnp.cos(j * li)), flat(-jnp.exp(-j * lr) * jnp.sin(j * li))
    return bblk, cblk, dnr, dni, dpr, dpi, e_re.reshape(1, g * n), e_im.reshape(1, g * n)


def _s5_call(p, tables, layer, d, glu_w, glu_b, u_blk, gate_blk, ts, cs):
    bsz, seq, _ = p.shape
    width = d.shape[-1]
    bblk, cblk, dnr, dni, dpr, dpi, lre, lim = tables
    nstate = lre.shape[-1]
    res = lambda a: _layer_resident(a, layer)
    return pl.pallas_call(
        functools.partial(_s5_kernel, ts=ts, cs=cs),
        grid=(bsz, seq // ts),
        in_specs=[pl.BlockSpec((1, ts, width), lambda b, t: (b, t, u_blk)),
                  pl.BlockSpec((1, ts, width), lambda b, t: (b, t, gate_blk)),
                  res(bblk), res(cblk), res(dnr), res(dni), res(dpr), res(dpi), res(lre), res(lim),
                  res(d), res(glu_w), res(glu_b)],
        out_specs=pl.BlockSpec((1, ts, width), lambda b, t: (b, t, 0)),
        out_shape=jax.ShapeDtypeStruct((bsz, seq, width), BF16),
        scratch_shapes=[pltpu.VMEM((8, nstate), F32), pltpu.VMEM((8, nstate), F32),
                        pltpu.VMEM((ts, width), F32)],
        compiler_params=_params(("parallel", "arbitrary")),
        name="s5",
    )(p, p, bblk, cblk, dnr, dni, dpr, dpi, lre, lim, d, glu_w, glu_b)


def _rope128(x, cos, sin):
    lane = lax.broadcasted_iota(jnp.int32, x.shape, 1)
    half = QK_ROPE // 2
    partner = jnp.where(lane < half, pltpu.roll(x, LANES - half, axis=1), pltpu.roll(x, half, axis=1))
    return x * cos + partner * sin


def _qkv_kernel(cq_ref, ckv_ref, kpe_ref, cos_ref, sin_ref, qn_ref, kvn_ref, wq_ref, wkv_ref,
                q_ref, k_ref, v_ref, *, scale):
    def latent(ref, gain_ref):
        x = ref[0].astype(F32)
        ms = jnp.mean(x * x, axis=-1, keepdims=True)
        return (x * lax.rsqrt(ms + NORM_EPS) * gain_ref[...]).astype(BF16)

    cq = latent(cq_ref, qn_ref)
    ckv = latent(ckv_ref, kvn_ref)
    cos, sin = cos_ref[0], sin_ref[0]
    kpe = _rope128(kpe_ref[0].astype(F32), cos, sin).astype(BF16)
    hw = 2 * LANES
    for h in range(q_ref.shape[1]):
        qh = _dot(cq, wq_ref[:, h * hw:(h + 1) * hw]) * scale
        q_ref[0, h] = jnp.concatenate(
            [qh[:, :LANES], _rope128(qh[:, LANES:], cos, sin)], axis=1).astype(BF16)
        kvh = _dot(ckv, wkv_ref[:, h * hw:(h + 1) * hw])
        k_ref[0, h] = jnp.concatenate([kvh[:, :LANES].astype(BF16), kpe], axis=1)
        v_ref[0, h] = kvh[:, LANES:].astype(BF16)


def _qkv_call(p, cos, sin, q_norm, kv_norm, wq, wkv, layer, cq_blk, ckv_blk, kpe_blk, tm):
    bsz, seq, _ = p.shape
    nh = MLA_HEADS
    hw = 2 * LANES
    scale = math.log2(math.e) / math.sqrt(QK_NOPE + QK_ROPE)
    head_out = lambda w: pl.BlockSpec((1, nh, tm, w), lambda b, t: (b, 0, t, 0))
    return pl.pallas_call(
        functools.partial(_qkv_kernel, scale=scale),
        grid=(bsz, seq // tm),
        in_specs=[pl.BlockSpec((1, tm, Q_LORA), lambda b, t: (b, t, cq_blk)),
                  pl.BlockSpec((1, tm, KV_LORA), lambda b, t: (b, t, ckv_blk)),
                  pl.BlockSpec((1, tm, LANES), lambda b, t: (b, t, kpe_blk)),
                  pl.BlockSpec((1, tm, LANES), lambda b, t: (b, t, 0)),
                  pl.BlockSpec((1, tm, LANES), lambda b, t: (b, t, 0)),
                  _layer_resident(q_norm, layer), _layer_resident(kv_norm, layer),
                  _layer_resident(wq, layer), _layer_resident(wkv, layer)],
        out_specs=[head_out(hw), head_out(hw), head_out(V_HEAD)],
        out_shape=[jax.ShapeDtypeStruct((bsz, nh, seq, hw), BF16),
                   jax.ShapeDtypeStruct((bsz, nh, seq, hw), BF16),
                   jax.ShapeDtypeStruct((bsz, nh, seq, V_HEAD), BF16)],
        compiler_params=_params(("parallel", "parallel")),
        name="qkv",
    )(p, p, p, cos, sin, q_norm, kv_norm, wq, wkv)


def _attn_kernel(q_ref, k_ref, v_ref, o_ref, *, tq, tk):
    qi = pl.program_id(2)
    nsub = tq // tk
    vd = v_ref.shape[3]
    qs = [q_ref[0, 0, i * tk:(i + 1) * tk, :] for i in range(nsub)]
    diag = (lax.broadcasted_iota(jnp.int32, (tk, tk), 0)
            >= lax.broadcasted_iota(jnp.int32, (tk, tk), 1))

    def scores(j, kinds):
        rows = pl.ds(pl.multiple_of(j * tk, tk), tk)
        kb = k_ref[0, 0, rows, :]
        return [None if kinds[i] is None else _dot_nt(qs[i], kb) for i in range(nsub)]

    def update(j, carries, kinds, ss):
        rows = pl.ds(pl.multiple_of(j * tk, tk), tk)
        vb = jnp.concatenate([v_ref[0, 0, rows, :], jnp.ones((tk, vd), BF16)], axis=1)
        out = []
        for i in range(nsub):
            if kinds[i] is None:
                out.append(carries[i])
                continue
            m, acc = carries[i]
            s = jnp.where(diag, ss[i], -1e30) if kinds[i] else ss[i]
            m_new = jnp.maximum(m, jnp.max(s, axis=-1, keepdims=True))
            pexp = jnp.exp2(s - m_new).astype(BF16)
            acc = jnp.exp2(m - m_new) * acc + _dot(pexp, vb)
            out.append((m_new, acc))
        return tuple(out)

    def blocks(js, carries, kinds_list):
        ss = [scores(j, kinds) for j, kinds in zip(js, kinds_list)]
        for j, kinds, s in zip(js, kinds_list, ss):
            carries = update(j, carries, kinds, s)
        return carries

    init = tuple((jnp.full((tk, 1), -1e30, F32), jnp.zeros((tk, 2 * vd), F32)) for _ in range(nsub))
    nfull = qi * nsub
    visible = [[False] * nsub] * nsub
    carries = lax.fori_loop(
        0, qi, lambda jq, c: blocks([jq * nsub + d for d in range(nsub)], c, visible), init)
    tail = [[None if i < d else (i == d) for i in range(nsub)] for d in range(nsub)]
    carries = blocks([nfull + d for d in range(nsub)], carries, tail)
    for i in range(nsub):
        _, acc = carries[i]
        o_ref[0, i * tk:(i + 1) * tk, :] = (acc[:, :vd] / acc[:, vd:]).astype(o_ref.dtype)


def _attn_call(q, k, v, tq, tk):
    bsz, nh, seq, hw = q.shape
    vd = v.shape[3]
    return pl.pallas_call(
        functools.partial(_attn_kernel, tq=tq, tk=tk),
        grid=(bsz, nh, seq // tq),
        in_specs=[pl.BlockSpec((1, 1, tq, hw), lambda b, h, i: (b, h, i, 0)),
                  pl.BlockSpec((1, 1, seq, hw), lambda b, h, i: (b, h, 0, 0)),
                  pl.BlockSpec((1, 1, seq, vd), lambda b, h, i: (b, h, 0, 0))],
        out_specs=pl.BlockSpec((1, tq, vd), lambda b, h, i: (b, i, h)),
        out_shape=jax.ShapeDtypeStruct((bsz, seq, nh * vd), BF16),
        compiler_params=_params(("parallel", "parallel", "arbitrary")),
        name="attn",
    )(q, k, v)


def _rope_tables(positions):
    inv_freq = 1.0 / (ROPE_BASE ** (jnp.arange(0, QK_ROPE, 2, dtype=F32) / QK_ROPE))
    ang = positions.astype(F32)[..., None] * inv_freq
    cos, sin = jnp.cos(ang), jnp.sin(ang)
    zero = jnp.zeros_like(cos)
    return (jnp.concatenate([cos, cos, zero, zero], axis=-1),
            jnp.concatenate([-sin, sin, zero, zero], axis=-1))


def kernel(x, c, positions, ev_ada_w, ev_ada_b, ev_norm_pre, ev_norm_post, ev_w_in, ev_mu, ev_w0, ev_w2, ev_a0, ev_a2, ev_k_k, ev_k_a, ev_r_k, ev_lnx_g, ev_lnx_b, ev_lam_re, ev_lam_im, ev_log_dt, ev_b_re, ev_b_im, ev_c_re, ev_c_im, ev_d, ev_glu_w, ev_glu_b, ev_w_out, od_ada_w, od_ada_b, od_norm_pre, od_norm_post, od_w_in, od_q_norm, od_w_q_up, od_kv_norm, od_w_kv_up, od_w_out):
    bsz, seq, d = x.shape
    n_even, n_odd = ev_w_in.shape[0], od_w_in.shape[0]
    depth = n_even + n_odd
    rw = ev_w0.shape[-1]
    sw = ev_d.shape[-1]
    a_in = 4 * rw + 2 * RWKV_LORA
    h = x.astype(F32)

    c8 = jnp.zeros((ADA_ROWS, d), F32).at[:bsz].set(c.astype(F32))
    ev_ada = _ada_call(c8, ev_ada_w, ev_ada_b)
    od_ada = _ada_call(c8, od_ada_w, od_ada_b)
    cos, sin = _rope_tables(positions)

    ev_w_in_b = ev_w_in.astype(BF16)
    ev_in = ev_w_in.shape[2]
    ev_segments = ((0, 0, 4 * rw), (4 * rw, a_in, ev_in - a_in), (4 * rw + ev_in - a_in, 4 * rw, a_in - 4 * rw))
    ev_w_out_b, ev_glu_w_b = ev_w_out.astype(BF16), ev_glu_w.astype(BF16)
    s5_tables = jax.vmap(functools.partial(_s5_tables, cs=S5_CHUNK))(
        ev_lam_re, ev_lam_im, ev_log_dt, ev_b_re, ev_b_im, ev_c_re, ev_c_im)
    o2, o3 = Q_LORA + KV_LORA, Q_LORA + KV_LORA + QK_ROPE
    mw = od_w_in.shape[2] - o3
    od_w_in_p = jnp.concatenate([od_w_in[:, :, o3:], od_w_in[:, :, :o3],
                                 jnp.zeros((n_odd, d, LANES - QK_ROPE), od_w_in.dtype)],
                                axis=2).astype(BF16)
    nh = MLA_HEADS
    wq = od_w_q_up.reshape(n_odd, Q_LORA, nh, QK_NOPE + QK_ROPE)
    wq = jnp.pad(wq, ((0, 0), (0, 0), (0, 0), (0, 2 * LANES - QK_NOPE - QK_ROPE)))
    wq = wq.reshape(n_odd, Q_LORA, nh * 2 * LANES).astype(BF16)
    wkv = od_w_kv_up.astype(BF16)
    od_w_out_b = od_w_out.astype(BF16)

    rows = lambda a: a.reshape(a.shape[0], 1, a.shape[-1])
    rwkv_params = [ev_mu[:, :4 * rw].reshape(n_even, 4, rw), rows(ev_mu[:, 4 * rw:]),
                   rows(ev_w0), ev_w2.astype(BF16), rows(ev_a0), ev_a2.astype(BF16),
                   rows(ev_k_k), rows(ev_k_a), rows(ev_r_k), rows(ev_lnx_g), rows(ev_lnx_b)]
    ev_pre, ev_post, od_pre, od_post = (rows(a) for a in (ev_norm_pre, ev_norm_post,
                                                          od_norm_pre, od_norm_post))
    ev_d_r, ev_glu_b_r, od_qn, od_kvn = (rows(a) for a in (ev_d, ev_glu_b, od_q_norm, od_kv_norm))

    tm_out = min(512, seq)
    for i in range(depth):
        j = i // 2
        if i % 2 == 0:
            p = _norm_proj_call(h, ev_pre, ev_ada, ev_w_in_b, j, ev_segments,
                                tm=min(256, seq), tn=PROJ_TN)
            y_a = _rwkv_call(p, rwkv_params, j, tb=min(RWKV_TB, seq), ch=RWKV_CHUNK)
            y_b = _s5_call(p, s5_tables, j, ev_d_r, ev_glu_w_b, ev_glu_b_r,
                           u_blk=(4 * rw) // sw, gate_blk=(4 * rw) // sw + 1,
                           ts=min(S5_TS, seq), cs=S5_CHUNK)
            half = pl.BlockSpec((1, tm_out, rw), lambda b, t_: (b, t_, 0))
            h = _out_call(_out_even_kernel, [y_a, y_b], [half, half],
                          ev_w_out_b, j, ev_post, ev_ada, h, tm_out)
        else:
            p = _norm_proj_call(h, od_pre, od_ada, od_w_in_p, j, ((0, 0, od_w_in_p.shape[2]),),
                                tm=min(512, seq), tn=PROJ_TN)
            q, k, v = _qkv_call(p, cos, sin, od_qn, od_kvn, wq, wkv, j,
                                cq_blk=mw // Q_LORA, ckv_blk=mw // KV_LORA + 1,
                                kpe_blk=(mw + o2) // LANES, tm=min(512, seq))
            o = _attn_call(q, k, v, tq=min(4096, seq), tk=min(512, seq))
            full = pl.BlockSpec((1, tm_out, mw), lambda b, t_: (b, t_, 0))
            h = _out_call(_out_odd_kernel, [o, p], [full, full],
                          od_w_out_b, j, od_post, od_ada, h, tm_out)
    return h.astype(x.dtype)
```

```python
import functools
import math

import jax
import jax.numpy as jnp
from jax import lax
from jax.experimental import pallas as pl
from jax.experimental.pallas import tpu as pltpu

F32 = jnp.float32
BF16 = jnp.bfloat16

NORM_EPS = 1e-6
LNX_EPS = 64e-5
ROPE_BASE = 10000.0

RWKV_HEAD = 64
RWKV_LORA = 64
S5_GROUP = 16
S5_STATE = 64
MLA_HEADS = 16
QK_NOPE = 128
QK_ROPE = 64
V_HEAD = 128
Q_LORA = 512
KV_LORA = 512

LANES = 128
V7X_VMEM_LIMIT = 56 * 1024 * 1024

RWKV_CHUNK = 64
RWKV_TB = 256
S5_CHUNK = 16
S5_TS = 256
S5_GT = 16
PROJ_TN = 1024


def _params(sem, vmem=V7X_VMEM_LIMIT):
    return pltpu.CompilerParams(dimension_semantics=sem, vmem_limit_bytes=vmem)


def _sigmoid(x):
    return 1.0 / (1.0 + jnp.exp(-x))


def _silu(x):
    return x * _sigmoid(x)


def _dot(a, b):
    return jnp.dot(a, b, preferred_element_type=F32)


def _dot_nt(a, b):
    return lax.dot_general(a, b, (((1,), (1,)), ((), ())), preferred_element_type=F32)


def _dot_tn(a, b):
    return lax.dot_general(a, b, (((0,), (0,)), ((), ())), preferred_element_type=F32)


def _resident(shape):
    nd = len(shape)
    return pl.BlockSpec(shape, lambda *_: (0,) * nd, pipeline_mode=pl.Buffered(1))


def _layer_resident(stacked, layer):
    nd = stacked.ndim
    return pl.BlockSpec((None,) + stacked.shape[1:], lambda *_: (layer,) + (0,) * (nd - 1),
                        pipeline_mode=pl.Buffered(1))


ADA_PARTS = 3
ADA_ROWS = 8


def _ada_kernel(c_ref, w_ref, b_ref, o_ref):
    s = _silu(c_ref[...]).astype(BF16)
    o_ref[0, 0, :, 0, :] = _dot(s, w_ref[0].astype(BF16)) + b_ref[0]


def _ada_call(c8, w, b):
    nl, d, n3 = w.shape
    tn = d // 2
    per = d // tn
    return pl.pallas_call(
        _ada_kernel,
        grid=(nl, n3 // tn),
        in_specs=[pl.BlockSpec((ADA_ROWS, d), lambda l, n: (0, 0)),
                  pl.BlockSpec((1, d, tn), lambda l, n: (l, 0, n)),
                  pl.BlockSpec((1, 1, tn), lambda l, n: (l, 0, n))],
        out_specs=pl.BlockSpec((1, 1, ADA_ROWS, 1, tn), lambda l, n: (l, n // per, 0, 0, n % per)),
        out_shape=jax.ShapeDtypeStruct((nl, ADA_PARTS, ADA_ROWS, 1, d), F32),
        compiler_params=_params(("parallel", "parallel")),
        name="ada",
    )(c8, w, b.reshape(nl, 1, n3))


def _ada_row(ada, layer, part):
    d = ada.shape[-1]
    return pl.BlockSpec((None, None, None, 1, d), lambda b, *_: (layer, part, b, 0, 0))


def _norm_proj_kernel(h_ref, g_ref, sc_ref, sh_ref, w_ref, o_ref, *, tn, segments):
    x = h_ref[0]
    ms = jnp.mean(x * x, axis=-1, keepdims=True)
    z = (x * lax.rsqrt(ms + NORM_EPS) * g_ref[...]) * (1.0 + sc_ref[...]) + sh_ref[...]
    z = z.astype(BF16)
    for out0, w0, width in segments:
        for c0 in range(0, width, tn):
            cw = min(tn, width - c0)
            o_ref[0, :, out0 + c0:out0 + c0 + cw] = _dot(
                z, w_ref[:, w0 + c0:w0 + c0 + cw]).astype(o_ref.dtype)


def _norm_proj_call(h, g, ada, w, layer, segments, tm, tn):
    bsz, seq, d = h.shape
    n = w.shape[2]
    assert sum(width for _, _, width in segments) == n
    return pl.pallas_call(
        functools.partial(_norm_proj_kernel, tn=tn, segments=segments),
        grid=(bsz, seq // tm),
        in_specs=[pl.BlockSpec((1, tm, d), lambda b, t: (b, t, 0)),
                  _layer_resident(g, layer),
                  _ada_row(ada, layer, 1),
                  _ada_row(ada, layer, 0),
                  _layer_resident(w, layer)],
        out_specs=pl.BlockSpec((1, tm, n), lambda b, t: (b, t, 0)),
        out_shape=jax.ShapeDtypeStruct((bsz, seq, n), BF16),
        compiler_params=_params(("parallel", "parallel")),
        name="norm_proj",
    )(h, g, ada, ada, w)


def _finish(y, post_ref, gate_ref, h_ref, o_ref):
    ms = jnp.mean(y * y, axis=-1, keepdims=True)
    yn = y * lax.rsqrt(ms + NORM_EPS) * post_ref[...]
    o_ref[0] = h_ref[0] + gate_ref[...] * yn


def _out_even_kernel(ya_ref, yb_ref, w_ref, post_ref, gate_ref, h_ref, o_ref):
    half = ya_ref.shape[2]
    y = _dot(ya_ref[0], w_ref[:half, :]) + _dot(yb_ref[0], w_ref[half:, :])
    _finish(y, post_ref, gate_ref, h_ref, o_ref)


def _out_odd_kernel(o_in_ref, g_ref, w_ref, post_ref, gate_ref, h_ref, o_ref):
    yin = o_in_ref[0].astype(F32) * _silu(g_ref[0].astype(F32))
    y = _dot(yin.astype(BF16), w_ref[...])
    _finish(y, post_ref, gate_ref, h_ref, o_ref)


def _out_call(kernel, acts, act_specs, w, layer, post, ada, h, tm):
    bsz, seq, d = h.shape
    return pl.pallas_call(
        kernel,
        grid=(bsz, seq // tm),
        in_specs=act_specs + [
            _layer_resident(w, layer),
            _layer_resident(post, layer),
            _ada_row(ada, layer, 2),
            pl.BlockSpec((1, tm, d), lambda b, t: (b, t, 0))],
        out_specs=pl.BlockSpec((1, tm, d), lambda b, t: (b, t, 0)),
        out_shape=jax.ShapeDtypeStruct((bsz, seq, d), F32),
        compiler_params=_params(("parallel", "parallel")),
        name=kernel.__name__.strip("_"),
    )(*acts, w, post, ada, h)


def _rwkv_kernel(r_ref, k_ref, v_ref, g_ref, lo_ref,
                 mu_ref, mulo_ref, w0_ref, w2_ref, a0_ref, a2_ref, kk_ref, ka_ref, rk_ref,
                 lng_ref, lnb_ref, hsum_ref, hexp_ref,
                 o_ref,
                 carry_ref, carrylo_ref, state_ref, y_ref, *bufs, tb, ch):
    width = r_ref.shape[2]
    npair = width // LANES
    nch = tb // ch
    c2 = 2 * ch
    nsq = int(math.log2(ch)) - 1
    step_id = pl.program_id(1)
    set_a, set_b = bufs[:len(bufs) // 2], bufs[len(bufs) // 2:]

    @pl.when(step_id == 0)
    def _():
        carry_ref[...] = jnp.zeros_like(carry_ref)
        carrylo_ref[...] = jnp.zeros_like(carrylo_ref)
        state_ref[...] = jnp.zeros_like(state_ref)

    def run(wr, rd):
        rx_w, remx_w, repd_w, kd_w, repe_w, ke_w, vv_w, pc_w, bonus_w, sg_w = wr
        rx_r, remx_r, repd_r, kd_r, repe_r, ke_r, vv_r, pc_r, bonus_r, sg_r = rd or wr
        cat = jnp.concatenate

        def headsum(t):
            sums = _dot(t.astype(BF16), hsum_ref[...])
            hi = sums.astype(BF16)
            lo = (sums - hi.astype(F32)).astype(BF16)
            return _dot(cat([hi, lo], axis=1), hexp_ref[...])

        pi = lax.broadcasted_iota(jnp.int32, (c2, c2), 0)
        pj = lax.broadcasted_iota(jnp.int32, (c2, c2), 1)
        blk = (pi // ch) == (pj // ch)
        strict = blk & (pj < pi)
        incl = blk & (pj <= pi)
        m0 = lax.broadcasted_iota(jnp.int32, (ch, LANES), 1) < RWKV_HEAD

        def by_head(t):
            z = jnp.zeros_like(t)
            return cat([jnp.where(m0, t, z), jnp.where(m0, z, t)], axis=0)

        def chunk(c):
            rows = slice(c * ch, (c + 1) * ch)
            pairs = range(npair)
            cols = [slice(p * LANES, (p + 1) * LANES) for p in pairs]
            lhs = [cat([by_head(remx_r[rows, cols[p]]), by_head(rx_r[rows, cols[p]])], axis=0)
                   for p in pairs]
            m1 = [_dot_nt(lhs[p], cat([repd_r[rows, cols[p]]] * 2 + [kd_r[rows, cols[p]]] * 2, axis=0))
                  for p in pairs]
            s_old = [state_ref[p] for p in pairs]
            gs = [_dot_nt(lhs[p], s_old[p].astype(BF16)) for p in pairs]
            v_bd = [by_head(vv_r[rows, cols[p]]) for p in pairs]
            sa = [gs[p][:c2] + _dot(jnp.where(strict, m1[p][:c2, c2:], 0.0).astype(BF16), v_bd[p])
                  for p in pairs]
            pw = [jnp.where(strict, m1[p][:c2, :c2], 0.0) for p in pairs]
            for _i in range(nsq):
                both = [_dot(pw[p].astype(BF16), cat([pw[p], sa[p]], axis=1).astype(BF16)) for p in pairs]
                pw = [both[p][:, :c2] for p in pairs]
                sa = [sa[p] + both[p][:, c2:] for p in pairs]
            sa = [(sa[p] + _dot(pw[p].astype(BF16), sa[p].astype(BF16))).astype(BF16) for p in pairs]
            vals = [cat([sa[p], v_bd[p]], axis=0) for p in pairs]
            for p in pairs:
                arbr = cat([jnp.where(incl, m1[p][c2:, :c2], 0.0), jnp.where(incl, m1[p][c2:, c2:], 0.0)],
                           axis=1).astype(BF16)
                y_bd = gs[p][c2:] + _dot(arbr, vals[p])
                y_ref[rows, cols[p]] = y_bd[:ch] + y_bd[ch:]
            for p in pairs:
                upd = _dot_tn(vals[p], cat([by_head(repe_r[rows, cols[p]]), by_head(ke_r[rows, cols[p]])],
                                           axis=0))
                state_ref[p] = s_old[p] * pc_r[8 * c:8 * c + 1, cols[p]] + upd

        row0 = lax.broadcasted_iota(jnp.int32, (tb, 1), 0) == 0
        env = {}

        def shift(x, cref, slot, mu):
            prev = jnp.where(row0, cref[slot:slot + 1, :], pltpu.roll(x, 1, axis=0))
            cref[slot:slot + 1, :] = x[tb - 1:tb, :]
            return x + (prev - x) * mu

        def stage_shift():
            env["r"] = shift(r_ref[0].astype(F32), carry_ref, 0, mu_ref[0:1, :])
            env["k"] = shift(k_ref[0].astype(F32), carry_ref, 1, mu_ref[1:2, :])
            v = shift(v_ref[0].astype(F32), carry_ref, 2, mu_ref[2:3, :])
            g = shift(g_ref[0].astype(F32), carry_ref, 3, mu_ref[3:4, :])
            lo = shift(lo_ref[0].astype(F32), carrylo_ref, 0, mulo_ref[...])
            env["v"] = v
            vv_w[...] = v.astype(BF16)
            sg_w[...] = _silu(g)
            x = w0_ref[...] + _dot(jnp.tanh(lo[:, :RWKV_LORA]).astype(BF16), w2_ref[...])
            y = -x
            softplus = jnp.maximum(y, 0.0) + jnp.log(1.0 + jnp.exp(-jnp.abs(y)))
            env["ld"] = -jnp.exp(-softplus - 0.5)
            env["a"] = _sigmoid(a0_ref[...] + _dot(lo[:, RWKV_LORA:].astype(BF16), a2_ref[...]))

        def stage_keys():
            r, k, v, a = env["r"], env["k"], env["v"], env["a"]
            kk = k * kk_ref[...]
            k2 = k * (1.0 + (a - 1.0) * ka_ref[...])
            sums = headsum(cat([kk * kk, r * k2 * rk_ref[...]], axis=0))
            kk = kk * lax.rsqrt(jnp.maximum(sums[:tb], 1e-24))
            bonus_w[...] = sums[tb:] * v
            env["kk"], env["k2"], env["rep"] = kk, k2, kk * a

        def stage_decay():
            ld = env["ld"]
            ri = lax.broadcasted_iota(jnp.int32, (tb, tb), 0)
            ci = lax.broadcasted_iota(jnp.int32, (tb, tb), 1)
            lincl = jnp.where(((ri // ch) == (ci // ch)) & (ci <= ri), 1.0, 0.0).astype(BF16)
            hi = ld.astype(BF16)
            r1 = ld - hi.astype(F32)
            mid = r1.astype(BF16)
            low = (r1 - mid.astype(F32)).astype(BF16)
            parts = _dot(lincl, cat([hi, mid, low], axis=1))
            env["cum"] = parts[:, :width] + parts[:, width:2 * width] + parts[:, 2 * width:]

        def stage_operands():
            r, kk, k2, rep, ld, cum = (env[n] for n in ("r", "kk", "k2", "rep", "ld", "cum"))
            pinv = jnp.exp(-cum)
            pend = []
            for s in range(nch):
                pc = jnp.exp(cum[(s + 1) * ch - 1:(s + 1) * ch, :])
                pc_w[8 * s:8 * s + 8, :] = jnp.broadcast_to(pc, (8, width))
                pend.append(pc * pinv[s * ch:(s + 1) * ch, :])
            pend = cat(pend, axis=0)
            rx_w[...] = (r * jnp.exp(cum)).astype(BF16)
            remx_w[...] = (-kk * jnp.exp(cum - ld)).astype(BF16)
            repd_w[...] = (rep * pinv).astype(BF16)
            kd_w[...] = (k2 * pinv).astype(BF16)
            repe_w[...] = (rep * pend).astype(BF16)
            ke_w[...] = (k2 * pend).astype(BF16)

        stages = [stage_shift, stage_keys, stage_decay, stage_operands]
        if rd is None:
            for stage in stages:
                stage()
            return
        for c in range(nch):
            chunk(c)
            if c < len(stages):
                stages[c]()
        for stage in stages[nch:]:
            stage()

        yv = y_ref[...]
        inv_n = 1.0 / RWKV_HEAD
        mean = headsum(yv) * inv_n
        dlt = yv - mean
        var = headsum(dlt * dlt) * inv_n
        yn = dlt * lax.rsqrt(var + LNX_EPS) * lng_ref[...] + lnb_ref[...]
        o_ref[0] = ((yn + bonus_r[...]) * sg_r[...]).astype(o_ref.dtype)

    @pl.when(step_id == 0)
    def _():
        run(set_a, None)

    @pl.when((step_id > 0) & (step_id % 2 == 0))
    def _():
        run(set_a, set_b)

    @pl.when(step_id % 2 == 1)
    def _():
        run(set_b, set_a)


def _rwkv_call(p, params, layer, tb, ch):
    bsz, seq, _ = p.shape
    width = params[2].shape[-1]
    head_of = jnp.arange(width) // RWKV_HEAD
    hsum = (head_of[:, None] == jnp.arange(LANES)[None, :]).astype(BF16)
    hexp = jnp.concatenate([hsum.T, hsum.T], axis=0)
    npair = width // LANES
    nblk = seq // tb
    lo_blk = p.shape[-1] // LANES - 1
    act = lambda j: pl.BlockSpec((1, tb, width), lambda b, s, j=j: (b, jnp.minimum(s, nblk - 1), j))
    bf = lambda: pltpu.VMEM((tb, width), BF16)
    f32 = lambda: pltpu.VMEM((tb, width), F32)
    buf_set = lambda: [bf(), bf(), bf(), bf(), bf(), bf(), bf(),
                       pltpu.VMEM((8 * (tb // ch), width), F32), f32(), f32()]
    return pl.pallas_call(
        functools.partial(_rwkv_kernel, tb=tb, ch=ch),
        grid=(bsz, nblk + 1),
        in_specs=[act(0), act(1), act(2), act(3),
                  pl.BlockSpec((1, tb, LANES), lambda b, s: (b, jnp.minimum(s, nblk - 1), lo_blk))]
                 + [_layer_resident(a, layer) for a in params]
                 + [_resident(hsum.shape), _resident(hexp.shape)],
        out_specs=pl.BlockSpec((1, tb, width), lambda b, s: (b, jnp.maximum(s - 1, 0), 0)),
        out_shape=jax.ShapeDtypeStruct((bsz, seq, width), BF16),
        scratch_shapes=[pltpu.VMEM((8, width), F32), pltpu.VMEM((8, LANES), F32),
                        pltpu.VMEM((npair, LANES, LANES), F32), f32()] + buf_set() + buf_set(),
        compiler_params=_params(("parallel", "arbitrary")),
        name="rwkv",
    )(p, p, p, p, p, *params, hsum, hexp)


def _s5_kernel(u_ref, gate_ref, bblk_ref, cblk_ref, dnr_ref, dni_ref, dpr_ref, dpi_ref,
               lre_ref, lim_ref, d_ref, gw_ref, gb_ref, o_ref, xr_ref, xi_ref, y_ref, *, ts, cs):
    @pl.when(pl.program_id(1) == 0)
    def _():
        xr_ref[...] = jnp.zeros_like(xr_ref)
        xi_ref[...] = jnp.zeros_like(xi_ref)

    ntile, ucols, scols2 = bblk_ref.shape
    scols = scols2 // 2
    nsub = ts // cs
    u_bf = u_ref[0]
    ri = lax.broadcasted_iota(jnp.int32, (ts, ts), 0)
    ci = lax.broadcasted_iota(jnp.int32, (ts, ts), 1)
    ltri = jnp.where(((ri // cs) == (ci // cs)) & (ci <= ri), 1.0, 0.0).astype(BF16)

    tiles = range(ntile)
    cols = [slice(t * scols, (t + 1) * scols) for t in tiles]
    cat = jnp.concatenate

    bu = [_dot(u_bf[:, t * ucols:(t + 1) * ucols], bblk_ref[t]) for t in tiles]

    def scaled_inputs(t):
        dnr, dni = dnr_ref[:, cols[t]], dni_ref[:, cols[t]]
        zr, zi = [], []
        for s in range(nsub):
            br = bu[t][s * cs:(s + 1) * cs, :scols]
            bi = bu[t][s * cs:(s + 1) * cs, scols:]
            zr.append(br * dnr - bi * dni)
            zi.append(br * dni + bi * dnr)
        return cat([cat(zr, axis=0), cat(zi, axis=0)], axis=1).astype(BF16)

    csum = [_dot(ltri, scaled_inputs(t)) for t in tiles]

    def states(t):
        sc = cols[t]
        dpr, dpi = dpr_ref[:, sc], dpi_ref[:, sc]
        lre, lim = lre_ref[:, sc], lim_ref[:, sc]
        pr, pi_ = xr_ref[0:1, sc], xi_ref[0:1, sc]
        xr, xi = [], []
        for s in range(nsub):
            ar = lre * pr - lim * pi_
            ai = lre * pi_ + lim * pr
            cr = csum[t][s * cs:(s + 1) * cs, :scols] + ar
            cim = csum[t][s * cs:(s + 1) * cs, scols:] + ai
            xs_r = cr * dpr - cim * dpi
            xs_i = cr * dpi + cim * dpr
            pr, pi_ = xs_r[cs - 1:cs, :], xs_i[cs - 1:cs, :]
            xr.append(xs_r)
            xi.append(xs_i)
        xr_ref[0:1, sc] = pr
        xi_ref[0:1, sc] = pi_
        return cat([cat(xr, axis=0), cat(xi, axis=0)], axis=1).astype(BF16)

    for t in tiles:
        y_ref[:, t * ucols:(t + 1) * ucols] = _dot(states(t), cblk_ref[t])

    y = y_ref[...] + d_ref[...] * u_bf.astype(F32)
    cdf = 0.5 * (1.0 + jnp.tanh(math.sqrt(2.0 / math.pi) * (y + 0.044715 * (y * y * y))))
    y = y * cdf
    glu = _dot(y.astype(BF16), gw_ref[...]) + gb_ref[...]
    o_ref[0] = (y * _sigmoid(glu) * _silu(gate_ref[0].astype(F32))).astype(o_ref.dtype)


def _s5_tables(lam_re, lam_im, log_dt, b_re, b_im, c_re, c_im, cs):
    g, n = lam_re.shape
    pch = b_re.shape[-1]
    nt = g // S5_GT
    dt = jnp.exp(log_dt)[:, None]
    mag = jnp.exp(lam_re * dt)
    e_re, e_im = mag * jnp.cos(lam_im * dt), mag * jnp.sin(lam_im * dt)
    den = lam_re * lam_re + lam_im * lam_im
    coef_re = ((e_re - 1.0) * lam_re + e_im * lam_im) / den
    coef_im = (e_im * lam_re - (e_re - 1.0) * lam_im) / den
    bb_re = coef_re[..., None] * b_re - coef_im[..., None] * b_im
    bb_im = coef_re[..., None] * b_im + coef_im[..., None] * b_re
    eye = jnp.eye(S5_GT, dtype=F32)

    same_group = eye[None, :, None, :, None]

    def btile(bb):
        bt = bb.reshape(nt, S5_GT, n, pch).transpose(0, 1, 3, 2)
        return (bt[:, :, :, None, :] * same_group).reshape(nt, S5_GT * pch, S5_GT * n)

    def ctile(cc):
        ct = cc.reshape(nt, S5_GT, pch, n).transpose(0, 1, 3, 2)
        return (ct[:, :, :, None, :] * same_group).reshape(nt, S5_GT * n, S5_GT * pch)

    bblk = jnp.concatenate([btile(bb_re), btile(bb_im)], axis=-1).astype(BF16)
    cblk = jnp.concatenate([ctile(c_re), -ctile(c_im)], axis=1).astype(BF16)
    j = jnp.arange(cs, dtype=F32)[:, None, None]
    lr, li = (lam_re * dt)[None], (lam_im * dt)[None]
    flat = lambda t: t.reshape(t.shape[0], g * n)
    dpr, dpi = flat(jnp.exp(j * lr) * jnp.cos(j * li)), flat(jnp.exp(j * lr) * jnp.sin(j * li))
    dnr, dni = flat(jnp.exp(-j * lr) * jnp.cos(j * li)), flat(-jnp.exp(-j * lr) * jnp.sin(j * li))
    return bblk, cblk, dnr, dni, dpr, dpi, e_re.reshape(1, g * n), e_im.reshape(1, g * n)


def _s5_call(p, tables, layer, d, glu_w, glu_b, u_blk, gate_blk, ts, cs):
    bsz, seq, _ = p.shape
    width = d.shape[-1]
    bblk, cblk, dnr, dni, dpr, dpi, lre, lim = tables
    nstate = lre.shape[-1]
    res = lambda a: _layer_resident(a, layer)
    return pl.pallas_call(
        functools.partial(_s5_kernel, ts=ts, cs=cs),
        grid=(bsz, seq // ts),
        in_specs=[pl.BlockSpec((1, ts, width), lambda b, t: (b, t, u_blk)),
                  pl.BlockSpec((1, ts, width), lambda b, t: (b, t, gate_blk)),
                  res(bblk), res(cblk), res(dnr), res(dni), res(dpr), res(dpi), res(lre), res(lim),
                  res(d), res(glu_w), res(glu_b)],
        out_specs=pl.BlockSpec((1, ts, width), lambda b, t: (b, t, 0)),
        out_shape=jax.ShapeDtypeStruct((bsz, seq, width), BF16),
        scratch_shapes=[pltpu.VMEM((8, nstate), F32), pltpu.VMEM((8, nstate), F32),
                        pltpu.VMEM((ts, width), F32)],
        compiler_params=_params(("parallel", "arbitrary")),
        name="s5",
    )(p, p, bblk, cblk, dnr, dni, dpr, dpi, lre, lim, d, glu_w, glu_b)


def _rope128(x, cos, sin):
    lane = lax.broadcasted_iota(jnp.int32, x.shape, 1)
    half = QK_ROPE // 2
    partner = jnp.where(lane < half, pltpu.roll(x, LANES - half, axis=1), pltpu.roll(x, half, axis=1))
    return x * cos + partner * sin


def _qkv_kernel(cq_ref, ckv_ref, kpe_ref, cos_ref, sin_ref, qn_ref, kvn_ref, wq_ref, wkv_ref,
                q_ref, k_ref, v_ref, *, scale):
    def latent(ref, gain_ref):
        x = ref[0].astype(F32)
        ms = jnp.mean(x * x, axis=-1, keepdims=True)
        return (x * lax.rsqrt(ms + NORM_EPS) * gain_ref[...]).astype(BF16)

    cq = latent(cq_ref, qn_ref)
    ckv = latent(ckv_ref, kvn_ref)
    cos, sin = cos_ref[0], sin_ref[0]
    kpe = _rope128(kpe_ref[0].astype(F32), cos, sin).astype(BF16)
    hw = 2 * LANES
    for h in range(q_ref.shape[1]):
        qh = _dot(cq, wq_ref[:, h * hw:(h + 1) * hw]) * scale
        q_ref[0, h] = jnp.concatenate(
            [qh[:, :LANES], _rope128(qh[:, LANES:], cos, sin)], axis=1).astype(BF16)
        kvh = _dot(ckv, wkv_ref[:, h * hw:(h + 1) * hw])
        k_ref[0, h] = jnp.concatenate([kvh[:, :LANES].astype(BF16), kpe], axis=1)
        v_ref[0, h] = kvh[:, LANES:].astype(BF16)


def _qkv_call(p, cos, sin, q_norm, kv_norm, wq, wkv, layer, cq_blk, ckv_blk, kpe_blk, tm):
    bsz, seq, _ = p.shape
    nh = MLA_HEADS
    hw = 2 * LANES
    scale = math.log2(math.e) / math.sqrt(QK_NOPE + QK_ROPE)
    head_out = lambda w: pl.BlockSpec((1, nh, tm, w), lambda b, t: (b, 0, t, 0))
    return pl.pallas_call(
        functools.partial(_qkv_kernel, scale=scale),
        grid=(bsz, seq // tm),
        in_specs=[pl.BlockSpec((1, tm, Q_LORA), lambda b, t: (b, t, cq_blk)),
                  pl.BlockSpec((1, tm, KV_LORA), lambda b, t: (b, t, ckv_blk)),
                  pl.BlockSpec((1, tm, LANES), lambda b, t: (b, t, kpe_blk)),
                  pl.BlockSpec((1, tm, LANES), lambda b, t: (b, t, 0)),
                  pl.BlockSpec((1, tm, LANES), lambda b, t: (b, t, 0)),
                  _layer_resident(q_norm, layer), _layer_resident(kv_norm, layer),
                  _layer_resident(wq, layer), _layer_resident(wkv, layer)],
        out_specs=[head_out(hw), head_out(hw), head_out(V_HEAD)],
        out_shape=[jax.ShapeDtypeStruct((bsz, nh, seq, hw), BF16),
                   jax.ShapeDtypeStruct((bsz, nh, seq, hw), BF16),
                   jax.ShapeDtypeStruct((bsz, nh, seq, V_HEAD), BF16)],
        compiler_params=_params(("parallel", "parallel")),
        name="qkv",
    )(p, p, p, cos, sin, q_norm, kv_norm, wq, wkv)


def _attn_kernel(q_ref, k_ref, v_ref, o_ref, *, tq, tk):
    qi = pl.program_id(2)
    nsub = tq // tk
    vd = v_ref.shape[3]
    qs = [q_ref[0, 0, i * tk:(i + 1) * tk, :] for i in range(nsub)]
    diag = (lax.broadcasted_iota(jnp.int32, (tk, tk), 0)
            >= lax.broadcasted_iota(jnp.int32, (tk, tk), 1))

    def scores(j, kinds):
        rows = pl.ds(pl.multiple_of(j * tk, tk), tk)
        kb = k_ref[0, 0, rows, :]
        return [None if kinds[i] is None else _dot_nt(qs[i], kb) for i in range(nsub)]

    def update(j, carries, kinds, ss):
        rows = pl.ds(pl.multiple_of(j * tk, tk), tk)
        vb = jnp.concatenate([v_ref[0, 0, rows, :], jnp.ones((tk, vd), BF16)], axis=1)
        out = []
        for i in range(nsub):
            if kinds[i] is None:
                out.append(carries[i])
                continue
            m, acc = carries[i]
            s = jnp.where(diag, ss[i], -1e30) if kinds[i] else ss[i]
            m_new = jnp.maximum(m, jnp.max(s, axis=-1, keepdims=True))
            pexp = jnp.exp2(s - m_new).astype(BF16)
            acc = jnp.exp2(m - m_new) * acc + _dot(pexp, vb)
            out.append((m_new, acc))
        return tuple(out)

    def blocks(js, carries, kinds_list):
        ss = [scores(j, kinds) for j, kinds in zip(js, kinds_list)]
        for j, kinds, s in zip(js, kinds_list, ss):
            carries = update(j, carries, kinds, s)
        return carries

    init = tuple((jnp.full((tk, 1), -1e30, F32), jnp.zeros((tk, 2 * vd), F32)) for _ in range(nsub))
    nfull = qi * nsub
    visible = [[False] * nsub] * nsub
    carries = lax.fori_loop(
        0, qi, lambda jq, c: blocks([jq * nsub + d for d in range(nsub)], c, visible), init)
    tail = [[None if i < d else (i == d) for i in range(nsub)] for d in range(nsub)]
    carries = blocks([nfull + d for d in range(nsub)], carries, tail)
    for i in range(nsub):
        _, acc = carries[i]
        o_ref[0, i * tk:(i + 1) * tk, :] = (acc[:, :vd] / acc[:, vd:]).astype(o_ref.dtype)


def _attn_call(q, k, v, tq, tk):
    bsz, nh, seq, hw = q.shape
    vd = v.shape[3]
    return pl.pallas_call(
        functools.partial(_attn_kernel, tq=tq, tk=tk),
        grid=(bsz, nh, seq // tq),
        in_specs=[pl.BlockSpec((1, 1, tq, hw), lambda b, h, i: (b, h, i, 0)),
                  pl.BlockSpec((1, 1, seq, hw), lambda b, h, i: (b, h, 0, 0)),
                  pl.BlockSpec((1, 1, seq, vd), lambda b, h, i: (b, h, 0, 0))],
        out_specs=pl.BlockSpec((1, tq, vd), lambda b, h, i: (b, i, h)),
        out_shape=jax.ShapeDtypeStruct((bsz, seq, nh * vd), BF16),
        compiler_params=_params(("parallel", "parallel", "arbitrary")),
        name="attn",
    )(q, k, v)


def _rope_tables(positions):
    inv_freq = 1.0 / (ROPE_BASE ** (jnp.arange(0, QK_ROPE, 2, dtype=F32) / QK_ROPE))
    ang = positions.astype(F32)[..., None] * inv_freq
    cos, sin = jnp.cos(ang), jnp.sin(ang)
    zero = jnp.zeros_like(cos)
    return (jnp.concatenate([cos, cos, zero, zero], axis=-1),
            jnp.concatenate([-sin, sin, zero, zero], axis=-1))


def kernel(x, c, positions, ev_ada_w, ev_ada_b, ev_norm_pre, ev_norm_post, ev_w_in, ev_mu, ev_w0, ev_w2, ev_a0, ev_a2, ev_k_k, ev_k_a, ev_r_k, ev_lnx_g, ev_lnx_b, ev_lam_re, ev_lam_im, ev_log_dt, ev_b_re, ev_b_im, ev_c_re, ev_c_im, ev_d, ev_glu_w, ev_glu_b, ev_w_out, od_ada_w, od_ada_b, od_norm_pre, od_norm_post, od_w_in, od_q_norm, od_w_q_up, od_kv_norm, od_w_kv_up, od_w_out):
    bsz, seq, d = x.shape
    n_even, n_odd = ev_w_in.shape[0], od_w_in.shape[0]
    depth = n_even + n_odd
    rw = ev_w0.shape[-1]
    sw = ev_d.shape[-1]
    a_in = 4 * rw + 2 * RWKV_LORA
    h = x.astype(F32)

    c8 = jnp.zeros((ADA_ROWS, d), F32).at[:bsz].set(c.astype(F32))
    ev_ada = _ada_call(c8, ev_ada_w, ev_ada_b)
    od_ada = _ada_call(c8, od_ada_w, od_ada_b)
    cos, sin = _rope_tables(positions)

    ev_w_in_b = ev_w_in.astype(BF16)
    ev_in = ev_w_in.shape[2]
    ev_segments = ((0, 0, 4 * rw), (4 * rw, a_in, ev_in - a_in), (4 * rw + ev_in - a_in, 4 * rw, a_in - 4 * rw))
    ev_w_out_b, ev_glu_w_b = ev_w_out.astype(BF16), ev_glu_w.astype(BF16)
    s5_tables = jax.vmap(functools.partial(_s5_tables, cs=S5_CHUNK))(
        ev_lam_re, ev_lam_im, ev_log_dt, ev_b_re, ev_b_im, ev_c_re, ev_c_im)
    o2, o3 = Q_LORA + KV_LORA, Q_LORA + KV_LORA + QK_ROPE
    mw = od_w_in.shape[2] - o3
    od_w_in_b = od_w_in.astype(BF16)
    od_w_in_p = jnp.concatenate([od_w_in_b[:, :, o3:], od_w_in_b[:, :, :o3],
                                 jnp.zeros((n_odd, d, LANES - QK_ROPE), BF16)], axis=2)
    nh = MLA_HEADS
    wq = od_w_q_up.astype(BF16).reshape(n_odd, Q_LORA, nh, QK_NOPE + QK_ROPE)
    wq = jnp.pad(wq, ((0, 0), (0, 0), (0, 0), (0, 2 * LANES - QK_NOPE - QK_ROPE)))
    wq = wq.reshape(n_odd, Q_LORA, nh * 2 * LANES)
    wkv = od_w_kv_up.astype(BF16)
    od_w_out_b = od_w_out.astype(BF16)

    rows = lambda a: a.reshape(a.shape[0], 1, a.shape[-1])
    rwkv_params = [ev_mu[:, :4 * rw].reshape(n_even, 4, rw), rows(ev_mu[:, 4 * rw:]),
                   rows(ev_w0), ev_w2.astype(BF16), rows(ev_a0), ev_a2.astype(BF16),
                   rows(ev_k_k), rows(ev_k_a), rows(ev_r_k), rows(ev_lnx_g), rows(ev_lnx_b)]
    ev_pre, ev_post, od_pre, od_post = (rows(a) for a in (ev_norm_pre, ev_norm_post,
                                                          od_norm_pre, od_norm_post))
    ev_d_r, ev_glu_b_r, od_qn, od_kvn = (rows(a) for a in (ev_d, ev_glu_b, od_q_norm, od_kv_norm))

    tm_out = min(512, seq)
    for i in range(depth):
        j = i // 2
        if i % 2 == 0:
            p = _norm_proj_call(h, ev_pre, ev_ada, ev_w_in_b, j, ev_segments,
                                tm=min(256, seq), tn=PROJ_TN)
            y_a = _rwkv_call(p, rwkv_params, j, tb=min(RWKV_TB, seq), ch=RWKV_CHUNK)
            y_b = _s5_call(p, s5_tables, j, ev_d_r, ev_glu_w_b, ev_glu_b_r,
                           u_blk=(4 * rw) // sw, gate_blk=(4 * rw) // sw + 1,
                           ts=min(S5_TS, seq), cs=S5_CHUNK)
            half = pl.BlockSpec((1, tm_out, rw), lambda b, t_: (b, t_, 0))
            h = _out_call(_out_even_kernel, [y_a, y_b], [half, half],
                          ev_w_out_b, j, ev_post, ev_ada, h, tm_out)
        else:
            p = _norm_proj_call(h, od_pre, od_ada, od_w_in_p, j, ((0, 0, od_w_in_p.shape[2]),),
                                tm=min(512, seq), tn=PROJ_TN)
            q, k, v = _qkv_call(p, cos, sin, od_qn, od_kvn, wq, wkv, j,
                                cq_blk=mw // Q_LORA, ckv_blk=mw // KV_LORA + 1,
                                kpe_blk=(mw + o2) // LANES, tm=min(512, seq))
            o = _attn_call(q, k, v, tq=min(4096, seq), tk=min(512, seq))
            full = pl.BlockSpec((1, tm_out, mw), lambda b, t_: (b, t_, 0))
            h = _out_call(_out_odd_kernel, [o, p], [full, full],
                          od_w_out_b, j, od_post, od_ada, h, tm_out)
    return h.astype(x.dtype)
```

```python
import functools
import math

import jax
import jax.numpy as jnp
from jax import lax
from jax.experimental import pallas as pl
from jax.experimental.pallas import tpu as pltpu

F32 = jnp.float32
BF16 = jnp.bfloat16

NORM_EPS = 1e-6
LNX_EPS = 64e-5
ROPE_BASE = 10000.0

RWKV_HEAD = 64
RWKV_LORA = 64
S5_GROUP = 16
S5_STATE = 64
MLA_HEADS = 16
QK_NOPE = 128
QK_ROPE = 64
V_HEAD = 128
Q_LORA = 512
KV_LORA = 512

LANES = 128
V7X_VMEM_LIMIT = 56 * 1024 * 1024

RWKV_CHUNK = 64
RWKV_TB = 256
S5_CHUNK = 16
S5_TS = 256
S5_GT = 8
PROJ_TN = 1024


def _params(sem, vmem=V7X_VMEM_LIMIT):
    return pltpu.CompilerParams(dimension_semantics=sem, vmem_limit_bytes=vmem)


def _sigmoid(x):
    return 1.0 / (1.0 + jnp.exp(-x))


def _silu(x):
    return x * _sigmoid(x)


def _dot(a, b):
    return jnp.dot(a, b, preferred_element_type=F32)


def _dot_nt(a, b):
    return lax.dot_general(a, b, (((1,), (1,)), ((), ())), preferred_element_type=F32)


def _dot_tn(a, b):
    return lax.dot_general(a, b, (((0,), (0,)), ((), ())), preferred_element_type=F32)


def _resident(shape):
    nd = len(shape)
    return pl.BlockSpec(shape, lambda *_: (0,) * nd, pipeline_mode=pl.Buffered(1))


def _layer_resident(stacked, layer):
    nd = stacked.ndim
    return pl.BlockSpec((None,) + stacked.shape[1:], lambda *_: (layer,) + (0,) * (nd - 1),
                        pipeline_mode=pl.Buffered(1))


ADA_PARTS = 3
ADA_ROWS = 8


def _ada_kernel(c_ref, w_ref, b_ref, o_ref):
    s = _silu(c_ref[...]).astype(BF16)
    o_ref[0, 0, :, 0, :] = _dot(s, w_ref[0].astype(BF16)) + b_ref[0]


def _ada_call(c8, w, b):
    nl, d, n3 = w.shape
    tn = d // 2
    per = d // tn
    return pl.pallas_call(
        _ada_kernel,
        grid=(nl, n3 // tn),
        in_specs=[pl.BlockSpec((ADA_ROWS, d), lambda l, n: (0, 0)),
                  pl.BlockSpec((1, d, tn), lambda l, n: (l, 0, n)),
                  pl.BlockSpec((1, 1, tn), lambda l, n: (l, 0, n))],
        out_specs=pl.BlockSpec((1, 1, ADA_ROWS, 1, tn), lambda l, n: (l, n // per, 0, 0, n % per)),
        out_shape=jax.ShapeDtypeStruct((nl, ADA_PARTS, ADA_ROWS, 1, d), F32),
        compiler_params=_params(("parallel", "parallel")),
        name="ada",
    )(c8, w, b.reshape(nl, 1, n3))


def _ada_row(ada, layer, part):
    d = ada.shape[-1]
    return pl.BlockSpec((None, None, None, 1, d), lambda b, *_: (layer, part, b, 0, 0))


def _norm_proj_kernel(h_ref, g_ref, sc_ref, sh_ref, w_ref, o_ref, *, tn, segments):
    x = h_ref[0]
    ms = jnp.mean(x * x, axis=-1, keepdims=True)
    z = (x * lax.rsqrt(ms + NORM_EPS) * g_ref[...]) * (1.0 + sc_ref[...]) + sh_ref[...]
    z = z.astype(BF16)
    for out0, w0, width in segments:
        for c0 in range(0, width, tn):
            cw = min(tn, width - c0)
            o_ref[0, :, out0 + c0:out0 + c0 + cw] = _dot(
                z, w_ref[:, w0 + c0:w0 + c0 + cw]).astype(o_ref.dtype)


def _norm_proj_call(h, g, ada, w, layer, segments, tm, tn):
    bsz, seq, d = h.shape
    n = w.shape[2]
    assert sum(width for _, _, width in segments) == n
    return pl.pallas_call(
        functools.partial(_norm_proj_kernel, tn=tn, segments=segments),
        grid=(bsz, seq // tm),
        in_specs=[pl.BlockSpec((1, tm, d), lambda b, t: (b, t, 0)),
                  _layer_resident(g, layer),
                  _ada_row(ada, layer, 1),
                  _ada_row(ada, layer, 0),
                  _layer_resident(w, layer)],
        out_specs=pl.BlockSpec((1, tm, n), lambda b, t: (b, t, 0)),
        out_shape=jax.ShapeDtypeStruct((bsz, seq, n), BF16),
        compiler_params=_params(("parallel", "parallel")),
        name="norm_proj",
    )(h, g, ada, ada, w)


def _finish(y, post_ref, gate_ref, h_ref, o_ref):
    ms = jnp.mean(y * y, axis=-1, keepdims=True)
    yn = y * lax.rsqrt(ms + NORM_EPS) * post_ref[...]
    o_ref[0] = h_ref[0] + gate_ref[...] * yn


def _out_even_kernel(ya_ref, yb_ref, w_ref, post_ref, gate_ref, h_ref, o_ref):
    half = ya_ref.shape[2]
    y = _dot(ya_ref[0], w_ref[:half, :]) + _dot(yb_ref[0], w_ref[half:, :])
    _finish(y, post_ref, gate_ref, h_ref, o_ref)


def _out_odd_kernel(o_in_ref, g_ref, w_ref, post_ref, gate_ref, h_ref, o_ref):
    yin = o_in_ref[0].astype(F32) * _silu(g_ref[0].astype(F32))
    y = _dot(yin.astype(BF16), w_ref[...])
    _finish(y, post_ref, gate_ref, h_ref, o_ref)


def _out_call(kernel, acts, act_specs, w, layer, post, ada, h, tm):
    bsz, seq, d = h.shape
    return pl.pallas_call(
        kernel,
        grid=(bsz, seq // tm),
        in_specs=act_specs + [
            _layer_resident(w, layer),
            _layer_resident(post, layer),
            _ada_row(ada, layer, 2),
            pl.BlockSpec((1, tm, d), lambda b, t: (b, t, 0))],
        out_specs=pl.BlockSpec((1, tm, d), lambda b, t: (b, t, 0)),
        out_shape=jax.ShapeDtypeStruct((bsz, seq, d), F32),
        compiler_params=_params(("parallel", "parallel")),
        name=kernel.__name__.strip("_"),
    )(*acts, w, post, ada, h)


def _rwkv_kernel(r_ref, k_ref, v_ref, g_ref, lo_ref,
                 mu_ref, mulo_ref, w0_ref, w2_ref, a0_ref, a2_ref, kk_ref, ka_ref, rk_ref,
                 lng_ref, lnb_ref, hsum_ref, hexp_ref,
                 o_ref,
                 carry_ref, carrylo_ref, state_ref, y_ref, *bufs, tb, ch):
    width = r_ref.shape[2]
    npair = width // LANES
    nch = tb // ch
    c2 = 2 * ch
    nsq = int(math.log2(ch)) - 1
    step_id = pl.program_id(1)
    set_a, set_b = bufs[:len(bufs) // 2], bufs[len(bufs) // 2:]

    @pl.when(step_id == 0)
    def _():
        carry_ref[...] = jnp.zeros_like(carry_ref)
        carrylo_ref[...] = jnp.zeros_like(carrylo_ref)
        state_ref[...] = jnp.zeros_like(state_ref)

    def run(wr, rd):
        rx_w, remx_w, repd_w, kd_w, repe_w, ke_w, vv_w, pc_w, bonus_w, sg_w = wr
        rx_r, remx_r, repd_r, kd_r, repe_r, ke_r, vv_r, pc_r, bonus_r, sg_r = rd or wr
        cat = jnp.concatenate

        def headsum(t):
            sums = _dot(t.astype(BF16), hsum_ref[...])
            hi = sums.astype(BF16)
            lo = (sums - hi.astype(F32)).astype(BF16)
            return _dot(cat([hi, lo], axis=1), hexp_ref[...])

        pi = lax.broadcasted_iota(jnp.int32, (c2, c2), 0)
        pj = lax.broadcasted_iota(jnp.int32, (c2, c2), 1)
        blk = (pi // ch) == (pj // ch)
        strict = blk & (pj < pi)
        incl = blk & (pj <= pi)
        m0 = lax.broadcasted_iota(jnp.int32, (ch, LANES), 1) < RWKV_HEAD

        def by_head(t):
            z = jnp.zeros_like(t)
            return cat([jnp.where(m0, t, z), jnp.where(m0, z, t)], axis=0)

        def chunk(c):
            rows = slice(c * ch, (c + 1) * ch)
            pairs = range(npair)
            cols = [slice(p * LANES, (p + 1) * LANES) for p in pairs]
            lhs = [cat([by_head(remx_r[rows, cols[p]]), by_head(rx_r[rows, cols[p]])], axis=0)
                   for p in pairs]
            m1 = [_dot_nt(lhs[p], cat([repd_r[rows, cols[p]]] * 2 + [kd_r[rows, cols[p]]] * 2, axis=0))
                  for p in pairs]
            s_old = [state_ref[p] for p in pairs]
            gs = [_dot_nt(lhs[p], s_old[p].astype(BF16)) for p in pairs]
            v_bd = [by_head(vv_r[rows, cols[p]]) for p in pairs]
            sa = [gs[p][:c2] + _dot(jnp.where(strict, m1[p][:c2, c2:], 0.0).astype(BF16), v_bd[p])
                  for p in pairs]
            pw = [jnp.where(strict, m1[p][:c2, :c2], 0.0) for p in pairs]
            for _i in range(nsq):
                both = [_dot(pw[p].astype(BF16), cat([pw[p], sa[p]], axis=1).astype(BF16)) for p in pairs]
                pw = [both[p][:, :c2] for p in pairs]
                sa = [sa[p] + both[p][:, c2:] for p in pairs]
            sa = [(sa[p] + _dot(pw[p].astype(BF16), sa[p].astype(BF16))).astype(BF16) for p in pairs]
            vals = [cat([sa[p], v_bd[p]], axis=0) for p in pairs]
            for p in pairs:
                arbr = cat([jnp.where(incl, m1[p][c2:, :c2], 0.0), jnp.where(incl, m1[p][c2:, c2:], 0.0)],
                           axis=1).astype(BF16)
                y_bd = gs[p][c2:] + _dot(arbr, vals[p])
                y_ref[rows, cols[p]] = y_bd[:ch] + y_bd[ch:]
            for p in pairs:
                upd = _dot_tn(vals[p], cat([by_head(repe_r[rows, cols[p]]), by_head(ke_r[rows, cols[p]])],
                                           axis=0))
                state_ref[p] = s_old[p] * pc_r[8 * c:8 * c + 1, cols[p]] + upd

        row0 = lax.broadcasted_iota(jnp.int32, (tb, 1), 0) == 0
        env = {}

        def shift(x, cref, slot, mu):
            prev = jnp.where(row0, cref[slot:slot + 1, :], pltpu.roll(x, 1, axis=0))
            cref[slot:slot + 1, :] = x[tb - 1:tb, :]
            return x + (prev - x) * mu

        def stage_shift():
            env["r"] = shift(r_ref[0].astype(F32), carry_ref, 0, mu_ref[0:1, :])
            env["k"] = shift(k_ref[0].astype(F32), carry_ref, 1, mu_ref[1:2, :])
            v = shift(v_ref[0].astype(F32), carry_ref, 2, mu_ref[2:3, :])
            g = shift(g_ref[0].astype(F32), carry_ref, 3, mu_ref[3:4, :])
            lo = shift(lo_ref[0].astype(F32), carrylo_ref, 0, mulo_ref[...])
            env["v"] = v
            vv_w[...] = v.astype(BF16)
            sg_w[...] = _silu(g)
            x = w0_ref[...] + _dot(jnp.tanh(lo[:, :RWKV_LORA]).astype(BF16), w2_ref[...])
            y = -x
            softplus = jnp.maximum(y, 0.0) + jnp.log(1.0 + jnp.exp(-jnp.abs(y)))
            env["ld"] = -jnp.exp(-softplus - 0.5)
            env["a"] = _sigmoid(a0_ref[...] + _dot(lo[:, RWKV_LORA:].astype(BF16), a2_ref[...]))

        def stage_keys():
            r, k, v, a = env["r"], env["k"], env["v"], env["a"]
            kk = k * kk_ref[...]
            k2 = k * (1.0 + (a - 1.0) * ka_ref[...])
            sums = headsum(cat([kk * kk, r * k2 * rk_ref[...]], axis=0))
            kk = kk * lax.rsqrt(jnp.maximum(sums[:tb], 1e-24))
            bonus_w[...] = sums[tb:] * v
            env["kk"], env["k2"], env["rep"] = kk, k2, kk * a

        def stage_decay():
            ld = env["ld"]
            ri = lax.broadcasted_iota(jnp.int32, (tb, tb), 0)
            ci = lax.broadcasted_iota(jnp.int32, (tb, tb), 1)
            lincl = jnp.where(((ri // ch) == (ci // ch)) & (ci <= ri), 1.0, 0.0).astype(BF16)
            hi = ld.astype(BF16)
            r1 = ld - hi.astype(F32)
            mid = r1.astype(BF16)
            low = (r1 - mid.astype(F32)).astype(BF16)
            parts = _dot(lincl, cat([hi, mid, low], axis=1))
            env["cum"] = parts[:, :width] + parts[:, width:2 * width] + parts[:, 2 * width:]

        def stage_operands():
            r, kk, k2, rep, ld, cum = (env[n] for n in ("r", "kk", "k2", "rep", "ld", "cum"))
            pinv = jnp.exp(-cum)
            pend = []
            for s in range(nch):
                pc = jnp.exp(cum[(s + 1) * ch - 1:(s + 1) * ch, :])
                pc_w[8 * s:8 * s + 8, :] = jnp.broadcast_to(pc, (8, width))
                pend.append(pc * pinv[s * ch:(s + 1) * ch, :])
            pend = cat(pend, axis=0)
            rx_w[...] = (r * jnp.exp(cum)).astype(BF16)
            remx_w[...] = (-kk * jnp.exp(cum - ld)).astype(BF16)
            repd_w[...] = (rep * pinv).astype(BF16)
            kd_w[...] = (k2 * pinv).astype(BF16)
            repe_w[...] = (rep * pend).astype(BF16)
            ke_w[...] = (k2 * pend).astype(BF16)

        stages = [stage_shift, stage_keys, stage_decay, stage_operands]
        if rd is None:
            for stage in stages:
                stage()
            return
        for c in range(nch):
            chunk(c)
            if c < len(stages):
                stages[c]()
        for stage in stages[nch:]:
            stage()

        yv = y_ref[...]
        inv_n = 1.0 / RWKV_HEAD
        mean = headsum(yv) * inv_n
        dlt = yv - mean
        var = headsum(dlt * dlt) * inv_n
        yn = dlt * lax.rsqrt(var + LNX_EPS) * lng_ref[...] + lnb_ref[...]
        o_ref[0] = ((yn + bonus_r[...]) * sg_r[...]).astype(o_ref.dtype)

    @pl.when(step_id == 0)
    def _():
        run(set_a, None)

    @pl.when((step_id > 0) & (step_id % 2 == 0))
    def _():
        run(set_a, set_b)

    @pl.when(step_id % 2 == 1)
    def _():
        run(set_b, set_a)


def _rwkv_call(p, params, layer, tb, ch):
    bsz, seq, _ = p.shape
    width = params[2].shape[-1]
    head_of = jnp.arange(width) // RWKV_HEAD
    hsum = (head_of[:, None] == jnp.arange(LANES)[None, :]).astype(BF16)
    hexp = jnp.concatenate([hsum.T, hsum.T], axis=0)
    npair = width // LANES
    nblk = seq // tb
    lo_blk = p.shape[-1] // LANES - 1
    act = lambda j: pl.BlockSpec((1, tb, width), lambda b, s, j=j: (b, jnp.minimum(s, nblk - 1), j))
    bf = lambda: pltpu.VMEM((tb, width), BF16)
    f32 = lambda: pltpu.VMEM((tb, width), F32)
    buf_set = lambda: [bf(), bf(), bf(), bf(), bf(), bf(), bf(),
                       pltpu.VMEM((8 * (tb // ch), width), F32), f32(), f32()]
    return pl.pallas_call(
        functools.partial(_rwkv_kernel, tb=tb, ch=ch),
        grid=(bsz, nblk + 1),
        in_specs=[act(0), act(1), act(2), act(3),
                  pl.BlockSpec((1, tb, LANES), lambda b, s: (b, jnp.minimum(s, nblk - 1), lo_blk))]
                 + [_layer_resident(a, layer) for a in params]
                 + [_resident(hsum.shape), _resident(hexp.shape)],
        out_specs=pl.BlockSpec((1, tb, width), lambda b, s: (b, jnp.maximum(s - 1, 0), 0)),
        out_shape=jax.ShapeDtypeStruct((bsz, seq, width), BF16),
        scratch_shapes=[pltpu.VMEM((8, width), F32), pltpu.VMEM((8, LANES), F32),
                        pltpu.VMEM((npair, LANES, LANES), F32), f32()] + buf_set() + buf_set(),
        compiler_params=_params(("parallel", "arbitrary")),
        name="rwkv",
    )(p, p, p, p, p, *params, hsum, hexp)


def _s5_kernel(u_ref, gate_ref, bblk_ref, cblk_ref, dnr_ref, dni_ref, dpr_ref, dpi_ref,
               lre_ref, lim_ref, d_ref, gw_ref, gb_ref, o_ref, xr_ref, xi_ref, y_ref, *, ts, cs):
    @pl.when(pl.program_id(1) == 0)
    def _():
        xr_ref[...] = jnp.zeros_like(xr_ref)
        xi_ref[...] = jnp.zeros_like(xi_ref)

    ntile, ucols, scols2 = bblk_ref.shape
    scols = scols2 // 2
    nsub = ts // cs
    u_bf = u_ref[0]
    ri = lax.broadcasted_iota(jnp.int32, (ts, ts), 0)
    ci = lax.broadcasted_iota(jnp.int32, (ts, ts), 1)
    ltri = jnp.where(((ri // cs) == (ci // cs)) & (ci <= ri), 1.0, 0.0).astype(BF16)

    tiles = range(ntile)
    cols = [slice(t * scols, (t + 1) * scols) for t in tiles]
    cat = jnp.concatenate

    bu = [_dot(u_bf[:, t * ucols:(t + 1) * ucols], bblk_ref[t]) for t in tiles]

    def scaled_inputs(t):
        dnr, dni = dnr_ref[:, cols[t]], dni_ref[:, cols[t]]
        zr, zi = [], []
        for s in range(nsub):
            br = bu[t][s * cs:(s + 1) * cs, :scols]
            bi = bu[t][s * cs:(s + 1) * cs, scols:]
            zr.append(br * dnr - bi * dni)
            zi.append(br * dni + bi * dnr)
        return cat([cat(zr, axis=0), cat(zi, axis=0)], axis=1).astype(BF16)

    csum = [_dot(ltri, scaled_inputs(t)) for t in tiles]

    def states(t):
        sc = cols[t]
        dpr, dpi = dpr_ref[:, sc], dpi_ref[:, sc]
        lre, lim = lre_ref[:, sc], lim_ref[:, sc]
        pr, pi_ = xr_ref[0:1, sc], xi_ref[0:1, sc]
        xr, xi = [], []
        for s in range(nsub):
            ar = lre * pr - lim * pi_
            ai = lre * pi_ + lim * pr
            cr = csum[t][s * cs:(s + 1) * cs, :scols] + ar
            cim = csum[t][s * cs:(s + 1) * cs, scols:] + ai
            xs_r = cr * dpr - cim * dpi
            xs_i = cr * dpi + cim * dpr
            pr, pi_ = xs_r[cs - 1:cs, :], xs_i[cs - 1:cs, :]
            xr.append(xs_r)
            xi.append(xs_i)
        xr_ref[0:1, sc] = pr
        xi_ref[0:1, sc] = pi_
        return cat([cat(xr, axis=0), cat(xi, axis=0)], axis=1).astype(BF16)

    for t in tiles:
        y_ref[:, t * ucols:(t + 1) * ucols] = _dot(states(t), cblk_ref[t])

    y = y_ref[...] + d_ref[...] * u_bf.astype(F32)
    cdf = 0.5 * (1.0 + jnp.tanh(math.sqrt(2.0 / math.pi) * (y + 0.044715 * (y * y * y))))
    y = y * cdf
    glu = _dot(y.astype(BF16), gw_ref[...]) + gb_ref[...]
    o_ref[0] = (y * _sigmoid(glu) * _silu(gate_ref[0].astype(F32))).astype(o_ref.dtype)


def _s5_tables(lam_re, lam_im, log_dt, b_re, b_im, c_re, c_im, cs):
    g, n = lam_re.shape
    pch = b_re.shape[-1]
    nt = g // S5_GT
    dt = jnp.exp(log_dt)[:, None]
    mag = jnp.exp(lam_re * dt)
    e_re, e_im = mag * jnp.cos(lam_im * dt), mag * jnp.sin(lam_im * dt)
    den = lam_re * lam_re + lam_im * lam_im
    coef_re = ((e_re - 1.0) * lam_re + e_im * lam_im) / den
    coef_im = (e_im * lam_re - (e_re - 1.0) * lam_im) / den
    bb_re = coef_re[..., None] * b_re - coef_im[..., None] * b_im
    bb_im = coef_re[..., None] * b_im + coef_im[..., None] * b_re
    eye = jnp.eye(S5_GT, dtype=F32)

    def btile(bb):
        return jnp.einsum('tgnq,gh->tgqhn', bb.reshape(nt, S5_GT, n, pch), eye).reshape(
            nt, S5_GT * pch, S5_GT * n)

    def ctile(cc):
        return jnp.einsum('tgpn,gh->tgnhp', cc.reshape(nt, S5_GT, pch, n), eye).reshape(
            nt, S5_GT * n, S5_GT * pch)

    bblk = jnp.concatenate([btile(bb_re), btile(bb_im)], axis=-1).astype(BF16)
    cblk = jnp.concatenate([ctile(c_re), -ctile(c_im)], axis=1).astype(BF16)
    j = jnp.arange(cs, dtype=F32)[:, None, None]
    lr, li = (lam_re * dt)[None], (lam_im * dt)[None]
    flat = lambda t: t.reshape(t.shape[0], g * n)
    dpr, dpi = flat(jnp.exp(j * lr) * jnp.cos(j * li)), flat(jnp.exp(j * lr) * jnp.sin(j * li))
    dnr, dni = flat(jnp.exp(-j * lr) * jnp.cos(j * li)), flat(-jnp.exp(-j * lr) * jnp.sin(j * li))
    return bblk, cblk, dnr, dni, dpr, dpi, e_re.reshape(1, g * n), e_im.reshape(1, g * n)


def _s5_call(p, tables, layer, d, glu_w, glu_b, u_blk, gate_blk, ts, cs):
    bsz, seq, _ = p.shape
    width = d.shape[-1]
    bblk, cblk, dnr, dni, dpr, dpi, lre, lim = tables
    nstate = lre.shape[-1]
    res = lambda a: _layer_resident(a, layer)
    return pl.pallas_call(
        functools.partial(_s5_kernel, ts=ts, cs=cs),
        grid=(bsz, seq // ts),
        in_specs=[pl.BlockSpec((1, ts, width), lambda b, t: (b, t, u_blk)),
                  pl.BlockSpec((1, ts, width), lambda b, t: (b, t, gate_blk)),
                  res(bblk), res(cblk), res(dnr), res(dni), res(dpr), res(dpi), res(lre), res(lim),
                  res(d), res(glu_w), res(glu_b)],
        out_specs=pl.BlockSpec((1, ts, width), lambda b, t: (b, t, 0)),
        out_shape=jax.ShapeDtypeStruct((bsz, seq, width), BF16),
        scratch_shapes=[pltpu.VMEM((8, nstate), F32), pltpu.VMEM((8, nstate), F32),
                        pltpu.VMEM((ts, width), F32)],
        compiler_params=_params(("parallel", "arbitrary")),
        name="s5",
    )(p, p, bblk, cblk, dnr, dni, dpr, dpi, lre, lim, d, glu_w, glu_b)


def _rope128(x, cos, sin):
    lane = lax.broadcasted_iota(jnp.int32, x.shape, 1)
    half = QK_ROPE // 2
    partner = jnp.where(lane < half, pltpu.roll(x, LANES - half, axis=1), pltpu.roll(x, half, axis=1))
    return x * cos + partner * sin


def _qkv_kernel(cq_ref, ckv_ref, kpe_ref, cos_ref, sin_ref, qn_ref, kvn_ref, wq_ref, wkv_ref,
                q_ref, k_ref, v_ref, *, scale):
    def latent(ref, gain_ref):
        x = ref[0].astype(F32)
        ms = jnp.mean(x * x, axis=-1, keepdims=True)
        return (x * lax.rsqrt(ms + NORM_EPS) * gain_ref[...]).astype(BF16)

    cq = latent(cq_ref, qn_ref)
    ckv = latent(ckv_ref, kvn_ref)
    cos, sin = cos_ref[0], sin_ref[0]
    kpe = _rope128(kpe_ref[0].astype(F32), cos, sin).astype(BF16)
    hw = 2 * LANES
    for h in range(q_ref.shape[1]):
        qh = _dot(cq, wq_ref[:, h * hw:(h + 1) * hw]) * scale
        q_ref[0, h] = jnp.concatenate(
            [qh[:, :LANES], _rope128(qh[:, LANES:], cos, sin)], axis=1).astype(BF16)
        kvh = _dot(ckv, wkv_ref[:, h * hw:(h + 1) * hw])
        k_ref[0, h] = jnp.concatenate([kvh[:, :LANES].astype(BF16), kpe], axis=1)
        v_ref[0, h] = kvh[:, LANES:].astype(BF16)


def _qkv_call(p, cos, sin, q_norm, kv_norm, wq, wkv, layer, cq_blk, ckv_blk, kpe_blk, tm):
    bsz, seq, _ = p.shape
    nh = MLA_HEADS
    hw = 2 * LANES
    scale = math.log2(math.e) / math.sqrt(QK_NOPE + QK_ROPE)
    head_out = lambda w: pl.BlockSpec((1, nh, tm, w), lambda b, t: (b, 0, t, 0))
    return pl.pallas_call(
        functools.partial(_qkv_kernel, scale=scale),
        grid=(bsz, seq // tm),
        in_specs=[pl.BlockSpec((1, tm, Q_LORA), lambda b, t: (b, t, cq_blk)),
                  pl.BlockSpec((1, tm, KV_LORA), lambda b, t: (b, t, ckv_blk)),
                  pl.BlockSpec((1, tm, LANES), lambda b, t: (b, t, kpe_blk)),
                  pl.BlockSpec((1, tm, LANES), lambda b, t: (b, t, 0)),
                  pl.BlockSpec((1, tm, LANES), lambda b, t: (b, t, 0)),
                  _layer_resident(q_norm, layer), _layer_resident(kv_norm, layer),
                  _layer_resident(wq, layer), _layer_resident(wkv, layer)],
        out_specs=[head_out(hw), head_out(hw), head_out(V_HEAD)],
        out_shape=[jax.ShapeDtypeStruct((bsz, nh, seq, hw), BF16),
                   jax.ShapeDtypeStruct((bsz, nh, seq, hw), BF16),
                   jax.ShapeDtypeStruct((bsz, nh, seq, V_HEAD), BF16)],
        compiler_params=_params(("parallel", "parallel")),
        name="qkv",
    )(p, p, p, cos, sin, q_norm, kv_norm, wq, wkv)


def _attn_kernel(q_ref, k_ref, v_ref, o_ref, *, tq, tk):
    qi = pl.program_id(2)
    nsub = tq // tk
    vd = v_ref.shape[3]
    qs = [q_ref[0, 0, i * tk:(i + 1) * tk, :] for i in range(nsub)]
    diag = (lax.broadcasted_iota(jnp.int32, (tk, tk), 0)
            >= lax.broadcasted_iota(jnp.int32, (tk, tk), 1))

    def scores(j, kinds):
        rows = pl.ds(pl.multiple_of(j * tk, tk), tk)
        kb = k_ref[0, 0, rows, :]
        return [None if kinds[i] is None else _dot_nt(qs[i], kb) for i in range(nsub)]

    def update(j, carries, kinds, ss):
        rows = pl.ds(pl.multiple_of(j * tk, tk), tk)
        vb = jnp.concatenate([v_ref[0, 0, rows, :], jnp.ones((tk, vd), BF16)], axis=1)
        out = []
        for i in range(nsub):
            if kinds[i] is None:
                out.append(carries[i])
                continue
            m, acc = carries[i]
            s = jnp.where(diag, ss[i], -1e30) if kinds[i] else ss[i]
            m_new = jnp.maximum(m, jnp.max(s, axis=-1, keepdims=True))
            pexp = jnp.exp2(s - m_new).astype(BF16)
            acc = jnp.exp2(m - m_new) * acc + _dot(pexp, vb)
            out.append((m_new, acc))
        return tuple(out)

    def blocks(js, carries, kinds_list):
        ss = [scores(j, kinds) for j, kinds in zip(js, kinds_list)]
        for j, kinds, s in zip(js, kinds_list, ss):
            carries = update(j, carries, kinds, s)
        return carries

    init = tuple((jnp.full((tk, 1), -1e30, F32), jnp.zeros((tk, 2 * vd), F32)) for _ in range(nsub))
    nfull = qi * nsub
    visible = [[False] * nsub] * nsub
    carries = lax.fori_loop(
        0, qi, lambda jq, c: blocks([jq * nsub + d for d in range(nsub)], c, visible), init)
    tail = [[None if i < d else (i == d) for i in range(nsub)] for d in range(nsub)]
    carries = blocks([nfull + d for d in range(nsub)], carries, tail)
    for i in range(nsub):
        _, acc = carries[i]
        o_ref[0, i * tk:(i + 1) * tk, :] = (acc[:, :vd] / acc[:, vd:]).astype(o_ref.dtype)


def _attn_call(q, k, v, tq, tk):
    bsz, nh, seq, hw = q.shape
    vd = v.shape[3]
    return pl.pallas_call(
        functools.partial(_attn_kernel, tq=tq, tk=tk),
        grid=(bsz, nh, seq // tq),
        in_specs=[pl.BlockSpec((1, 1, tq, hw), lambda b, h, i: (b, h, i, 0)),
                  pl.BlockSpec((1, 1, seq, hw), lambda b, h, i: (b, h, 0, 0)),
                  pl.BlockSpec((1, 1, seq, vd), lambda b, h, i: (b, h, 0, 0))],
        out_specs=pl.BlockSpec((1, tq, vd), lambda b, h, i: (b, i, h)),
        out_shape=jax.ShapeDtypeStruct((bsz, seq, nh * vd), BF16),
        compiler_params=_params(("parallel", "parallel", "arbitrary")),
        name="attn",
    )(q, k, v)


def _rope_tables(positions):
    inv_freq = 1.0 / (ROPE_BASE ** (jnp.arange(0, QK_ROPE, 2, dtype=F32) / QK_ROPE))
    ang = positions.astype(F32)[..., None] * inv_freq
    cos, sin = jnp.cos(ang), jnp.sin(ang)
    zero = jnp.zeros_like(cos)
    return (jnp.concatenate([cos, cos, zero, zero], axis=-1),
            jnp.concatenate([-sin, sin, zero, zero], axis=-1))


def kernel(x, c, positions, ev_ada_w, ev_ada_b, ev_norm_pre, ev_norm_post, ev_w_in, ev_mu, ev_w0, ev_w2, ev_a0, ev_a2, ev_k_k, ev_k_a, ev_r_k, ev_lnx_g, ev_lnx_b, ev_lam_re, ev_lam_im, ev_log_dt, ev_b_re, ev_b_im, ev_c_re, ev_c_im, ev_d, ev_glu_w, ev_glu_b, ev_w_out, od_ada_w, od_ada_b, od_norm_pre, od_norm_post, od_w_in, od_q_norm, od_w_q_up, od_kv_norm, od_w_kv_up, od_w_out):
    bsz, seq, d = x.shape
    n_even, n_odd = ev_w_in.shape[0], od_w_in.shape[0]
    depth = n_even + n_odd
    rw = ev_w0.shape[-1]
    sw = ev_d.shape[-1]
    a_in = 4 * rw + 2 * RWKV_LORA
    h = x.astype(F32)

    c8 = jnp.zeros((ADA_ROWS, d), F32).at[:bsz].set(c.astype(F32))
    ev_ada = _ada_call(c8, ev_ada_w, ev_ada_b)
    od_ada = _ada_call(c8, od_ada_w, od_ada_b)
    cos, sin = _rope_tables(positions)

    ev_w_in_b = ev_w_in.astype(BF16)
    ev_in = ev_w_in.shape[2]
    ev_segments = ((0, 0, 4 * rw), (4 * rw, a_in, ev_in - a_in), (4 * rw + ev_in - a_in, 4 * rw, a_in - 4 * rw))
    ev_w_out_b, ev_glu_w_b = ev_w_out.astype(BF16), ev_glu_w.astype(BF16)
    s5_tables = jax.vmap(functools.partial(_s5_tables, cs=S5_CHUNK))(
        ev_lam_re, ev_lam_im, ev_log_dt, ev_b_re, ev_b_im, ev_c_re, ev_c_im)
    o2, o3 = Q_LORA + KV_LORA, Q_LORA + KV_LORA + QK_ROPE
    mw = od_w_in.shape[2] - o3
    od_w_in_p = jnp.concatenate([od_w_in[:, :, o3:], od_w_in[:, :, :o3],
                                 jnp.zeros((n_odd, d, LANES - QK_ROPE), od_w_in.dtype)],
                                axis=2).astype(BF16)
    nh = MLA_HEADS
    wq = od_w_q_up.reshape(n_odd, Q_LORA, nh, QK_NOPE + QK_ROPE)
    wq = jnp.pad(wq, ((0, 0), (0, 0), (0, 0), (0, 2 * LANES - QK_NOPE - QK_ROPE)))
    wq = wq.reshape(n_odd, Q_LORA, nh * 2 * LANES).astype(BF16)
    wkv = od_w_kv_up.astype(BF16)
    od_w_out_b = od_w_out.astype(BF16)

    rows = lambda a: a.reshape(a.shape[0], 1, a.shape[-1])
    rwkv_params = [ev_mu[:, :4 * rw].reshape(n_even, 4, rw), rows(ev_mu[:, 4 * rw:]),
                   rows(ev_w0), ev_w2.astype(BF16), rows(ev_a0), ev_a2.astype(BF16),
                   rows(ev_k_k), rows(ev_k_a), rows(ev_r_k), rows(ev_lnx_g), rows(ev_lnx_b)]
    ev_pre, ev_post, od_pre, od_post = (rows(a) for a in (ev_norm_pre, ev_norm_post,
                                                          od_norm_pre, od_norm_post))
    ev_d_r, ev_glu_b_r, od_qn, od_kvn = (rows(a) for a in (ev_d, ev_glu_b, od_q_norm, od_kv_norm))

    tm_out = min(512, seq)
    for i in range(depth):
        j = i // 2
        if i % 2 == 0:
            p = _norm_proj_call(h, ev_pre, ev_ada, ev_w_in_b, j, ev_segments,
                                tm=min(256, seq), tn=PROJ_TN)
            y_a = _rwkv_call(p, rwkv_params, j, tb=min(RWKV_TB, seq), ch=RWKV_CHUNK)
            y_b = _s5_call(p, s5_tables, j, ev_d_r, ev_glu_w_b, ev_glu_b_r,
                           u_blk=(4 * rw) // sw, gate_blk=(4 * rw) // sw + 1,
                           ts=min(S5_TS, seq), cs=S5_CHUNK)
            half = pl.BlockSpec((1, tm_out, rw), lambda b, t_: (b, t_, 0))
            h = _out_call(_out_even_kernel, [y_a, y_b], [half, half],
                          ev_w_out_b, j, ev_post, ev_ada, h, tm_out)
        else:
            p = _norm_proj_call(h, od_pre, od_ada, od_w_in_p, j, ((0, 0, od_w_in_p.shape[2]),),
                                tm=min(512, seq), tn=PROJ_TN)
            q, k, v = _qkv_call(p, cos, sin, od_qn, od_kvn, wq, wkv, j,
                                cq_blk=mw // Q_LORA, ckv_blk=mw // KV_LORA + 1,
                                kpe_blk=(mw + o2) // LANES, tm=min(512, seq))
            o = _attn_call(q, k, v, tq=min(4096, seq), tk=min(512, seq))
            full = pl.BlockSpec((1, tm_out, mw), lambda b, t_: (b, t_, 0))
            h = _out_call(_out_odd_kernel, [o, p], [full, full],
                          od_w_out_b, j, od_post, od_ada, h, tm_out)
    return h.astype(x.dtype)
```

```python
import functools
import math

import jax
import jax.numpy as jnp
from jax import lax
from jax.experimental import pallas as pl
from jax.experimental.pallas import tpu as pltpu

F32 = jnp.float32
BF16 = jnp.bfloat16

NORM_EPS = 1e-6
LNX_EPS = 64e-5
ROPE_BASE = 10000.0

RWKV_HEAD = 64
RWKV_LORA = 64
S5_GROUP = 16
S5_STATE = 64
MLA_HEADS = 16
QK_NOPE = 128
QK_ROPE = 64
V_HEAD = 128
Q_LORA = 512
KV_LORA = 512

LANES = 128
V7X_VMEM_LIMIT = 56 * 1024 * 1024

RWKV_CHUNK = 64
RWKV_TB = 256
S5_CHUNK = 16
S5_TS = 256
S5_GT = 8
PROJ_TN = 1024


def _params(sem, vmem=V7X_VMEM_LIMIT):
    return pltpu.CompilerParams(dimension_semantics=sem, vmem_limit_bytes=vmem)


def _sigmoid(x):
    return 1.0 / (1.0 + jnp.exp(-x))


def _silu(x):
    return x * _sigmoid(x)


def _dot(a, b):
    return jnp.dot(a, b, preferred_element_type=F32)


def _dot_nt(a, b):
    return lax.dot_general(a, b, (((1,), (1,)), ((), ())), preferred_element_type=F32)


def _dot_tn(a, b):
    return lax.dot_general(a, b, (((0,), (0,)), ((), ())), preferred_element_type=F32)


def _resident(shape):
    nd = len(shape)
    return pl.BlockSpec(shape, lambda *_: (0,) * nd, pipeline_mode=pl.Buffered(1))


def _layer_resident(stacked, layer):
    nd = stacked.ndim
    return pl.BlockSpec((None,) + stacked.shape[1:], lambda *_: (layer,) + (0,) * (nd - 1),
                        pipeline_mode=pl.Buffered(1))


ADA_PARTS = 3
ADA_ROWS = 8


def _ada_kernel(c_ref, w_ref, b_ref, o_ref):
    s = _silu(c_ref[...]).astype(BF16)
    o_ref[0, 0, :, 0, :] = _dot(s, w_ref[0].astype(BF16)) + b_ref[0]


def _ada_call(c8, w, b):
    nl, d, n3 = w.shape
    tn = d // 2
    per = d // tn
    return pl.pallas_call(
        _ada_kernel,
        grid=(nl, n3 // tn),
        in_specs=[pl.BlockSpec((ADA_ROWS, d), lambda l, n: (0, 0)),
                  pl.BlockSpec((1, d, tn), lambda l, n: (l, 0, n)),
                  pl.BlockSpec((1, 1, tn), lambda l, n: (l, 0, n))],
        out_specs=pl.BlockSpec((1, 1, ADA_ROWS, 1, tn), lambda l, n: (l, n // per, 0, 0, n % per)),
        out_shape=jax.ShapeDtypeStruct((nl, ADA_PARTS, ADA_ROWS, 1, d), F32),
        compiler_params=_params(("parallel", "parallel")),
        name="ada",
    )(c8, w, b.reshape(nl, 1, n3))


def _ada_row(ada, layer, part):
    d = ada.shape[-1]
    return pl.BlockSpec((None, None, None, 1, d), lambda b, *_: (layer, part, b, 0, 0))


def _norm_proj_kernel(h_ref, g_ref, sc_ref, sh_ref, w_ref, o_ref, *, tn, segments):
    x = h_ref[0]
    ms = jnp.mean(x * x, axis=-1, keepdims=True)
    z = (x * lax.rsqrt(ms + NORM_EPS) * g_ref[...]) * (1.0 + sc_ref[...]) + sh_ref[...]
    z = z.astype(BF16)
    for out0, w0, width in segments:
        for c0 in range(0, width, tn):
            cw = min(tn, width - c0)
            o_ref[0, :, out0 + c0:out0 + c0 + cw] = _dot(
                z, w_ref[:, w0 + c0:w0 + c0 + cw]).astype(o_ref.dtype)


def _norm_proj_call(h, g, ada, w, layer, segments, tm, tn):
    bsz, seq, d = h.shape
    n = w.shape[2]
    assert sum(width for _, _, width in segments) == n
    return pl.pallas_call(
        functools.partial(_norm_proj_kernel, tn=tn, segments=segments),
        grid=(bsz, seq // tm),
        in_specs=[pl.BlockSpec((1, tm, d), lambda b, t: (b, t, 0)),
                  _layer_resident(g, layer),
                  _ada_row(ada, layer, 1),
                  _ada_row(ada, layer, 0),
                  _layer_resident(w, layer)],
        out_specs=pl.BlockSpec((1, tm, n), lambda b, t: (b, t, 0)),
        out_shape=jax.ShapeDtypeStruct((bsz, seq, n), BF16),
        compiler_params=_params(("parallel", "parallel")),
        name="norm_proj",
    )(h, g, ada, ada, w)


def _finish(y, post_ref, gate_ref, h_ref, o_ref):
    ms = jnp.mean(y * y, axis=-1, keepdims=True)
    yn = y * lax.rsqrt(ms + NORM_EPS) * post_ref[...]
    o_ref[0] = h_ref[0] + gate_ref[...] * yn


def _out_even_kernel(ya_ref, yb_ref, w_ref, post_ref, gate_ref, h_ref, o_ref):
    half = ya_ref.shape[2]
    y = _dot(ya_ref[0], w_ref[:half, :]) + _dot(yb_ref[0], w_ref[half:, :])
    _finish(y, post_ref, gate_ref, h_ref, o_ref)


def _out_odd_kernel(o_in_ref, g_ref, w_ref, post_ref, gate_ref, h_ref, o_ref):
    yin = o_in_ref[0].astype(F32) * _silu(g_ref[0].astype(F32))
    y = _dot(yin.astype(BF16), w_ref[...])
    _finish(y, post_ref, gate_ref, h_ref, o_ref)


def _out_call(kernel, acts, act_specs, w, layer, post, ada, h, tm):
    bsz, seq, d = h.shape
    return pl.pallas_call(
        kernel,
        grid=(bsz, seq // tm),
        in_specs=act_specs + [
            _layer_resident(w, layer),
            _layer_resident(post, layer),
            _ada_row(ada, layer, 2),
            pl.BlockSpec((1, tm, d), lambda b, t: (b, t, 0))],
        out_specs=pl.BlockSpec((1, tm, d), lambda b, t: (b, t, 0)),
        out_shape=jax.ShapeDtypeStruct((bsz, seq, d), F32),
        compiler_params=_params(("parallel", "parallel")),
        name=kernel.__name__.strip("_"),
    )(*acts, w, post, ada, h)


def _rwkv_kernel(r_ref, k_ref, v_ref, g_ref, lo_ref,
                 mu_ref, mulo_ref, w0_ref, w2_ref, a0_ref, a2_ref, kk_ref, ka_ref, rk_ref,
                 lng_ref, lnb_ref, hsum_ref, hexp_ref,
                 o_ref,
                 carry_ref, carrylo_ref, state_ref, y_ref, *bufs, tb, ch):
    width = r_ref.shape[2]
    npair = width // LANES
    nch = tb // ch
    c2 = 2 * ch
    nsq = int(math.log2(ch)) - 1
    step_id = pl.program_id(1)
    set_a, set_b = bufs[:len(bufs) // 2], bufs[len(bufs) // 2:]

    @pl.when(step_id == 0)
    def _():
        carry_ref[...] = jnp.zeros_like(carry_ref)
        carrylo_ref[...] = jnp.zeros_like(carrylo_ref)
        state_ref[...] = jnp.zeros_like(state_ref)

    def run(wr, rd):
        rx_w, remx_w, repd_w, kd_w, repe_w, ke_w, vv_w, pc_w, bonus_w, sg_w = wr
        rx_r, remx_r, repd_r, kd_r, repe_r, ke_r, vv_r, pc_r, bonus_r, sg_r = rd or wr
        cat = jnp.concatenate

        halves = [slice(0, width // 2), slice(width // 2, width)]

        def headsum(t, cs):
            sums = _dot(t.astype(BF16), hsum_ref[cs, :])
            hi = sums.astype(BF16)
            lo = (sums - hi.astype(F32)).astype(BF16)
            return _dot(cat([hi, lo], axis=1), hexp_ref[:, cs])

        pi = lax.broadcasted_iota(jnp.int32, (c2, c2), 0)
        pj = lax.broadcasted_iota(jnp.int32, (c2, c2), 1)
        blk = (pi // ch) == (pj // ch)
        strict = blk & (pj < pi)
        incl = blk & (pj <= pi)
        m0 = lax.broadcasted_iota(jnp.int32, (ch, LANES), 1) < RWKV_HEAD

        def by_head(t):
            z = jnp.zeros_like(t)
            return cat([jnp.where(m0, t, z), jnp.where(m0, z, t)], axis=0)

        def chunk(c):
            rows = slice(c * ch, (c + 1) * ch)
            pairs = range(npair)
            cols = [slice(p * LANES, (p + 1) * LANES) for p in pairs]
            lhs = [cat([by_head(remx_r[rows, cols[p]]), by_head(rx_r[rows, cols[p]])], axis=0)
                   for p in pairs]
            m1 = [_dot_nt(lhs[p], cat([repd_r[rows, cols[p]]] * 2 + [kd_r[rows, cols[p]]] * 2, axis=0))
                  for p in pairs]
            s_old = [state_ref[p] for p in pairs]
            gs = [_dot_nt(lhs[p], s_old[p].astype(BF16)) for p in pairs]
            v_bd = [by_head(vv_r[rows, cols[p]]) for p in pairs]
            sa = [gs[p][:c2] + _dot(jnp.where(strict, m1[p][:c2, c2:], 0.0).astype(BF16), v_bd[p])
                  for p in pairs]
            pw = [jnp.where(strict, m1[p][:c2, :c2], 0.0) for p in pairs]
            for _i in range(nsq):
                both = [_dot(pw[p].astype(BF16), cat([pw[p], sa[p]], axis=1).astype(BF16)) for p in pairs]
                pw = [both[p][:, :c2] for p in pairs]
                sa = [sa[p] + both[p][:, c2:] for p in pairs]
            sa = [(sa[p] + _dot(pw[p].astype(BF16), sa[p].astype(BF16))).astype(BF16) for p in pairs]
            vals = [cat([sa[p], v_bd[p]], axis=0) for p in pairs]
            for p in pairs:
                arbr = cat([jnp.where(incl, m1[p][c2:, :c2], 0.0), jnp.where(incl, m1[p][c2:, c2:], 0.0)],
                           axis=1).astype(BF16)
                y_bd = gs[p][c2:] + _dot(arbr, vals[p])
                y_ref[rows, cols[p]] = y_bd[:ch] + y_bd[ch:]
            for p in pairs:
                upd = _dot_tn(vals[p], cat([by_head(repe_r[rows, cols[p]]), by_head(ke_r[rows, cols[p]])],
                                           axis=0))
                state_ref[p] = s_old[p] * pc_r[8 * c:8 * c + 1, cols[p]] + upd

        row0 = lax.broadcasted_iota(jnp.int32, (tb, 1), 0) == 0
        env = {}

        def shift(x, cref, slot, mu, cs):
            prev = jnp.where(row0, cref[slot:slot + 1, cs], pltpu.roll(x, 1, axis=0))
            cref[slot:slot + 1, cs] = x[tb - 1:tb, :]
            return x + (prev - x) * mu[:, cs]

        def stage_shift():
            lo = shift(lo_ref[0].astype(F32), carrylo_ref, 0, mulo_ref[...], slice(None))
            w_lo = jnp.tanh(lo[:, :RWKV_LORA]).astype(BF16)
            a_lo = lo[:, RWKV_LORA:].astype(BF16)
            for h, cs in enumerate(halves):
                env["r", h] = shift(r_ref[0, :, cs].astype(F32), carry_ref, 0, mu_ref[0:1, :], cs)
                env["k", h] = shift(k_ref[0, :, cs].astype(F32), carry_ref, 1, mu_ref[1:2, :], cs)
                v = shift(v_ref[0, :, cs].astype(F32), carry_ref, 2, mu_ref[2:3, :], cs)
                g = shift(g_ref[0, :, cs].astype(F32), carry_ref, 3, mu_ref[3:4, :], cs)
                env["v", h] = v
                vv_w[:, cs] = v.astype(BF16)
                sg_w[:, cs] = _silu(g)
                x = w0_ref[:, cs] + _dot(w_lo, w2_ref[:, cs])
                y = -x
                softplus = jnp.maximum(y, 0.0) + jnp.log(1.0 + jnp.exp(-jnp.abs(y)))
                env["ld", h] = -jnp.exp(-softplus - 0.5)
                env["a", h] = _sigmoid(a0_ref[:, cs] + _dot(a_lo, a2_ref[:, cs]))

        def stage_keys():
            for h, cs in enumerate(halves):
                r, k, v, a = env["r", h], env["k", h], env["v", h], env["a", h]
                kk = k * kk_ref[:, cs]
                k2 = k * (1.0 + (a - 1.0) * ka_ref[:, cs])
                sums = headsum(cat([kk * kk, r * k2 * rk_ref[:, cs]], axis=0), cs)
                kk = kk * lax.rsqrt(jnp.maximum(sums[:tb], 1e-24))
                bonus_w[:, cs] = sums[tb:] * v
                env["kk", h], env["k2", h], env["rep", h] = kk, k2, kk * a

        def stage_decay():
            ri = lax.broadcasted_iota(jnp.int32, (tb, tb), 0)
            ci = lax.broadcasted_iota(jnp.int32, (tb, tb), 1)
            lincl = jnp.where(((ri // ch) == (ci // ch)) & (ci <= ri), 1.0, 0.0).astype(BF16)
            hw = width // 2
            for h in range(len(halves)):
                ld = env["ld", h]
                hi = ld.astype(BF16)
                r1 = ld - hi.astype(F32)
                mid = r1.astype(BF16)
                low = (r1 - mid.astype(F32)).astype(BF16)
                parts = _dot(lincl, cat([hi, mid, low], axis=1))
                env["cum", h] = parts[:, :hw] + parts[:, hw:2 * hw] + parts[:, 2 * hw:]

        def stage_operands():
            for h, cs in enumerate(halves):
                r, kk, k2, rep, ld, cum = (env[n, h] for n in ("r", "kk", "k2", "rep", "ld", "cum"))
                pinv = jnp.exp(-cum)
                pend = []
                for s in range(nch):
                    pc = jnp.exp(cum[(s + 1) * ch - 1:(s + 1) * ch, :])
                    pc_w[8 * s:8 * s + 8, cs] = jnp.broadcast_to(pc, (8, width // 2))
                    pend.append(pc * pinv[s * ch:(s + 1) * ch, :])
                pend = cat(pend, axis=0)
                rx_w[:, cs] = (r * jnp.exp(cum)).astype(BF16)
                remx_w[:, cs] = (-kk * jnp.exp(cum - ld)).astype(BF16)
                repd_w[:, cs] = (rep * pinv).astype(BF16)
                kd_w[:, cs] = (k2 * pinv).astype(BF16)
                repe_w[:, cs] = (rep * pend).astype(BF16)
                ke_w[:, cs] = (k2 * pend).astype(BF16)

        stages = [stage_shift, stage_keys, stage_decay, stage_operands]
        if rd is None:
            for stage in stages:
                stage()
            return
        for c in range(nch):
            chunk(c)
            if c < len(stages):
                stages[c]()
        for stage in stages[nch:]:
            stage()

        inv_n = 1.0 / RWKV_HEAD
        for cs in halves:
            yv = y_ref[:, cs]
            mean = headsum(yv, cs) * inv_n
            dlt = yv - mean
            var = headsum(dlt * dlt, cs) * inv_n
            yn = dlt * lax.rsqrt(var + LNX_EPS) * lng_ref[:, cs] + lnb_ref[:, cs]
            o_ref[0, :, cs] = ((yn + bonus_r[:, cs]) * sg_r[:, cs]).astype(o_ref.dtype)

    @pl.when(step_id == 0)
    def _():
        run(set_a, None)

    @pl.when((step_id > 0) & (step_id % 2 == 0))
    def _():
        run(set_a, set_b)

    @pl.when(step_id % 2 == 1)
    def _():
        run(set_b, set_a)


def _rwkv_call(p, params, layer, tb, ch):
    bsz, seq, _ = p.shape
    width = params[2].shape[-1]
    head_of = jnp.arange(width) // RWKV_HEAD
    hsum = (head_of[:, None] == jnp.arange(LANES)[None, :]).astype(BF16)
    hexp = jnp.concatenate([hsum.T, hsum.T], axis=0)
    npair = width // LANES
    nblk = seq // tb
    lo_blk = p.shape[-1] // LANES - 1
    act = lambda j: pl.BlockSpec((1, tb, width), lambda b, s, j=j: (b, jnp.minimum(s, nblk - 1), j))
    bf = lambda: pltpu.VMEM((tb, width), BF16)
    f32 = lambda: pltpu.VMEM((tb, width), F32)
    buf_set = lambda: [bf(), bf(), bf(), bf(), bf(), bf(), bf(),
                       pltpu.VMEM((8 * (tb // ch), width), F32), f32(), f32()]
    return pl.pallas_call(
        functools.partial(_rwkv_kernel, tb=tb, ch=ch),
        grid=(bsz, nblk + 1),
        in_specs=[act(0), act(1), act(2), act(3),
                  pl.BlockSpec((1, tb, LANES), lambda b, s: (b, jnp.minimum(s, nblk - 1), lo_blk))]
                 + [_layer_resident(a, layer) for a in params]
                 + [_resident(hsum.shape), _resident(hexp.shape)],
        out_specs=pl.BlockSpec((1, tb, width), lambda b, s: (b, jnp.maximum(s - 1, 0), 0)),
        out_shape=jax.ShapeDtypeStruct((bsz, seq, width), BF16),
        scratch_shapes=[pltpu.VMEM((8, width), F32), pltpu.VMEM((8, LANES), F32),
                        pltpu.VMEM((npair, LANES, LANES), F32), f32()] + buf_set() + buf_set(),
        compiler_params=_params(("parallel", "arbitrary")),
        name="rwkv",
    )(p, p, p, p, p, *params, hsum, hexp)


def _s5_kernel(u_ref, gate_ref, bblk_ref, cblk_ref, dnr_ref, dni_ref, dpr_ref, dpi_ref,
               lre_ref, lim_ref, d_ref, gw_ref, gb_ref, o_ref, xr_ref, xi_ref, y_ref, *, ts, cs):
    @pl.when(pl.program_id(1) == 0)
    def _():
        xr_ref[...] = jnp.zeros_like(xr_ref)
        xi_ref[...] = jnp.zeros_like(xi_ref)

    ntile, ucols, scols2 = bblk_ref.shape
    scols = scols2 // 2
    nsub = ts // cs
    u_bf = u_ref[0]
    ri = lax.broadcasted_iota(jnp.int32, (ts, ts), 0)
    ci = lax.broadcasted_iota(jnp.int32, (ts, ts), 1)
    ltri = jnp.where(((ri // cs) == (ci // cs)) & (ci <= ri), 1.0, 0.0).astype(BF16)

    tiles = range(ntile)
    cols = [slice(t * scols, (t + 1) * scols) for t in tiles]
    cat = jnp.concatenate

    bu = [_dot(u_bf[:, t * ucols:(t + 1) * ucols], bblk_ref[t]) for t in tiles]

    def scaled_inputs(t):
        dnr, dni = dnr_ref[:, cols[t]], dni_ref[:, cols[t]]
        zr, zi = [], []
        for s in range(nsub):
            br = bu[t][s * cs:(s + 1) * cs, :scols]
            bi = bu[t][s * cs:(s + 1) * cs, scols:]
            zr.append(br * dnr - bi * dni)
            zi.append(br * dni + bi * dnr)
        return cat([cat(zr, axis=0), cat(zi, axis=0)], axis=1).astype(BF16)

    csum = [_dot(ltri, scaled_inputs(t)) for t in tiles]

    def states(t):
        sc = cols[t]
        dpr, dpi = dpr_ref[:, sc], dpi_ref[:, sc]
        lre, lim = lre_ref[:, sc], lim_ref[:, sc]
        pr, pi_ = xr_ref[0:1, sc], xi_ref[0:1, sc]
        xr, xi = [], []
        for s in range(nsub):
            ar = lre * pr - lim * pi_
            ai = lre * pi_ + lim * pr
            cr = csum[t][s * cs:(s + 1) * cs, :scols] + ar
            cim = csum[t][s * cs:(s + 1) * cs, scols:] + ai
            xs_r = cr * dpr - cim * dpi
            xs_i = cr * dpi + cim * dpr
            pr, pi_ = xs_r[cs - 1:cs, :], xs_i[cs - 1:cs, :]
            xr.append(xs_r)
            xi.append(xs_i)
        xr_ref[0:1, sc] = pr
        xi_ref[0:1, sc] = pi_
        return cat([cat(xr, axis=0), cat(xi, axis=0)], axis=1).astype(BF16)

    for t in tiles:
        y_ref[:, t * ucols:(t + 1) * ucols] = _dot(states(t), cblk_ref[t])

    y = y_ref[...] + d_ref[...] * u_bf.astype(F32)
    cdf = 0.5 * (1.0 + jnp.tanh(math.sqrt(2.0 / math.pi) * (y + 0.044715 * (y * y * y))))
    y = y * cdf
    glu = _dot(y.astype(BF16), gw_ref[...]) + gb_ref[...]
    o_ref[0] = (y * _sigmoid(glu) * _silu(gate_ref[0].astype(F32))).astype(o_ref.dtype)


def _s5_tables(lam_re, lam_im, log_dt, b_re, b_im, c_re, c_im, cs):
    g, n = lam_re.shape
    pch = b_re.shape[-1]
    nt = g // S5_GT
    dt = jnp.exp(log_dt)[:, None]
    mag = jnp.exp(lam_re * dt)
    e_re, e_im = mag * jnp.cos(lam_im * dt), mag * jnp.sin(lam_im * dt)
    den = lam_re * lam_re + lam_im * lam_im
    coef_re = ((e_re - 1.0) * lam_re + e_im * lam_im) / den
    coef_im = (e_im * lam_re - (e_re - 1.0) * lam_im) / den
    bb_re = coef_re[..., None] * b_re - coef_im[..., None] * b_im
    bb_im = coef_re[..., None] * b_im + coef_im[..., None] * b_re
    eye = jnp.eye(S5_GT, dtype=F32)

    def btile(bb):
        return jnp.einsum('tgnq,gh->tgqhn', bb.reshape(nt, S5_GT, n, pch), eye).reshape(
            nt, S5_GT * pch, S5_GT * n)

    def ctile(cc):
        return jnp.einsum('tgpn,gh->tgnhp', cc.reshape(nt, S5_GT, pch, n), eye).reshape(
            nt, S5_GT * n, S5_GT * pch)

    bblk = jnp.concatenate([btile(bb_re), btile(bb_im)], axis=-1).astype(BF16)
    cblk = jnp.concatenate([ctile(c_re), -ctile(c_im)], axis=1).astype(BF16)
    j = jnp.arange(cs, dtype=F32)[:, None, None]
    lr, li = (lam_re * dt)[None], (lam_im * dt)[None]
    flat = lambda t: t.reshape(t.shape[0], g * n)
    dpr, dpi = flat(jnp.exp(j * lr) * jnp.cos(j * li)), flat(jnp.exp(j * lr) * jnp.sin(j * li))
    dnr, dni = flat(jnp.exp(-j * lr) * jnp.cos(j * li)), flat(-jnp.exp(-j * lr) * jnp.sin(j * li))
    return bblk, cblk, dnr, dni, dpr, dpi, e_re.reshape(1, g * n), e_im.reshape(1, g * n)


def _s5_call(p, tables, layer, d, glu_w, glu_b, u_blk, gate_blk, ts, cs):
    bsz, seq, _ = p.shape
    width = d.shape[-1]
    bblk, cblk, dnr, dni, dpr, dpi, lre, lim = tables
    nstate = lre.shape[-1]
    res = lambda a: _layer_resident(a, layer)
    return pl.pallas_call(
        functools.partial(_s5_kernel, ts=ts, cs=cs),
        grid=(bsz, seq // ts),
        in_specs=[pl.BlockSpec((1, ts, width), lambda b, t: (b, t, u_blk)),
                  pl.BlockSpec((1, ts, width), lambda b, t: (b, t, gate_blk)),
                  res(bblk), res(cblk), res(dnr), res(dni), res(dpr), res(dpi), res(lre), res(lim),
                  res(d), res(glu_w), res(glu_b)],
        out_specs=pl.BlockSpec((1, ts, width), lambda b, t: (b, t, 0)),
        out_shape=jax.ShapeDtypeStruct((bsz, seq, width), BF16),
        scratch_shapes=[pltpu.VMEM((8, nstate), F32), pltpu.VMEM((8, nstate), F32),
                        pltpu.VMEM((ts, width), F32)],
        compiler_params=_params(("parallel", "arbitrary")),
        name="s5",
    )(p, p, bblk, cblk, dnr, dni, dpr, dpi, lre, lim, d, glu_w, glu_b)


def _rope128(x, cos, sin):
    lane = lax.broadcasted_iota(jnp.int32, x.shape, 1)
    half = QK_ROPE // 2
    partner = jnp.where(lane < half, pltpu.roll(x, LANES - half, axis=1), pltpu.roll(x, half, axis=1))
    return x * cos + partner * sin


def _qkv_kernel(cq_ref, ckv_ref, kpe_ref, cos_ref, sin_ref, qn_ref, kvn_ref, wq_ref, wkv_ref,
                q_ref, k_ref, v_ref, *, scale):
    def latent(ref, gain_ref):
        x = ref[0].astype(F32)
        ms = jnp.mean(x * x, axis=-1, keepdims=True)
        return (x * lax.rsqrt(ms + NORM_EPS) * gain_ref[...]).astype(BF16)

    cq = latent(cq_ref, qn_ref)
    ckv = latent(ckv_ref, kvn_ref)
    cos, sin = cos_ref[0], sin_ref[0]
    kpe = _rope128(kpe_ref[0].astype(F32), cos, sin).astype(BF16)
    hw = 2 * LANES
    for h in range(q_ref.shape[1]):
        qh = _dot(cq, wq_ref[:, h * hw:(h + 1) * hw]) * scale
        q_ref[0, h] = jnp.concatenate(
            [qh[:, :LANES], _rope128(qh[:, LANES:], cos, sin)], axis=1).astype(BF16)
        kvh = _dot(ckv, wkv_ref[:, h * hw:(h + 1) * hw])
        k_ref[0, h] = jnp.concatenate([kvh[:, :LANES].astype(BF16), kpe], axis=1)
        v_ref[0, h] = kvh[:, LANES:].astype(BF16)


def _qkv_call(p, cos, sin, q_norm, kv_norm, wq, wkv, layer, cq_blk, ckv_blk, kpe_blk, tm):
    bsz, seq, _ = p.shape
    nh = MLA_HEADS
    hw = 2 * LANES
    scale = math.log2(math.e) / math.sqrt(QK_NOPE + QK_ROPE)
    head_out = lambda w: pl.BlockSpec((1, nh, tm, w), lambda b, t: (b, 0, t, 0))
    return pl.pallas_call(
        functools.partial(_qkv_kernel, scale=scale),
        grid=(bsz, seq // tm),
        in_specs=[pl.BlockSpec((1, tm, Q_LORA), lambda b, t: (b, t, cq_blk)),
                  pl.BlockSpec((1, tm, KV_LORA), lambda b, t: (b, t, ckv_blk)),
                  pl.BlockSpec((1, tm, LANES), lambda b, t: (b, t, kpe_blk)),
                  pl.BlockSpec((1, tm, LANES), lambda b, t: (b, t, 0)),
                  pl.BlockSpec((1, tm, LANES), lambda b, t: (b, t, 0)),
                  _layer_resident(q_norm, layer), _layer_resident(kv_norm, layer),
                  _layer_resident(wq, layer), _layer_resident(wkv, layer)],
        out_specs=[head_out(hw), head_out(hw), head_out(V_HEAD)],
        out_shape=[jax.ShapeDtypeStruct((bsz, nh, seq, hw), BF16),
                   jax.ShapeDtypeStruct((bsz, nh, seq, hw), BF16),
                   jax.ShapeDtypeStruct((bsz, nh, seq, V_HEAD), BF16)],
        compiler_params=_params(("parallel", "parallel")),
        name="qkv",
    )(p, p, p, cos, sin, q_norm, kv_norm, wq, wkv)


def _attn_kernel(q_ref, k_ref, v_ref, o_ref, *, tq, tk):
    qi = pl.program_id(2)
    nsub = tq // tk
    vd = v_ref.shape[3]
    qs = [q_ref[0, 0, i * tk:(i + 1) * tk, :] for i in range(nsub)]
    diag = (lax.broadcasted_iota(jnp.int32, (tk, tk), 0)
            >= lax.broadcasted_iota(jnp.int32, (tk, tk), 1))

    def scores(j, kinds):
        rows = pl.ds(pl.multiple_of(j * tk, tk), tk)
        kb = k_ref[0, 0, rows, :]
        return [None if kinds[i] is None else _dot_nt(qs[i], kb) for i in range(nsub)]

    def update(j, carries, kinds, ss):
        rows = pl.ds(pl.multiple_of(j * tk, tk), tk)
        vb = jnp.concatenate([v_ref[0, 0, rows, :], jnp.ones((tk, vd), BF16)], axis=1)
        out = []
        for i in range(nsub):
            if kinds[i] is None:
                out.append(carries[i])
                continue
            m, acc = carries[i]
            s = jnp.where(diag, ss[i], -1e30) if kinds[i] else ss[i]
            m_new = jnp.maximum(m, jnp.max(s, axis=-1, keepdims=True))
            pexp = jnp.exp2(s - m_new).astype(BF16)
            acc = jnp.exp2(m - m_new) * acc + _dot(pexp, vb)
            out.append((m_new, acc))
        return tuple(out)

    def blocks(js, carries, kinds_list):
        ss = [scores(j, kinds) for j, kinds in zip(js, kinds_list)]
        for j, kinds, s in zip(js, kinds_list, ss):
            carries = update(j, carries, kinds, s)
        return carries

    init = tuple((jnp.full((tk, 1), -1e30, F32), jnp.zeros((tk, 2 * vd), F32)) for _ in range(nsub))
    nfull = qi * nsub
    visible = [[False] * nsub] * nsub
    carries = lax.fori_loop(
        0, qi, lambda jq, c: blocks([jq * nsub + d for d in range(nsub)], c, visible), init)
    tail = [[None if i < d else (i == d) for i in range(nsub)] for d in range(nsub)]
    carries = blocks([nfull + d for d in range(nsub)], carries, tail)
    for i in range(nsub):
        _, acc = carries[i]
        o_ref[0, i * tk:(i + 1) * tk, :] = (acc[:, :vd] / acc[:, vd:]).astype(o_ref.dtype)


def _attn_call(q, k, v, tq, tk):
    bsz, nh, seq, hw = q.shape
    vd = v.shape[3]
    return pl.pallas_call(
        functools.partial(_attn_kernel, tq=tq, tk=tk),
        grid=(bsz, nh, seq // tq),
        in_specs=[pl.BlockSpec((1, 1, tq, hw), lambda b, h, i: (b, h, i, 0)),
                  pl.BlockSpec((1, 1, seq, hw), lambda b, h, i: (b, h, 0, 0)),
                  pl.BlockSpec((1, 1, seq, vd), lambda b, h, i: (b, h, 0, 0))],
        out_specs=pl.BlockSpec((1, tq, vd), lambda b, h, i: (b, i, h)),
        out_shape=jax.ShapeDtypeStruct((bsz, seq, nh * vd), BF16),
        compiler_params=_params(("parallel", "parallel", "arbitrary")),
        name="attn",
    )(q, k, v)


def _rope_tables(positions):
    inv_freq = 1.0 / (ROPE_BASE ** (jnp.arange(0, QK_ROPE, 2, dtype=F32) / QK_ROPE))
    ang = positions.astype(F32)[..., None] * inv_freq
    cos, sin = jnp.cos(ang), jnp.sin(ang)
    zero = jnp.zeros_like(cos)
    return (jnp.concatenate([cos, cos, zero, zero], axis=-1),
            jnp.concatenate([-sin, sin, zero, zero], axis=-1))


def kernel(x, c, positions, ev_ada_w, ev_ada_b, ev_norm_pre, ev_norm_post, ev_w_in, ev_mu, ev_w0, ev_w2, ev_a0, ev_a2, ev_k_k, ev_k_a, ev_r_k, ev_lnx_g, ev_lnx_b, ev_lam_re, ev_lam_im, ev_log_dt, ev_b_re, ev_b_im, ev_c_re, ev_c_im, ev_d, ev_glu_w, ev_glu_b, ev_w_out, od_ada_w, od_ada_b, od_norm_pre, od_norm_post, od_w_in, od_q_norm, od_w_q_up, od_kv_norm, od_w_kv_up, od_w_out):
    bsz, seq, d = x.shape
    n_even, n_odd = ev_w_in.shape[0], od_w_in.shape[0]
    depth = n_even + n_odd
    rw = ev_w0.shape[-1]
    sw = ev_d.shape[-1]
    a_in = 4 * rw + 2 * RWKV_LORA
    h = x.astype(F32)

    c8 = jnp.zeros((ADA_ROWS, d), F32).at[:bsz].set(c.astype(F32))
    ev_ada = _ada_call(c8, ev_ada_w, ev_ada_b)
    od_ada = _ada_call(c8, od_ada_w, od_ada_b)
    cos, sin = _rope_tables(positions)

    ev_w_in_b = ev_w_in.astype(BF16)
    ev_in = ev_w_in.shape[2]
    ev_segments = ((0, 0, 4 * rw), (4 * rw, a_in, ev_in - a_in), (4 * rw + ev_in - a_in, 4 * rw, a_in - 4 * rw))
    ev_w_out_b, ev_glu_w_b = ev_w_out.astype(BF16), ev_glu_w.astype(BF16)
    s5_tables = jax.vmap(functools.partial(_s5_tables, cs=S5_CHUNK))(
        ev_lam_re, ev_lam_im, ev_log_dt, ev_b_re, ev_b_im, ev_c_re, ev_c_im)
    o2, o3 = Q_LORA + KV_LORA, Q_LORA + KV_LORA + QK_ROPE
    mw = od_w_in.shape[2] - o3
    od_w_in_p = jnp.concatenate([od_w_in[:, :, o3:], od_w_in[:, :, :o3],
                                 jnp.zeros((n_odd, d, LANES - QK_ROPE), od_w_in.dtype)],
                                axis=2).astype(BF16)
    nh = MLA_HEADS
    wq = od_w_q_up.reshape(n_odd, Q_LORA, nh, QK_NOPE + QK_ROPE)
    wq = jnp.pad(wq, ((0, 0), (0, 0), (0, 0), (0, 2 * LANES - QK_NOPE - QK_ROPE)))
    wq = wq.reshape(n_odd, Q_LORA, nh * 2 * LANES).astype(BF16)
    wkv = od_w_kv_up.astype(BF16)
    od_w_out_b = od_w_out.astype(BF16)

    rows = lambda a: a.reshape(a.shape[0], 1, a.shape[-1])
    rwkv_params = [ev_mu[:, :4 * rw].reshape(n_even, 4, rw), rows(ev_mu[:, 4 * rw:]),
                   rows(ev_w0), ev_w2.astype(BF16), rows(ev_a0), ev_a2.astype(BF16),
                   rows(ev_k_k), rows(ev_k_a), rows(ev_r_k), rows(ev_lnx_g), rows(ev_lnx_b)]
    ev_pre, ev_post, od_pre, od_post = (rows(a) for a in (ev_norm_pre, ev_norm_post,
                                                          od_norm_pre, od_norm_post))
    ev_d_r, ev_glu_b_r, od_qn, od_kvn = (rows(a) for a in (ev_d, ev_glu_b, od_q_norm, od_kv_norm))

    tm_out = min(512, seq)
    for i in range(depth):
        j = i // 2
        if i % 2 == 0:
            p = _norm_proj_call(h, ev_pre, ev_ada, ev_w_in_b, j, ev_segments,
                                tm=min(256, seq), tn=PROJ_TN)
            y_a = _rwkv_call(p, rwkv_params, j, tb=min(RWKV_TB, seq), ch=RWKV_CHUNK)
            y_b = _s5_call(p, s5_tables, j, ev_d_r, ev_glu_w_b, ev_glu_b_r,
                           u_blk=(4 * rw) // sw, gate_blk=(4 * rw) // sw + 1,
                           ts=min(S5_TS, seq), cs=S5_CHUNK)
            half = pl.BlockSpec((1, tm_out, rw), lambda b, t_: (b, t_, 0))
            h = _out_call(_out_even_kernel, [y_a, y_b], [half, half],
                          ev_w_out_b, j, ev_post, ev_ada, h, tm_out)
        else:
            p = _norm_proj_call(h, od_pre, od_ada, od_w_in_p, j, ((0, 0, od_w_in_p.shape[2]),),
                                tm=min(512, seq), tn=PROJ_TN)
            q, k, v = _qkv_call(p, cos, sin, od_qn, od_kvn, wq, wkv, j,
                                cq_blk=mw // Q_LORA, ckv_blk=mw // KV_LORA + 1,
                                kpe_blk=(mw + o2) // LANES, tm=min(512, seq))
            o = _attn_call(q, k, v, tq=min(4096, seq), tk=min(512, seq))
            full = pl.BlockSpec((1, tm_out, mw), lambda b, t_: (b, t_, 0))
            h = _out_call(_out_odd_kernel, [o, p], [full, full],
                          od_w_out_b, j, od_post, od_ada, h, tm_out)
    return h.astype(x.dtype)
```
